```python
import math
import jax, jax.numpy as jnp
from jax import lax
import numpy as np

D_MODEL = 1024
BATCH = 8
SEQ = 2048
DEPTH = 2

CTX_LEN = 256
GRID_W = 64
N_MIXERS = 2
N_LAYERS_A = (DEPTH + 1) // 2
N_LAYERS_B = DEPTH // 2
CHUNK = 64
EPS = 1e-6

HG_HEADS = 8
HG_DK = 128
HG_FDIM = HG_HEADS * HG_DK
HG_DV = D_MODEL // HG_HEADS
HG_IN = HG_FDIM + 2 * HG_FDIM + D_MODEL + D_MODEL

GD_HEADS = 8
GD_DK = 128
GD_DV = 256
GD_QK = GD_HEADS * GD_DK
GD_V = GD_HEADS * GD_DV
GD_CONV_W = 5
GD_CONV_CH = 2 * GD_QK + GD_V
GD_IN = GD_CONV_CH + GD_V + 4 * GD_HEADS

N_EXPERTS = 16
EC_CAPACITY_FACTOR = 2
D_EXPERT = 2816

kernel_name = "hybrid_hgrn2_gdn_ec_moe_diffusion"

F32 = jnp.float32


def rms_norm(x, g):
    xf = x.astype(F32)
    y = xf * lax.rsqrt(jnp.mean(xf * xf, axis=-1, keepdims=True) + EPS)
    return (y * g.astype(F32)).astype(x.dtype)


def adaln(cond, w, b):
    return (jax.nn.silu(cond) @ w + b).reshape(cond.shape[0], 6, D_MODEL)


def modulate(x, g, shift, scale):
    return rms_norm(x, g) * (1 + scale[:, None, :]) + shift[:, None, :]


def heads(t, n_heads):
    t = t.reshape(*t.shape[:-1], n_heads, -1)
    return jnp.swapaxes(t, -2, -3)


def l2norm(t):
    return t * lax.rsqrt(jnp.sum(t * t, axis=-1, keepdims=True) + EPS)


def raster_to_column(t):
    b, l, d = t.shape
    rows = l // GRID_W
    return t.reshape(b, rows, GRID_W, d).transpose(0, 2, 1, 3).reshape(b, l, d)


def column_to_raster(t):
    b, l, d = t.shape
    rows = l // GRID_W
    return t.reshape(b, GRID_W, rows, d).transpose(0, 2, 1, 3).reshape(b, l, d)


def to_chunks(t):
    b, h, l = t.shape[:3]
    t = t.reshape(b, h, l // CHUNK, CHUNK, *t.shape[3:])
    return jnp.moveaxis(t, 2, 0)


def from_chunks(t):
    n, b, h, cl, d = t.shape
    return jnp.moveaxis(t, 0, 2).reshape(b, h, n * cl, d)


def gla_chunk_scan(q, k, v, logf, s0):
    causal = jnp.tril(jnp.ones((CHUNK, CHUNK), bool))

    def step(s, inp):
        qi, ki, vi, gi = inp
        bcum = jnp.cumsum(gi, axis=-2)
        diff = bcum[..., :, None, :] - bcum[..., None, :, :]
        decay = jnp.exp(jnp.where(causal[..., None], diff, -jnp.inf))
        att = jnp.einsum('bhtd,bhsd,bhtsd->bhts', qi, ki, decay)
        o = jnp.einsum('bhts,bhsv->bhtv', att, vi) + jnp.einsum('bhtd,bhdv->bhtv', qi * jnp.exp(bcum), s)
        b_last = bcum[..., -1:, :]
        s_new = jnp.exp(b_last)[..., 0, :, None] * s + jnp.einsum('bhsd,bhsv->bhdv', ki * jnp.exp(b_last - bcum), vi)
        return s_new, o

    s_fin, o = lax.scan(step, s0, (to_chunks(q), to_chunks(k), to_chunks(v), to_chunks(logf)))
    return from_chunks(o), s_fin


def gated_delta_chunk_scan(q, k, v, beta, loga, s0):
    b, h, l, dk = q.shape
    dv = v.shape[-1]
    n = l // CHUNK
    cs = lambda t: t.reshape(b, h, n, CHUNK, *t.shape[3:])
    qc, kc, vc, bc, ac = cs(q), cs(k), cs(v), cs(beta), cs(loga)
    g = jnp.cumsum(ac, axis=-1)
    diff = g[..., :, None] - g[..., None, :]
    incl = jnp.tril(jnp.ones((CHUNK, CHUNK), bool))
    strict = jnp.tril(jnp.ones((CHUNK, CHUNK), bool), -1)
    gam = jnp.exp(jnp.where(incl, diff, -jnp.inf))
    kk = jnp.einsum('bhntd,bhnsd->bhnts', kc, kc)
    a_mat = jnp.where(strict, bc[..., :, None] * kk * gam, 0.0)
    eye = jnp.eye(CHUNK, dtype=a_mat.dtype)
    rhs = jnp.concatenate([bc[..., None] * vc, (bc * jnp.exp(g))[..., None] * kc], axis=-1)
    sol = lax.linalg.triangular_solve(eye + a_mat, rhs, left_side=True, lower=True, unit_diagonal=True)
    u_c, w_c = sol[..., :dv], sol[..., dv:]
    qk = jnp.einsum('bhntd,bhnsd->bhnts', qc, kc) * gam
    qg = qc * jnp.exp(g)[..., None]
    kd = kc * jnp.exp(g[..., -1:] - g)[..., None]
    g_last = jnp.exp(g[..., -1])

    def step(s, inp):
        ui, wi, qki, qgi, kdi, gli = inp
        w_eff = ui - jnp.einsum('bhck,bhkv->bhcv', wi, s)
        o = jnp.einsum('bhts,bhsv->bhtv', qki, w_eff) + jnp.einsum('bhtk,bhkv->bhtv', qgi, s)
        s_new = gli[..., None, None] * s + jnp.einsum('bhsk,bhsv->bhkv', kdi, w_eff)
        return s_new, o

    mv = lambda t: jnp.moveaxis(t, 2, 0)
    s_fin, o = lax.scan(step, s0, (mv(u_c), mv(w_c), mv(qk), mv(qg), mv(kd), mv(g_last)))
    return from_chunks(o), s_fin


def two_segment_scan(scan_fn, ctx_args, lat_args, reverse):
    flip = (lambda t: jnp.flip(t, axis=2)) if reverse else (lambda t: t)
    q, v = ctx_args[0], ctx_args[2]
    s0 = jnp.zeros((q.shape[0], q.shape[1], q.shape[-1], v.shape[-1]), F32)
    o_c, s_c = scan_fn(*[flip(t) for t in ctx_args], s0)
    o_l, _ = scan_fn(*[flip(t) for t in lat_args], s_c)
    return flip(o_c), flip(o_l)


def mixer_out(o, gate, gain, w_out, dtype):
    b, h, l, dv = o.shape
    o = jnp.swapaxes(o, 1, 2)
    o = o * lax.rsqrt(jnp.mean(o * o, axis=-1, keepdims=True) + EPS) * gain.astype(F32).reshape(h, dv)
    o = o.reshape(b, l, h * dv) * jax.nn.silu(gate)
    return o.astype(dtype) @ w_out


def hgrn2_mixer(h_ctx, h_lat, w_in, lb, o_gain, w_out, ctx_out):
    lbf = lb.astype(F32)

    def project(h):
        bsz, l, _ = h.shape
        z = (h @ w_in).astype(F32)
        q, f_raw, i_in, gate = jnp.split(z, [HG_FDIM, 3 * HG_FDIM, 3 * HG_FDIM + D_MODEL], axis=-1)
        f = lbf + (1 - lbf) * jax.nn.sigmoid(f_raw.reshape(bsz, l, 2, HG_FDIM))
        f = jnp.moveaxis(f, 2, 0)
        return (heads(jax.nn.silu(q), HG_HEADS), heads(1 - f, HG_HEADS),
                heads(i_in, HG_HEADS), heads(jnp.log(f), HG_HEADS), gate)

    qc, kc, vc, gc, gate_c = project(h_ctx)
    ql, kl, vl, gl, gate_l = project(h_lat)
    oc_f, ol_f = two_segment_scan(gla_chunk_scan, (qc, kc[0], vc, gc[0]), (ql, kl[0], vl, gl[0]), False)
    oc_b, ol_b = two_segment_scan(gla_chunk_scan, (qc, kc[1], vc, gc[1]), (ql, kl[1], vl, gl[1]), True)
    y_lat = mixer_out(ol_f + ol_b, gate_l, o_gain, w_out, h_lat.dtype)
    y_ctx = mixer_out(oc_f + oc_b, gate_c, o_gain, w_out, h_ctx.dtype) if ctx_out else None
    return y_ctx, y_lat


def short_conv(u, w):
    return lax.conv_general_dilated(u, w[:, None, :].astype(u.dtype), (1,),
                                    [(GD_CONV_W // 2, GD_CONV_W // 2)],
                                    dimension_numbers=('NWC', 'WIO', 'NWC'),
                                    feature_group_count=u.shape[-1])


def gdn_mixer(h_ctx, h_lat, w_in, conv_w, a_log, dt_bias, o_gain, w_out, ctx_out):
    def project(h):
        bsz, l, _ = h.shape
        z = h @ w_in
        qkv = jax.nn.silu(short_conv(z[..., :GD_CONV_CH], conv_w)).astype(F32)
        q, k, v = jnp.split(qkv, [GD_QK, 2 * GD_QK], axis=-1)
        q = l2norm(heads(q, GD_HEADS)) * (GD_DK ** -0.5)
        k = l2norm(heads(k, GD_HEADS))
        v = heads(v, GD_HEADS)
        zr = z[..., GD_CONV_CH:].astype(F32)
        gate = zr[..., :GD_V]
        a = zr[..., GD_V:GD_V + 2 * GD_HEADS].reshape(bsz, l, 2, GD_HEADS)
        bb = zr[..., GD_V + 2 * GD_HEADS:].reshape(bsz, l, 2, GD_HEADS)
        beta = jax.nn.sigmoid(bb).transpose(2, 0, 3, 1)
        loga = (-jnp.exp(a_log.astype(F32))[:, None, :, None]
                * jax.nn.softplus(a + dt_bias.astype(F32)).transpose(2, 0, 3, 1))
        return q, k, v, beta, loga, gate

    qc, kc, vc, bc, ac, gate_c = project(h_ctx)
    ql, kl, vl, bl, al, gate_l = project(h_lat)
    oc_f, ol_f = two_segment_scan(gated_delta_chunk_scan, (qc, kc, vc, bc[0], ac[0]), (ql, kl, vl, bl[0], al[0]), False)
    oc_b, ol_b = two_segment_scan(gated_delta_chunk_scan, (qc, kc, vc, bc[1], ac[1]), (ql, kl, vl, bl[1], al[1]), True)
    y_lat = mixer_out(ol_f + ol_b, gate_l, o_gain, w_out, h_lat.dtype)
    y_ctx = mixer_out(oc_f + oc_b, gate_c, o_gain, w_out, h_ctx.dtype) if ctx_out else None
    return y_ctx, y_lat


def ec_moe(h, w_router, w_gate, w_up, w_down):
    bsz, n, d = h.shape
    cap = EC_CAPACITY_FACTOR * n // N_EXPERTS
    aff = jax.nn.softmax(jnp.einsum('bnd,de->bne', h, w_router).astype(F32), axis=-1)
    top_aff, top_idx = lax.top_k(jnp.swapaxes(aff, 1, 2), cap)
    xs = jax.vmap(lambda hb, ib: hb[ib])(h, top_idx)
    hid = jax.nn.silu(jnp.einsum('becd,edf->becf', xs, w_gate)) * jnp.einsum('becd,edf->becf', xs, w_up)
    y = jnp.einsum('becf,efd->becd', hid, w_down) * top_aff[..., None].astype(h.dtype)
    return jax.vmap(lambda yb, ib: jnp.zeros((n, d), yb.dtype).at[ib].add(yb))(y, top_idx)


def setup_inputs(seed: int = 0) -> dict:
    key = jax.random.key(seed)
    ks = jax.random.split(key, 24)
    D = D_MODEL
    nrm = lambda k, shape, s: jax.random.normal(k, shape, F32) * s
    dt = jnp.exp(jax.random.uniform(ks[14], (N_LAYERS_B, 2, GD_HEADS), F32,
                                    minval=math.log(1e-3), maxval=math.log(1e-1)))
    return {
        'x': nrm(ks[0], (BATCH, SEQ, D), 1.0),
        'c': nrm(ks[1], (BATCH, D), 1.0),
        'ctx': nrm(ks[2], (BATCH, CTX_LEN, D), 1.0),
        'c_ctx': nrm(ks[3], (D,), 1.0),
        'w_mod': nrm(ks[4], (DEPTH, D, 6 * D), 0.5 * D ** -0.5),
        'b_mod': nrm(ks[5], (DEPTH, 6 * D), 0.02),
        'norm_g': 1.0 + nrm(ks[6], (DEPTH, 2, D), 0.02),
        'hg_w_in': nrm(ks[7], (N_LAYERS_A, D, HG_IN), D ** -0.5),
        'hg_lb': nrm(ks[8], (DEPTH + 1, 2, HG_FDIM), 0.1),
        'hg_onorm': 1.0 + nrm(ks[9], (N_LAYERS_A, D), 0.02),
        'hg_w_out': nrm(ks[10], (N_LAYERS_A, D, D), D ** -0.5),
        'gd_w_in': nrm(ks[11], (N_LAYERS_B, D, GD_IN), D ** -0.5),
        'gd_conv': nrm(ks[12], (N_LAYERS_B, GD_CONV_W, GD_CONV_CH), GD_CONV_W ** -0.5),
        'gd_a_log': jnp.log(jax.random.uniform(ks[13], (N_LAYERS_B, 2, GD_HEADS), F32, minval=1.0, maxval=16.0)),
        'gd_dt_bias': dt + jnp.log(-jnp.expm1(-dt)),
        'gd_onorm': 1.0 + nrm(ks[15], (N_LAYERS_B, GD_V), 0.02),
        'gd_w_out': nrm(ks[16], (N_LAYERS_B, GD_V, D), GD_V ** -0.5),
        'moe_router': nrm(ks[17], (DEPTH, D, N_EXPERTS), D ** -0.5),
        'moe_w_gate': nrm(ks[18], (DEPTH, N_EXPERTS, D, D_EXPERT), D ** -0.5),
        'moe_w_up': nrm(ks[19], (DEPTH, N_EXPERTS, D, D_EXPERT), D ** -0.5),
        'moe_w_down': nrm(ks[20], (DEPTH, N_EXPERTS, D_EXPERT, D), D_EXPERT ** -0.5),
        'final_g': 1.0 + nrm(ks[21], (D,), 0.02),
    }


def reference(x, c, ctx, c_ctx, w_mod, b_mod, norm_g, hg_w_in, hg_lb, hg_onorm, hg_w_out,
              gd_w_in, gd_conv, gd_a_log, gd_dt_bias, gd_onorm, gd_w_out,
              moe_router, moe_w_gate, moe_w_up, moe_w_down, final_g):
    lb_all = jnp.cumsum(jax.nn.softmax(hg_lb.astype(F32), axis=0), axis=0)
    xc = ctx
    for i in range(DEPTH):
        last = i == DEPTH - 1
        m_lat = adaln(c, w_mod[i], b_mod[i])
        m_ctx = adaln(c_ctx[None, :], w_mod[i], b_mod[i])
        hl = modulate(x, norm_g[i, 0], m_lat[:, 0], m_lat[:, 1])
        hc = modulate(xc, norm_g[i, 0], m_ctx[:, 0], m_ctx[:, 1])
        j = i // N_MIXERS
        if i % N_MIXERS == 0:
            oc, ol = hgrn2_mixer(hc, hl, hg_w_in[j], lb_all[i], hg_onorm[j], hg_w_out[j], not last)
        else:
            oc, ol = gdn_mixer(hc, raster_to_column(hl), gd_w_in[j], gd_conv[j], gd_a_log[j],
                               gd_dt_bias[j], gd_onorm[j], gd_w_out[j], not last)
            ol = column_to_raster(ol)
        x = x + m_lat[:, 2][:, None, :] * ol
        hl = modulate(x, norm_g[i, 1], m_lat[:, 3], m_lat[:, 4])
        x = x + m_lat[:, 5][:, None, :] * ec_moe(hl, moe_router[i], moe_w_gate[i], moe_w_up[i], moe_w_down[i])
        if not last:
            xc = xc + m_ctx[:, 2][:, None, :] * oc
            hc = modulate(xc, norm_g[i, 1], m_ctx[:, 3], m_ctx[:, 4])
            xc = xc + m_ctx[:, 5][:, None, :] * ec_moe(hc, moe_router[i], moe_w_gate[i], moe_w_up[i], moe_w_down[i])
    return rms_norm(x, final_g)
```

```python
import functools
import math

import jax
import jax.numpy as jnp
from jax import lax
from jax.experimental import pallas as pl
from jax.experimental.pallas import tpu as pltpu

F32 = jnp.float32
BF16 = jnp.bfloat16
HIGHEST = lax.Precision.HIGHEST

EPS = 1e-6
LANES = 128
SUBLANES = 8
V7X_VMEM_BYTES = 64 * 1024 * 1024
VMEM_LIMIT = V7X_VMEM_BYTES * 7 // 8

HEAD_DK = 128
GD_DV = 256
GD_CONV_W = 5
GRID_W = 64
EC_CAPACITY_FACTOR = 2
CHUNK = 64
SUB = 16
EXP_CLAMP = 60.0
MAX_BLOCK_ROWS = 256
FFN_TILE = 256
AFF_BITS = 31


def _cp(*sem):
    return pltpu.CompilerParams(dimension_semantics=sem, vmem_limit_bytes=VMEM_LIMIT)


def _dot(a, b):
    return jnp.dot(a, b, preferred_element_type=F32)


def _dot_nt(a, b):
    return lax.dot_general(a, b, (((1,), (1,)), ((), ())), preferred_element_type=F32)


def _dot_tn(a, b):
    return lax.dot_general(a, b, (((0,), (0,)), ((), ())), preferred_element_type=F32)


def _dot_hi(a, b):
    return jnp.dot(a, b, preferred_element_type=F32, precision=HIGHEST)


def _dot_nt_hi(a, b):
    return lax.dot_general(a, b, (((1,), (1,)), ((), ())), preferred_element_type=F32,
                           precision=HIGHEST)


def _silu(x):
    return x * jax.nn.sigmoid(x)


def _norm_mod(x, g, shift, scale):
    y = x * lax.rsqrt(jnp.mean(x * x, axis=-1, keepdims=True) + EPS)
    return y * g * (1.0 + scale) + shift


def _iota(shape, dim):
    return lax.broadcasted_iota(jnp.int32, shape, dim)


def _adaln_kernel(c_ref, w_ref, b_ref, o_ref):
    o_ref[0] = _dot_hi(_silu(c_ref[...]), w_ref[0]) + b_ref[0]


def _adaln(cond, w_mod, b_mod):
    depth, d, n = w_mod.shape
    r = cond.shape[0]
    tn = n // 12
    return pl.pallas_call(
        _adaln_kernel,
        grid=(depth, n // tn),
        in_specs=[pl.BlockSpec((r, d), lambda i, j: (0, 0)),
                  pl.BlockSpec((1, d, tn), lambda i, j: (i, 0, j)),
                  pl.BlockSpec((1, 1, tn), lambda i, j: (i, 0, j))],
        out_specs=pl.BlockSpec((1, r, tn), lambda i, j: (i, 0, j)),
        out_shape=jax.ShapeDtypeStruct((depth, r, n), F32),
        compiler_params=_cp("parallel", "parallel"),
        name="adaln",
    )(cond, w_mod, b_mod.reshape(depth, 1, n))


def _hg_inproj_kernel(x_ref, mod_ref, g_ref, lb_ref, wq_ref, wf_ref, wi_ref, wg_ref,
                      q_ref, lf_ref, v_ref, gate_ref, *, layer, n_heads):
    m = mod_ref[0, 0]
    hb = _norm_mod(x_ref[0], g_ref[...], m[0:1], m[1:2]).astype(BF16)
    fdim = wq_ref.shape[1]
    dk = fdim // n_heads
    dv = wi_ref.shape[1] // n_heads

    q = _silu(_dot(hb, wq_ref[...]))
    for h in range(n_heads):
        q_ref[0, h] = q[:, h * dk:(h + 1) * dk].astype(BF16)

    lbp = lb_ref[...]
    e = jnp.exp(lbp - jnp.max(lbp, axis=0))
    lb = jnp.sum(e[:layer + 1], axis=0) / jnp.sum(e, axis=0)

    zf = _dot(hb, wf_ref[...])
    for d in range(2):
        lbd = lb[d:d + 1]
        f = lbd + (1.0 - lbd) * jax.nn.sigmoid(zf[:, d * fdim:(d + 1) * fdim])
        lf = jnp.log(f)
        for h in range(n_heads):
            lf_ref[d, 0, h] = lf[:, h * dk:(h + 1) * dk]

    v = _dot(hb, wi_ref[...])
    for h in range(n_heads):
        v_ref[0, h] = v[:, h * dv:(h + 1) * dv].astype(BF16)

    gate_ref[0] = _silu(_dot(hb, wg_ref[...])).astype(BF16)


def _hg_inproj(x_all, modsel, g, hg_lb, w_in, *, layer, nc, tm):
    b, t, d = x_all.shape
    fdim = hg_lb.shape[-1]
    n_heads = fdim // HEAD_DK
    dv = d // n_heads
    wq = w_in[:, :fdim].astype(BF16)
    wf = w_in[:, fdim:3 * fdim].astype(BF16)
    wi = w_in[:, 3 * fdim:3 * fdim + d].astype(BF16)
    wg = w_in[:, 3 * fdim + d:].astype(BF16)
    full = lambda a: pl.BlockSpec(a.shape, lambda i, j: (0,) * a.ndim)
    g2 = g.reshape(1, d)
    return pl.pallas_call(
        functools.partial(_hg_inproj_kernel, layer=layer, n_heads=n_heads),
        grid=(b, t // tm),
        in_specs=[pl.BlockSpec((1, tm, d), lambda i, j: (i, j, 0)),
                  pl.BlockSpec((1, 1, 6, d), lambda i, j: (i, jnp.where(j >= nc, 1, 0), 0, 0)),
                  full(g2), full(hg_lb), full(wq), full(wf), full(wi), full(wg)],
        out_specs=[pl.BlockSpec((1, n_heads, tm, HEAD_DK), lambda i, j: (i, 0, j, 0)),
                   pl.BlockSpec((2, 1, n_heads, tm, HEAD_DK), lambda i, j: (0, i, 0, j, 0)),
                   pl.BlockSpec((1, n_heads, tm, dv), lambda i, j: (i, 0, j, 0)),
                   pl.BlockSpec((1, tm, d), lambda i, j: (i, j, 0))],
        out_shape=[jax.ShapeDtypeStruct((b, n_heads, t, HEAD_DK), BF16),
                   jax.ShapeDtypeStruct((2, b, n_heads, t, HEAD_DK), F32),
                   jax.ShapeDtypeStruct((b, n_heads, t, dv), BF16),
                   jax.ShapeDtypeStruct((b, t, d), BF16)],
        compiler_params=_cp("parallel", "parallel"),
        name="hg_inproj",
    )(x_all, modsel, g2, hg_lb, wq, wf, wi, wg)


def _scan_block_index(step, nc, nl, rev):
    if not rev:
        return step
    return jnp.where(step < nc, nc - 1 - step, nc + nl - 1 - (step - nc))


def _chunk_masks(rev):
    r = _iota((CHUNK, CHUNK), 0)
    c = _iota((CHUNK, CHUNK), 1)
    incl = (c >= r) if rev else (c <= r)
    strict = (c > r) if rev else (c < r)
    return incl, strict


def _gla_scan_kernel(q_ref, lf_ref, v_ref, o_ref, st_ref, oin_ref, qc_ref, kh_ref, dec_ref,
                     *, rev, n_heads, nck):
    @pl.when(pl.program_id(1) == 0)
    def _():
        st_ref[...] = jnp.zeros(st_ref.shape, F32)

    incl, _ = _chunk_masks(rev)
    mincl = jnp.where(incl, 1.0, 0.0)

    def pre(hh, carry):
        for ci in range(nck):
            rows = pl.ds(ci * CHUNK, CHUNK)
            q = q_ref[0, hh, rows, :].astype(F32)
            g = lf_ref[0, 0, hh, rows, :]
            b = _dot_hi(mincl, g)
            be = b - g
            bl = b[0:1] if rev else b[CHUNK - 1:CHUNK]
            k = 1.0 - jnp.exp(g)
            qc_ref[hh, rows, :] = (q * jnp.exp(b)).astype(BF16)
            kh_ref[hh, rows, :] = (k * jnp.exp(bl - b)).astype(BF16)
            dec_ref[hh, ci] = jnp.exp(bl)
            atts = []
            for i in range(CHUNK // SUB):
                r0 = i * SUB
                bref = be[r0 + SUB - 1:r0 + SUB] if rev else be[r0:r0 + 1]
                qi = (q[r0:r0 + SUB] * jnp.exp(b[r0:r0 + SUB] - bref)).astype(BF16)
                kt = (k * jnp.exp(jnp.minimum(bref - b, EXP_CLAMP))).astype(BF16)
                atts.append(_dot_nt(qi, kt))
            att = jnp.where(incl, jnp.concatenate(atts, axis=0), 0.0).astype(BF16)
            oin_ref[hh, rows, :] = _dot(att, v_ref[0, hh, rows, :])
        return carry

    lax.fori_loop(0, n_heads, pre, 0)

    for step in range(nck):
        ci = nck - 1 - step if rev else step
        rows = pl.ds(ci * CHUNK, CHUNK)
        for h in range(n_heads):
            st = st_ref[h]
            o = oin_ref[h, rows, :] + _dot_nt(qc_ref[h, rows, :], st.astype(BF16))
            o_ref[0, h, rows, :] = o.astype(o_ref.dtype)
            st_ref[h] = st * dec_ref[h, ci] + _dot_tn(v_ref[0, h, rows, :], kh_ref[h, rows, :])


def _gla_scan(q, lf, v, *, rev, nc, tm, out_dtype):
    b, n_heads, t, dk = q.shape
    dv = v.shape[-1]
    nblk = t // tm
    nl = nblk - nc
    nck = tm // CHUNK
    d = 1 if rev else 0
    blk = lambda i, s: _scan_block_index(s, nc, nl, rev)
    return pl.pallas_call(
        functools.partial(_gla_scan_kernel, rev=rev, n_heads=n_heads, nck=nck),
        grid=(b, nblk),
        in_specs=[pl.BlockSpec((1, n_heads, tm, dk), lambda i, s: (i, 0, blk(i, s), 0)),
                  pl.BlockSpec((1, 1, n_heads, tm, dk), lambda i, s: (d, i, 0, blk(i, s), 0)),
                  pl.BlockSpec((1, n_heads, tm, dv), lambda i, s: (i, 0, blk(i, s), 0))],
        out_specs=pl.BlockSpec((1, n_heads, tm, dv), lambda i, s: (i, 0, blk(i, s), 0)),
        out_shape=jax.ShapeDtypeStruct((b, n_heads, t, dv), out_dtype),
        scratch_shapes=[pltpu.VMEM((n_heads, dv, dk), F32),
                        pltpu.VMEM((n_heads, tm, dv), F32),
                        pltpu.VMEM((n_heads, tm, dk), BF16),
                        pltpu.VMEM((n_heads, tm, dk), BF16),
                        pltpu.VMEM((n_heads, nck, 1, dk), F32)],
        compiler_params=_cp("parallel", "arbitrary"),
        name="gla_scan_bwd" if rev else "gla_scan_fwd",
    )(q, lf, v)


def _outproj_kernel(of_ref, ob_ref, gate_ref, gain_ref, w_ref, x_ref, mod_ref, o_ref, y_ref,
                    *, n_heads):
    dv = of_ref.shape[-1]
    for h in range(n_heads):
        o = of_ref[0, h].astype(F32) + ob_ref[0, h].astype(F32)
        cs = slice(h * dv, (h + 1) * dv)
        o = o * lax.rsqrt(jnp.mean(o * o, axis=-1, keepdims=True) + EPS) * gain_ref[:, cs]
        y_ref[:, cs] = (o * gate_ref[0, :, cs].astype(F32)).astype(BF16)
    m = mod_ref[0, 0]
    o_ref[0] = x_ref[0] + m[2:3] * _dot(y_ref[...], w_ref[...])


def _outproj(o_f, o_b, gate, gain, w_out, x_all, modsel, *, nc, tm, skip_ctx):
    b, n_heads, t, dv = o_f.shape
    d = x_all.shape[-1]
    hv = n_heads * dv
    off = nc if skip_ctx else 0
    nblk = t // tm - off
    seg = (lambda j: 1) if skip_ctx else (lambda j: jnp.where(j >= nc, 1, 0))
    w = w_out.astype(BF16)
    gain2 = gain.reshape(1, hv)
    return pl.pallas_call(
        functools.partial(_outproj_kernel, n_heads=n_heads),
        grid=(b, nblk),
        in_specs=[pl.BlockSpec((1, n_heads, tm, dv), lambda i, j: (i, 0, j + off, 0)),
                  pl.BlockSpec((1, n_heads, tm, dv), lambda i, j: (i, 0, j + off, 0)),
                  pl.BlockSpec((1, tm, hv), lambda i, j: (i, j + off, 0)),
                  pl.BlockSpec((1, hv), lambda i, j: (0, 0)),
                  pl.BlockSpec((hv, d), lambda i, j: (0, 0)),
                  pl.BlockSpec((1, tm, d), lambda i, j: (i, j + off, 0)),
                  pl.BlockSpec((1, 1, 6, d), lambda i, j: (i, seg(j), 0, 0))],
        out_specs=pl.BlockSpec((1, tm, d), lambda i, j: (i, j, 0)),
        out_shape=jax.ShapeDtypeStruct((b, nblk * tm, d), F32),
        scratch_shapes=[pltpu.VMEM((tm, hv), BF16)],
        compiler_params=_cp("parallel", "parallel"),
        name="outproj",
    )(o_f, o_b, gate, gain2, w, x_all, modsel)


def _gd_inproj_kernel(xp_ref, x_ref, xn_ref, mod_ref, g_ref, wqkv_ref, wg_ref, wab_ref, cw_ref,
                      alog_ref, dtb_ref, q_ref, k_ref, v_ref, gate_ref, ab_ref, zs_ref,
                      *, nc, nblk, n_heads, tm, cw):
    t = pl.program_id(1)
    first = (t == 0) | (t == nc)
    last = (t == nc - 1) | (t == nblk - 1)
    m = mod_ref[0, 0]
    halo = SUBLANES
    xe = jnp.concatenate([xp_ref[0], x_ref[0], xn_ref[0]], axis=0)
    he = _norm_mod(xe, g_ref[...], m[0:1], m[1:2]).astype(BF16)
    hc = he[halo:halo + tm]
    row = _iota((tm + 2 * halo, 1), 0)
    valid = ((row >= halo) | jnp.logical_not(first)) & ((row < tm + halo) | jnp.logical_not(last))

    gate_ref[0] = _silu(_dot(hc, wg_ref[...])).astype(BF16)

    zab = _dot(hc, wab_ref[...])
    lane = _iota(zab.shape, 1)
    loga = -jnp.exp(alog_ref[...]) * jax.nn.softplus(zab + dtb_ref[...])
    ab_ref[0] = jnp.where(lane < 2 * n_heads, loga, jax.nn.sigmoid(zab))

    qk = n_heads * HEAD_DK
    nchan = wqkv_ref.shape[1]
    pad = GD_CONV_W // 2
    for cc in range(nchan // cw):
        c0 = cc * cw
        z = _dot(he, wqkv_ref[:, c0:c0 + cw])
        zs_ref[...] = jnp.where(valid, z, 0.0)
        acc = cw_ref[0:1, c0:c0 + cw] * zs_ref[halo - pad:halo - pad + tm, :]
        for j in range(1, GD_CONV_W):
            acc = acc + cw_ref[j:j + 1, c0:c0 + cw] * zs_ref[halo - pad + j:halo - pad + j + tm, :]
        u = _silu(acc)
        if c0 < 2 * qk:
            dst, base, scale = (q_ref, c0, HEAD_DK ** -0.5) if c0 < qk else (k_ref, c0 - qk, 1.0)
            for j in range(cw // HEAD_DK):
                tt = u[:, j * HEAD_DK:(j + 1) * HEAD_DK]
                n = tt * lax.rsqrt(jnp.sum(tt * tt, axis=-1, keepdims=True) + EPS)
                dst[0, base // HEAD_DK + j] = (n * scale).astype(BF16)
        else:
            base = c0 - 2 * qk
            for j in range(cw // GD_DV):
                v_ref[0, base // GD_DV + j] = u[:, j * GD_DV:(j + 1) * GD_DV].astype(BF16)


def _gd_inproj(x_all, modsel, g, w_in, conv_w, a_log, dt_bias, *, nc, tm):
    b, t, d = x_all.shape
    n_heads = a_log.shape[-1]
    qk = n_heads * HEAD_DK
    vd = n_heads * GD_DV
    nchan = 2 * qk + vd
    nblk = t // tm
    cw = min(512, qk)
    wqkv = w_in[:, :nchan].astype(BF16)
    wg = w_in[:, nchan:nchan + vd].astype(BF16)
    wab = jnp.pad(w_in[:, nchan + vd:], ((0, 0), (0, LANES - 4 * n_heads))).astype(BF16)
    alog = jnp.pad(a_log.reshape(1, 2 * n_heads), ((0, 0), (0, LANES - 2 * n_heads)))
    dtb = jnp.pad(dt_bias.reshape(1, 2 * n_heads), ((0, 0), (0, LANES - 2 * n_heads)))
    g2 = g.reshape(1, d)
    full = lambda a: pl.BlockSpec(a.shape, lambda i, j: (0,) * a.ndim)
    spb = tm // SUBLANES
    nslab = t // SUBLANES
    return pl.pallas_call(
        functools.partial(_gd_inproj_kernel, nc=nc, nblk=nblk, n_heads=n_heads, tm=tm, cw=cw),
        grid=(b, nblk),
        in_specs=[pl.BlockSpec((1, SUBLANES, d), lambda i, j: (i, jnp.maximum(j * spb - 1, 0), 0)),
                  pl.BlockSpec((1, tm, d), lambda i, j: (i, j, 0)),
                  pl.BlockSpec((1, SUBLANES, d),
                               lambda i, j: (i, jnp.minimum((j + 1) * spb, nslab - 1), 0)),
                  pl.BlockSpec((1, 1, 6, d), lambda i, j: (i, jnp.where(j >= nc, 1, 0), 0, 0)),
                  full(g2), full(wqkv), full(wg), full(wab), full(conv_w), full(alog), full(dtb)],
        out_specs=[pl.BlockSpec((1, n_heads, tm, HEAD_DK), lambda i, j: (i, 0, j, 0)),
                   pl.BlockSpec((1, n_heads, tm, HEAD_DK), lambda i, j: (i, 0, j, 0)),
                   pl.BlockSpec((1, n_heads, tm, GD_DV), lambda i, j: (i, 0, j, 0)),
                   pl.BlockSpec((1, tm, vd), lambda i, j: (i, j, 0)),
                   pl.BlockSpec((1, tm, LANES), lambda i, j: (i, j, 0))],
        out_shape=[jax.ShapeDtypeStruct((b, n_heads, t, HEAD_DK), BF16),
                   jax.ShapeDtypeStruct((b, n_heads, t, HEAD_DK), BF16),
                   jax.ShapeDtypeStruct((b, n_heads, t, GD_DV), BF16),
                   jax.ShapeDtypeStruct((b, t, vd), BF16),
                   jax.ShapeDtypeStruct((b, t, LANES), F32)],
        scratch_shapes=[pltpu.VMEM((tm + 2 * SUBLANES, cw), F32)],
        compiler_params=_cp("parallel", "parallel"),
        name="gd_inproj",
    )(x_all, x_all, x_all, modsel, g2, wqkv, wg, wab, conv_w, alog, dtb)


def _neumann_inverse(a, eye):
    t = eye - a
    ab = a.astype(BF16)
    p = _dot(ab, ab)
    n = 2
    while n < CHUNK:
        pb = p.astype(BF16)
        out = _dot(pb, jnp.concatenate([pb, t.astype(BF16)], axis=1))
        t = t + out[:, CHUNK:]
        p = out[:, :CHUNK]
        n *= 2
    return t


def _gdn_scan_kernel(q_ref, k_ref, v_ref, ab_ref, o_ref, s_ref, colg_ref, colb_ref, rowg_ref,
                     u_ref, w_ref, qk_ref, qg_ref, kd_ref, gl_ref, *, rev, d, n_heads, nck, tm):
    @pl.when(pl.program_id(1) == 0)
    def _():
        s_ref[...] = jnp.zeros(s_ref.shape, F32)

    incl, strict = _chunk_masks(rev)
    eye = jnp.where(_iota((CHUNK, CHUNK), 0) == _iota((CHUNK, CHUNK), 1), 1.0, 0.0)

    ab = ab_ref[0]
    r = _iota((tm, tm), 0)
    c = _iota((tm, tm), 1)
    same = (r // CHUNK) == (c // CHUNK)
    lm = jnp.where(same & ((c >= r) if rev else (c <= r)), 1.0, 0.0)
    gcol = _dot_hi(lm, ab)
    eye_l = jnp.where(_iota((LANES, LANES), 0) == _iota((LANES, LANES), 1), 1.0, 0.0)
    grow = _dot_nt_hi(eye_l, gcol)
    for h in range(n_heads):
        ca = d * n_heads + h
        cb = 2 * n_heads + ca
        colg_ref[h] = gcol[:, ca:ca + 1]
        colb_ref[h] = ab[:, cb:cb + 1]
        for ci in range(nck):
            rowg_ref[h, ci] = grow[ca:ca + 1, ci * CHUNK:(ci + 1) * CHUNK]

    dv = v_ref.shape[-1]

    def pre(hh, carry):
        for ci in range(nck):
            rows = pl.ds(ci * CHUNK, CHUNK)
            q = q_ref[0, hh, rows, :]
            k = k_ref[0, hh, rows, :]
            v = v_ref[0, hh, rows, :]
            gc = colg_ref[hh, rows, :]
            bc = colb_ref[hh, rows, :]
            gr = rowg_ref[hh, ci]
            gam = jnp.where(incl, jnp.exp(jnp.minimum(gc - gr, 0.0)), 0.0)
            kk = _dot_nt(k, k)
            a = jnp.where(strict, bc * kk * gam, 0.0)
            tinv = _neumann_inverse(a, eye)
            kf = k.astype(F32)
            eg = jnp.exp(gc)
            rhs = jnp.concatenate([(bc * v.astype(F32)).astype(BF16),
                                   (bc * eg * kf).astype(BF16)], axis=1)
            uw = _dot(tinv.astype(BF16), rhs)
            u_ref[hh, rows, :] = uw[:, :dv]
            w_ref[hh, rows, :] = uw[:, dv:].astype(BF16)
            qk_ref[hh, rows, :] = (_dot_nt(q, k) * gam).astype(BF16)
            qg_ref[hh, rows, :] = (q.astype(F32) * eg).astype(BF16)
            gl = gc[0:1] if rev else gc[CHUNK - 1:CHUNK]
            kd_ref[hh, rows, :] = (kf * jnp.exp(gl - gc)).astype(BF16)
            gl_ref[hh, ci] = jnp.broadcast_to(jnp.exp(gl), (1, LANES))
        return carry

    lax.fori_loop(0, n_heads, pre, 0)

    for step in range(nck):
        ci = nck - 1 - step if rev else step
        rows = pl.ds(ci * CHUNK, CHUNK)
        for h in range(n_heads):
            s = s_ref[h]
            sb = s.astype(BF16)
            weff = (u_ref[h, rows, :] - _dot(w_ref[h, rows, :], sb)).astype(BF16)
            o = _dot(qk_ref[h, rows, :], weff) + _dot(qg_ref[h, rows, :], sb)
            o_ref[0, h, rows, :] = o.astype(o_ref.dtype)
            s_ref[h] = s * gl_ref[h, ci][:, 0:1] + _dot_tn(kd_ref[h, rows, :], weff)


def _gdn_scan(q, k, v, ab, *, rev, nc, tm, out_dtype):
    b, n_heads, t, dk = q.shape
    dv = v.shape[-1]
    nblk = t // tm
    nl = nblk - nc
    nck = tm // CHUNK
    d = 1 if rev else 0
    blk = lambda s: _scan_block_index(s, nc, nl, rev)
    return pl.pallas_call(
        functools.partial(_gdn_scan_kernel, rev=rev, d=d, n_heads=n_heads, nck=nck, tm=tm),
        grid=(b, nblk),
        in_specs=[pl.BlockSpec((1, n_heads, tm, dk), lambda i, s: (i, 0, blk(s), 0)),
                  pl.BlockSpec((1, n_heads, tm, dk), lambda i, s: (i, 0, blk(s), 0)),
                  pl.BlockSpec((1, n_heads, tm, dv), lambda i, s: (i, 0, blk(s), 0)),
                  pl.BlockSpec((1, tm, LANES), lambda i, s: (i, blk(s), 0))],
        out_specs=pl.BlockSpec((1, n_heads, tm, dv), lambda i, s: (i, 0, blk(s), 0)),
        out_shape=jax.ShapeDtypeStruct((b, n_heads, t, dv), out_dtype),
        scratch_shapes=[pltpu.VMEM((n_heads, dk, dv), F32),
                        pltpu.VMEM((n_heads, tm, 1), F32),
                        pltpu.VMEM((n_heads, tm, 1), F32),
                        pltpu.VMEM((n_heads, nck, 1, CHUNK), F32),
                        pltpu.VMEM((n_heads, tm, dv), F32),
                        pltpu.VMEM((n_heads, tm, dk), BF16),
                        pltpu.VMEM((n_heads, tm, CHUNK), BF16),
                        pltpu.VMEM((n_heads, tm, dk), BF16),
                        pltpu.VMEM((n_heads, tm, dk), BF16),
                        pltpu.VMEM((n_heads, nck, 1, LANES), F32)],
        compiler_params=_cp("parallel", "arbitrary"),
        name="gdn_scan_bwd" if rev else "gdn_scan_fwd",
    )(q, k, v, ab)


def _route_kernel(x_ref, mod_ref, g_ref, wrt_ref, h_ref, slot_ref, aff_ref, lg_ref,
                  *, n_ctx, cap_c, cap_l, tr):
    n_exp, t = lg_ref.shape
    for rt in range(t // tr):
        m = mod_ref[0, 0 if rt * tr < n_ctx else 1]
        rows = slice(rt * tr, (rt + 1) * tr)
        h = _norm_mod(x_ref[0, rows, :], g_ref[...], m[3:4], m[4:5])
        h_ref[0, rows, :] = h.astype(BF16)
        lg_ref[:, rows] = _dot_nt_hi(wrt_ref[...], h)
    lg = lg_ref[...]
    e = jnp.exp(lg - jnp.max(lg, axis=0, keepdims=True))
    aff = e / jnp.sum(e, axis=0, keepdims=True)
    aff_ref[0] = aff
    bits = lax.bitcast_convert_type(aff, jnp.int32)
    lane = _iota((n_exp, t), 1)

    if n_ctx:
        is_ctx = lane < n_ctx
        regions = [(is_ctx, cap_c), (jnp.logical_not(is_ctx), cap_l)]
    else:
        regions = [(None, cap_l)]

    def count(pred, mask):
        p = pred if mask is None else (pred & mask)
        return jnp.sum(jnp.where(p, 1.0, 0.0), axis=1, keepdims=True)

    def thr_body(i, thrs):
        bit = lax.shift_left(jnp.int32(1), AFF_BITS - 1 - i)
        out = []
        for (mask, cap), thr in zip(regions, thrs):
            cand = thr | bit
            out.append(jnp.where(count(bits >= cand, mask) >= cap, cand, thr))
        return tuple(out)

    zero = jnp.zeros((n_exp, 1), jnp.int32)
    thrs = lax.fori_loop(0, AFF_BITS, thr_body, tuple(zero for _ in regions))

    idx_bits = t.bit_length()
    sel = None
    for (mask, cap), thr in zip(regions, thrs):
        gt = bits > thr
        tie = bits == thr
        need = cap - count(gt, mask)

        def j_body(i, j, tie=tie, mask=mask, need=need):
            cand = j | lax.shift_left(jnp.int32(1), idx_bits - 1 - i)
            return jnp.where(count(tie & (lane < cand), mask) <= need, cand, j)

        jmax = lax.fori_loop(0, idx_bits, j_body, zero)
        s = gt | (tie & (lane < jmax))
        if mask is not None:
            s = s & mask
        sel = s if sel is None else (sel | s)

    lt = LANES if t % LANES == 0 else CHUNK
    ut = jnp.where(_iota((lt, lt), 0) < _iota((lt, lt), 1), 1.0, 0.0).astype(BF16)
    self = jnp.where(sel, 1.0, 0.0)
    run = jnp.zeros((n_exp, 1), F32)
    pres = []
    for i in range(t // lt):
        tile = self[:, i * lt:(i + 1) * lt]
        pres.append(_dot(tile.astype(BF16), ut) + run)
        run = run + jnp.sum(tile, axis=1, keepdims=True)
    pre = jnp.concatenate(pres, axis=1)
    if n_ctx:
        slot = jnp.where(is_ctx, pre + cap_l, pre - cap_c)
    else:
        slot = pre
    slot_ref[0] = jnp.where(sel, slot, -1.0)


def _route(x, modsel, g, w_router, *, n_ctx, cap_c, cap_l, tr):
    b, t, d = x.shape
    n_exp = w_router.shape[-1]
    wrt = w_router.T
    g2 = g.reshape(1, d)
    return pl.pallas_call(
        functools.partial(_route_kernel, n_ctx=n_ctx, cap_c=cap_c, cap_l=cap_l, tr=tr),
        grid=(b,),
        in_specs=[pl.BlockSpec((1, t, d), lambda i: (i, 0, 0)),
                  pl.BlockSpec((1, 2, 6, d), lambda i: (i, 0, 0, 0)),
                  pl.BlockSpec((1, d), lambda i: (0, 0)),
                  pl.BlockSpec((n_exp, d), lambda i: (0, 0))],
        out_specs=[pl.BlockSpec((1, t, d), lambda i: (i, 0, 0)),
                   pl.BlockSpec((1, n_exp, t), lambda i: (i, 0, 0)),
                   pl.BlockSpec((1, n_exp, t), lambda i: (i, 0, 0))],
        out_shape=[jax.ShapeDtypeStruct((b, t, d), BF16),
                   jax.ShapeDtypeStruct((b, n_exp, t), F32),
                   jax.ShapeDtypeStruct((b, n_exp, t), F32)],
        scratch_shapes=[pltpu.VMEM((n_exp, t), F32)],
        compiler_params=_cp("parallel"),
        name="moe_route",
    )(x, modsel, g2, wrt)


def _gather_kernel(slot_ref, h_ref, x_ref, *, n_ctx, cap_c, cap_l):
    slot = slot_ref[0, 0]
    t = slot.shape[1]
    sl = slot[:, n_ctx:]
    p = jnp.where(sl == _iota((cap_l, t - n_ctx), 0).astype(F32), 1.0, 0.0).astype(BF16)
    x_ref[0, 0, 0:cap_l, :] = _dot(p, h_ref[0, n_ctx:, :]).astype(BF16)
    if n_ctx:
        sc = slot[:, :n_ctx] - float(cap_l)
        p = jnp.where(sc == _iota((cap_c, n_ctx), 0).astype(F32), 1.0, 0.0).astype(BF16)
        x_ref[0, 0, cap_l:, :] = _dot(p, h_ref[0, :n_ctx, :]).astype(BF16)


def _gather(h, slot_row, *, n_ctx, cap_c, cap_l):
    b, t, d = h.shape
    n_exp = slot_row.shape[1]
    m = cap_l + (cap_c if n_ctx else 0)
    return pl.pallas_call(
        functools.partial(_gather_kernel, n_ctx=n_ctx, cap_c=cap_c, cap_l=cap_l),
        grid=(b, n_exp),
        in_specs=[pl.BlockSpec((1, 1, 1, t), lambda i, e: (i, e, 0, 0)),
                  pl.BlockSpec((1, t, d), lambda i, e: (i, 0, 0))],
        out_specs=pl.BlockSpec((1, 1, m, d), lambda i, e: (e, i, 0, 0)),
        out_shape=jax.ShapeDtypeStruct((n_exp, b, m, d), BF16),
        compiler_params=_cp("parallel", "parallel"),
        name="moe_gather",
    )(slot_row.reshape(b, n_exp, 1, t), h)


def _ffn_kernel(x_ref, wg_ref, wu_ref, wd_ref, y_ref, acc_ref, wgb_ref, wub_ref, wdb_ref, *, tr):
    ft = pl.program_id(1)

    @pl.when(ft == 0)
    def _():
        acc_ref[...] = jnp.zeros(acc_ref.shape, F32)

    wgb_ref[...] = wg_ref[0].astype(BF16)
    wub_ref[...] = wu_ref[0].astype(BF16)
    wdb_ref[...] = wd_ref[0].astype(BF16)

    def body(i, carry):
        rows = pl.ds(pl.multiple_of(i * tr, tr), tr)
        x = x_ref[0, rows, :]
        hid = (_silu(_dot(x, wgb_ref[...])) * _dot(x, wub_ref[...])).astype(BF16)
        acc_ref[rows, :] += _dot(hid, wdb_ref[...])
        return carry

    lax.fori_loop(0, x_ref.shape[1] // tr, body, 0)

    @pl.when(ft == pl.num_programs(1) - 1)
    def _():
        y_ref[0] = acc_ref[...].astype(y_ref.dtype)


def _ffn(xs, w_gate, w_up, w_down, *, tr):
    n_exp, bm, d = xs.shape
    f = w_gate.shape[-1]
    tf = FFN_TILE if f % FFN_TILE == 0 else f
    return pl.pallas_call(
        functools.partial(_ffn_kernel, tr=tr),
        grid=(n_exp, f // tf),
        in_specs=[pl.BlockSpec((1, bm, d), lambda e, j: (e, 0, 0)),
                  pl.BlockSpec((1, d, tf), lambda e, j: (e, 0, j)),
                  pl.BlockSpec((1, d, tf), lambda e, j: (e, 0, j)),
                  pl.BlockSpec((1, tf, d), lambda e, j: (e, j, 0))],
        out_specs=pl.BlockSpec((1, bm, d), lambda e, j: (e, 0, 0)),
        out_shape=jax.ShapeDtypeStruct((n_exp, bm, d), BF16),
        scratch_shapes=[pltpu.VMEM((bm, d), F32),
                        pltpu.VMEM((d, tf), BF16),
                        pltpu.VMEM((d, tf), BF16),
                        pltpu.VMEM((tf, d), BF16)],
        compiler_params=_cp("parallel", "arbitrary"),
        name="moe_ffn",
    )(xs, w_gate, w_up, w_down)


def _combine_kernel(*refs, nc, cap_c, cap_l, final):
    if final:
        x_ref, ys_ref, sc_ref, ac_ref, mod_ref, fg_ref, o_ref, acc_ref = refs
    else:
        x_ref, ys_ref, sc_ref, ac_ref, mod_ref, o_ref, acc_ref = refs
    n_exp = ys_ref.shape[0]
    tm = x_ref.shape[1]

    def run(k0, kk):
        sc = sc_ref[0]
        ac = ac_ref[0]
        lane = (_iota((tm, kk), 1) + k0).astype(F32)
        acc_ref[...] = jnp.zeros(acc_ref.shape, F32)
        for e in range(n_exp):
            pt = jnp.where(sc[:, e:e + 1] == lane, 1.0, 0.0).astype(BF16)
            acc_ref[...] += ac[:, e:e + 1] * _dot(pt, ys_ref[e, 0, k0:k0 + kk, :])
        x2 = x_ref[0] + mod_ref[0, 0][5:6] * acc_ref[...]
        if final:
            x2 = x2 * lax.rsqrt(jnp.mean(x2 * x2, axis=-1, keepdims=True) + EPS) * fg_ref[...]
        o_ref[0] = x2

    if nc:
        t = pl.program_id(1)

        @pl.when(t < nc)
        def _():
            run(cap_l, cap_c)

        @pl.when(t >= nc)
        def _():
            run(0, cap_l)
    else:
        run(0, cap_l)


def _combine(x, ys, slot_col, aff_col, modsel, final_g, *, nc, cap_c, cap_l, tm):
    b, t, d = x.shape
    n_exp, _, m, _ = ys.shape
    seg = (lambda j: jnp.where(j >= nc, 1, 0)) if nc else (lambda j: 1)
    in_specs = [pl.BlockSpec((1, tm, d), lambda i, j: (i, j, 0)),
                pl.BlockSpec((n_exp, 1, m, d), lambda i, j: (0, i, 0, 0)),
                pl.BlockSpec((1, tm, n_exp), lambda i, j: (i, j, 0)),
                pl.BlockSpec((1, tm, n_exp), lambda i, j: (i, j, 0)),
                pl.BlockSpec((1, 1, 6, d), lambda i, j: (i, seg(j), 0, 0))]
    args = [x, ys, slot_col, aff_col, modsel]
    if final_g is not None:
        in_specs.append(pl.BlockSpec((1, d), lambda i, j: (0, 0)))
        args.append(final_g.reshape(1, d))
    return pl.pallas_call(
        functools.partial(_combine_kernel, nc=nc, cap_c=cap_c, cap_l=cap_l,
                          final=final_g is not None),
        grid=(b, t // tm),
        in_specs=in_specs,
        out_specs=pl.BlockSpec((1, tm, d), lambda i, j: (i, j, 0)),
        out_shape=jax.ShapeDtypeStruct((b, t, d), F32),
        scratch_shapes=[pltpu.VMEM((tm, d), F32)],
        compiler_params=_cp("parallel", "parallel"),
        name="moe_combine",
    )(*args)


def _moe(x, modsel, g, w_router, w_gate, w_up, w_down, final_g, *, n_ctx, n_lat, tm):
    b, t, d = x.shape
    n_exp = w_router.shape[-1]
    cap_l = EC_CAPACITY_FACTOR * n_lat // n_exp
    cap_c = EC_CAPACITY_FACTOR * n_ctx // n_exp
    h, slot_row, aff_row = _route(x, modsel, g, w_router, n_ctx=n_ctx, cap_c=cap_c, cap_l=cap_l,
                                  tr=tm)
    xs = _gather(h, slot_row, n_ctx=n_ctx, cap_c=cap_c, cap_l=cap_l)
    m = xs.shape[2]
    ys = _ffn(xs.reshape(n_exp, b * m, d), w_gate, w_up, w_down, tr=min(MAX_BLOCK_ROWS, m))
    return _combine(x, ys.reshape(n_exp, b, m, d), jnp.swapaxes(slot_row, 1, 2),
                    jnp.swapaxes(aff_row, 1, 2), modsel, final_g,
                    nc=n_ctx // tm, cap_c=cap_c, cap_l=cap_l, tm=tm)


def _raster_to_column(t):
    b, l, d = t.shape
    rows = l // GRID_W
    return t.reshape(b, rows, GRID_W, d).transpose(0, 2, 1, 3).reshape(b, l, d)


def _column_to_raster(t):
    b, l, d = t.shape
    rows = l // GRID_W
    return t.reshape(b, GRID_W, rows, d).transpose(0, 2, 1, 3).reshape(b, l, d)


def kernel(x, c, ctx, c_ctx, w_mod, b_mod, norm_g, hg_w_in, hg_lb, hg_onorm, hg_w_out, gd_w_in,
           gd_conv, gd_a_log, gd_dt_bias, gd_onorm, gd_w_out, moe_router, moe_w_gate, moe_w_up,
           moe_w_down, final_g):
    bsz, n_lat, d = x.shape
    n_ctx = ctx.shape[1]
    depth = w_mod.shape[0]
    assert depth == 2, "layer 0 = HGRN2, layer 1 = gated DeltaNet"
    tm = math.gcd(math.gcd(n_ctx, n_lat), MAX_BLOCK_ROWS)
    assert tm % CHUNK == 0
    nc = n_ctx // tm
    o_dt = BF16

    rows = -(-(bsz + 1) // SUBLANES) * SUBLANES
    cond = jnp.zeros((rows, d), F32).at[:bsz].set(c).at[bsz].set(c_ctx)
    mod = _adaln(cond, w_mod, b_mod).reshape(depth, rows, 6, d)

    def modsel(i):
        ctx_mod = jnp.broadcast_to(mod[i, bsz][None], (bsz, 6, d))
        return jnp.stack([ctx_mod, mod[i, :bsz]], axis=1)

    x_all = jnp.concatenate([ctx, x], axis=1)

    ms = modsel(0)
    q, lf, v, gate = _hg_inproj(x_all, ms, norm_g[0, 0], hg_lb, hg_w_in[0], layer=0, nc=nc, tm=tm)
    o_f = _gla_scan(q, lf, v, rev=False, nc=nc, tm=tm, out_dtype=o_dt)
    o_b = _gla_scan(q, lf, v, rev=True, nc=nc, tm=tm, out_dtype=o_dt)
    x_all = _outproj(o_f, o_b, gate, hg_onorm[0], hg_w_out[0], x_all, ms, nc=nc, tm=tm,
                     skip_ctx=False)
    x_all = _moe(x_all, ms, norm_g[0, 1], moe_router[0], moe_w_gate[0], moe_w_up[0],
                 moe_w_down[0], None, n_ctx=n_ctx, n_lat=n_lat, tm=tm)

    ms = modsel(1)
    x_cm = jnp.concatenate([x_all[:, :n_ctx], _raster_to_column(x_all[:, n_ctx:])], axis=1)
    q, k, v, gate, ab = _gd_inproj(x_cm, ms, norm_g[1, 0], gd_w_in[0], gd_conv[0], gd_a_log[0],
                                   gd_dt_bias[0], nc=nc, tm=tm)
    o_f = _gdn_scan(q, k, v, ab, rev=False, nc=nc, tm=tm, out_dtype=o_dt)
    o_b = _gdn_scan(q, k, v, ab, rev=True, nc=nc, tm=tm, out_dtype=o_dt)
    x_lat = _outproj(o_f, o_b, gate, gd_onorm[0], gd_w_out[0], x_cm, ms, nc=nc, tm=tm,
                     skip_ctx=True)
    x_lat = _column_to_raster(x_lat)
    return _moe(x_lat, ms, norm_g[1, 1], moe_router[1], moe_w_gate[1], moe_w_up[1],
                moe_w_down[1], final_g, n_ctx=0, n_lat=n_lat, tm=tm)
```

```python
import functools
import math

import jax
import jax.numpy as jnp
from jax import lax
from jax.experimental import pallas as pl
from jax.experimental.pallas import tpu as pltpu

F32 = jnp.float32
BF16 = jnp.bfloat16
HIGHEST = lax.Precision.HIGHEST

EPS = 1e-6
LANES = 128
SUBLANES = 8
V7X_VMEM_BYTES = 64 * 1024 * 1024
VMEM_LIMIT = V7X_VMEM_BYTES * 7 // 8

HEAD_DK = 128
GD_DV = 256
GD_CONV_W = 5
GRID_W = 64
EC_CAPACITY_FACTOR = 2
CHUNK = 64
SUB = 16
EXP_CLAMP = 60.0
MAX_BLOCK_ROWS = 256
FFN_TILE = 256
AFF_BITS = 31


def _cp(*sem):
    return pltpu.CompilerParams(dimension_semantics=sem, vmem_limit_bytes=VMEM_LIMIT)


def _dot(a, b):
    return jnp.dot(a, b, preferred_element_type=F32)


def _dot_nt(a, b):
    return lax.dot_general(a, b, (((1,), (1,)), ((), ())), preferred_element_type=F32)


def _dot_tn(a, b):
    return lax.dot_general(a, b, (((0,), (0,)), ((), ())), preferred_element_type=F32)


def _dot_hi(a, b):
    return jnp.dot(a, b, preferred_element_type=F32, precision=HIGHEST)


def _dot_nt_hi(a, b):
    return lax.dot_general(a, b, (((1,), (1,)), ((), ())), preferred_element_type=F32,
                           precision=HIGHEST)


def _silu(x):
    return x * jax.nn.sigmoid(x)


def _norm_mod(x, g, shift, scale):
    y = x * lax.rsqrt(jnp.mean(x * x, axis=-1, keepdims=True) + EPS)
    return y * g * (1.0 + scale) + shift


def _iota(shape, dim):
    return lax.broadcasted_iota(jnp.int32, shape, dim)


def _split3(x):
    hi = x.astype(BF16)
    r1 = x - hi.astype(F32)
    mid = r1.astype(BF16)
    lo = (r1 - mid.astype(F32)).astype(BF16)
    return [hi, mid, lo]


def _adaln_kernel(c_ref, w_ref, b_ref, o_ref):
    o_ref[0] = _dot_hi(_silu(c_ref[...]), w_ref[0]) + b_ref[0]


def _adaln(cond, w_mod, b_mod):
    depth, d, n = w_mod.shape
    r = cond.shape[0]
    tn = n // 12
    return pl.pallas_call(
        _adaln_kernel,
        grid=(depth, n // tn),
        in_specs=[pl.BlockSpec((r, d), lambda i, j: (0, 0)),
                  pl.BlockSpec((1, d, tn), lambda i, j: (i, 0, j)),
                  pl.BlockSpec((1, 1, tn), lambda i, j: (i, 0, j))],
        out_specs=pl.BlockSpec((1, r, tn), lambda i, j: (i, 0, j)),
        out_shape=jax.ShapeDtypeStruct((depth, r, n), F32),
        compiler_params=_cp("parallel", "parallel"),
        name="adaln",
    )(cond, w_mod, b_mod.reshape(depth, 1, n))


def _hg_inproj_kernel(x_ref, mod_ref, g_ref, lb_ref, wq_ref, wf_ref, wi_ref, wg_ref,
                      q_ref, lf_ref, v_ref, gate_ref, *, layer, n_heads):
    m = mod_ref[0, 0]
    hb = _norm_mod(x_ref[0], g_ref[...], m[0:1], m[1:2]).astype(BF16)
    fdim = wq_ref.shape[1]
    dk = fdim // n_heads
    dv = wi_ref.shape[1] // n_heads

    q = _silu(_dot(hb, wq_ref[...]))
    for h in range(n_heads):
        q_ref[0, h] = q[:, h * dk:(h + 1) * dk].astype(BF16)

    lbp = lb_ref[...]
    e = jnp.exp(lbp - jnp.max(lbp, axis=0))
    lb = jnp.sum(e[:layer + 1], axis=0) / jnp.sum(e, axis=0)

    zf = _dot(hb, wf_ref[...])
    for d in range(2):
        lbd = lb[d:d + 1]
        f = lbd + (1.0 - lbd) * jax.nn.sigmoid(zf[:, d * fdim:(d + 1) * fdim])
        lf = jnp.log(f)
        for h in range(n_heads):
            lf_ref[d, 0, h] = lf[:, h * dk:(h + 1) * dk]

    v = _dot(hb, wi_ref[...])
    for h in range(n_heads):
        v_ref[0, h] = v[:, h * dv:(h + 1) * dv].astype(BF16)

    gate_ref[0] = _silu(_dot(hb, wg_ref[...])).astype(BF16)


def _hg_inproj(x_all, modsel, g, hg_lb, w_in, *, layer, nc, tm):
    b, t, d = x_all.shape
    fdim = hg_lb.shape[-1]
    n_heads = fdim // HEAD_DK
    dv = d // n_heads
    wq = w_in[:, :fdim].astype(BF16)
    wf = w_in[:, fdim:3 * fdim].astype(BF16)
    wi = w_in[:, 3 * fdim:3 * fdim + d].astype(BF16)
    wg = w_in[:, 3 * fdim + d:].astype(BF16)
    full = lambda a: pl.BlockSpec(a.shape, lambda i, j: (0,) * a.ndim)
    g2 = g.reshape(1, d)
    return pl.pallas_call(
        functools.partial(_hg_inproj_kernel, layer=layer, n_heads=n_heads),
        grid=(b, t // tm),
        in_specs=[pl.BlockSpec((1, tm, d), lambda i, j: (i, j, 0)),
                  pl.BlockSpec((1, 1, 6, d), lambda i, j: (i, jnp.where(j >= nc, 1, 0), 0, 0)),
                  full(g2), full(hg_lb), full(wq), full(wf), full(wi), full(wg)],
        out_specs=[pl.BlockSpec((1, n_heads, tm, HEAD_DK), lambda i, j: (i, 0, j, 0)),
                   pl.BlockSpec((2, 1, n_heads, tm, HEAD_DK), lambda i, j: (0, i, 0, j, 0)),
                   pl.BlockSpec((1, n_heads, tm, dv), lambda i, j: (i, 0, j, 0)),
                   pl.BlockSpec((1, tm, d), lambda i, j: (i, j, 0))],
        out_shape=[jax.ShapeDtypeStruct((b, n_heads, t, HEAD_DK), BF16),
                   jax.ShapeDtypeStruct((2, b, n_heads, t, HEAD_DK), F32),
                   jax.ShapeDtypeStruct((b, n_heads, t, dv), BF16),
                   jax.ShapeDtypeStruct((b, t, d), BF16)],
        compiler_params=_cp("parallel", "parallel"),
        name="hg_inproj",
    )(x_all, modsel, g2, hg_lb, wq, wf, wi, wg)


def _scan_block_index(step, nc, nl, rev):
    if not rev:
        return step
    return jnp.where(step < nc, nc - 1 - step, nc + nl - 1 - (step - nc))


def _chunk_masks(rev):
    r = _iota((CHUNK, CHUNK), 0)
    c = _iota((CHUNK, CHUNK), 1)
    incl = (c >= r) if rev else (c <= r)
    strict = (c > r) if rev else (c < r)
    return incl, strict


def _gla_scan_kernel(q_ref, lf_ref, v_ref, o_ref, st_ref, oin_ref, qc_ref, kh_ref,
                     *, rev, n_heads, nck, tm, group):
    @pl.when(pl.program_id(1) == 0)
    def _():
        st_ref[...] = jnp.zeros(st_ref.shape, F32)

    incl, _ = _chunk_masks(rev)
    dk = q_ref.shape[-1]
    r = _iota((tm, tm), 0)
    c = _iota((tm, tm), 1)
    tri = ((r // CHUNK) == (c // CHUNK)) & ((c >= r) if rev else (c <= r))
    lmb = jnp.where(tri, 1.0, 0.0).astype(BF16)

    dec = {}
    for h0 in range(0, n_heads, group):
        hs = range(h0, h0 + group)
        b = {}
        for h in hs:
            g3 = _dot(lmb, jnp.concatenate(_split3(lf_ref[0, 0, h]), axis=1))
            b[h] = g3[:, :dk] + g3[:, dk:2 * dk] + g3[:, 2 * dk:]
        for h in hs:
            g = lf_ref[0, 0, h]
            q = q_ref[0, h].astype(F32)
            k = 1.0 - jnp.exp(g)
            be = b[h] - g
            bls = [b[h][ci * CHUNK:ci * CHUNK + 1] if rev
                   else b[h][(ci + 1) * CHUNK - 1:(ci + 1) * CHUNK] for ci in range(nck)]
            blc = jnp.concatenate([jnp.broadcast_to(x, (CHUNK, dk)) for x in bls], axis=0)
            qc_ref[h] = (q * jnp.exp(b[h])).astype(BF16)
            kh_ref[h] = (k * jnp.exp(blc - b[h])).astype(BF16)
            dec[h] = [jnp.exp(x) for x in bls]
            for ci in range(nck):
                c0 = ci * CHUNK
                bc = b[h][c0:c0 + CHUNK]
                kc = k[c0:c0 + CHUNK]
                atts = []
                for i in range(CHUNK // SUB):
                    r0 = c0 + i * SUB
                    bref = be[r0 + SUB - 1:r0 + SUB] if rev else be[r0:r0 + 1]
                    qi = (q[r0:r0 + SUB] * jnp.exp(b[h][r0:r0 + SUB] - bref)).astype(BF16)
                    kt = (kc * jnp.exp(jnp.minimum(bref - bc, EXP_CLAMP))).astype(BF16)
                    atts.append(_dot_nt(qi, kt))
                att = jnp.where(incl, jnp.concatenate(atts, axis=0), 0.0).astype(BF16)
                oin_ref[h, c0:c0 + CHUNK, :] = _dot(att, v_ref[0, h, c0:c0 + CHUNK, :])

    for step in range(nck):
        ci = nck - 1 - step if rev else step
        rows = pl.ds(ci * CHUNK, CHUNK)
        for h in range(n_heads):
            st = st_ref[h]
            o = oin_ref[h, rows, :] + _dot_nt(qc_ref[h, rows, :], st.astype(BF16))
            o_ref[0, h, rows, :] = o.astype(o_ref.dtype)
            st_ref[h] = st * dec[h][ci] + _dot_tn(v_ref[0, h, rows, :], kh_ref[h, rows, :])


def _gla_scan(q, lf, v, *, rev, nc, tm, out_dtype):
    b, n_heads, t, dk = q.shape
    dv = v.shape[-1]
    nblk = t // tm
    nl = nblk - nc
    nck = tm // CHUNK
    d = 1 if rev else 0
    blk = lambda i, s: _scan_block_index(s, nc, nl, rev)
    return pl.pallas_call(
        functools.partial(_gla_scan_kernel, rev=rev, n_heads=n_heads, nck=nck, tm=tm,
                          group=math.gcd(n_heads, 4)),
        grid=(b, nblk),
        in_specs=[pl.BlockSpec((1, n_heads, tm, dk), lambda i, s: (i, 0, blk(i, s), 0)),
                  pl.BlockSpec((1, 1, n_heads, tm, dk), lambda i, s: (d, i, 0, blk(i, s), 0)),
                  pl.BlockSpec((1, n_heads, tm, dv), lambda i, s: (i, 0, blk(i, s), 0))],
        out_specs=pl.BlockSpec((1, n_heads, tm, dv), lambda i, s: (i, 0, blk(i, s), 0)),
        out_shape=jax.ShapeDtypeStruct((b, n_heads, t, dv), out_dtype),
        scratch_shapes=[pltpu.VMEM((n_heads, dv, dk), F32),
                        pltpu.VMEM((n_heads, tm, dv), F32),
                        pltpu.VMEM((n_heads, tm, dk), BF16),
                        pltpu.VMEM((n_heads, tm, dk), BF16)],
        compiler_params=_cp("parallel", "arbitrary"),
        name="gla_scan_bwd" if rev else "gla_scan_fwd",
    )(q, lf, v)


def _outproj_kernel(of_ref, ob_ref, gate_ref, gain_ref, w_ref, x_ref, mod_ref, o_ref, y_ref,
                    *, n_heads):
    dv = of_ref.shape[-1]
    for h in range(n_heads):
        o = of_ref[0, h].astype(F32) + ob_ref[0, h].astype(F32)
        cs = slice(h * dv, (h + 1) * dv)
        o = o * lax.rsqrt(jnp.mean(o * o, axis=-1, keepdims=True) + EPS) * gain_ref[:, cs]
        y_ref[:, cs] = (o * gate_ref[0, :, cs].astype(F32)).astype(BF16)
    m = mod_ref[0, 0]
    o_ref[0] = x_ref[0] + m[2:3] * _dot(y_ref[...], w_ref[...])


def _outproj(o_f, o_b, gate, gain, w_out, x_all, modsel, *, nc, tm, skip_ctx):
    b, n_heads, t, dv = o_f.shape
    d = x_all.shape[-1]
    hv = n_heads * dv
    off = nc if skip_ctx else 0
    nblk = t // tm - off
    seg = (lambda j: 1) if skip_ctx else (lambda j: jnp.where(j >= nc, 1, 0))
    w = w_out.astype(BF16)
    gain2 = gain.reshape(1, hv)
    return pl.pallas_call(
        functools.partial(_outproj_kernel, n_heads=n_heads),
        grid=(b, nblk),
        in_specs=[pl.BlockSpec((1, n_heads, tm, dv), lambda i, j: (i, 0, j + off, 0)),
                  pl.BlockSpec((1, n_heads, tm, dv), lambda i, j: (i, 0, j + off, 0)),
                  pl.BlockSpec((1, tm, hv), lambda i, j: (i, j + off, 0)),
                  pl.BlockSpec((1, hv), lambda i, j: (0, 0)),
                  pl.BlockSpec((hv, d), lambda i, j: (0, 0)),
                  pl.BlockSpec((1, tm, d), lambda i, j: (i, j + off, 0)),
                  pl.BlockSpec((1, 1, 6, d), lambda i, j: (i, seg(j), 0, 0))],
        out_specs=pl.BlockSpec((1, tm, d), lambda i, j: (i, j, 0)),
        out_shape=jax.ShapeDtypeStruct((b, nblk * tm, d), F32),
        scratch_shapes=[pltpu.VMEM((tm, hv), BF16)],
        compiler_params=_cp("parallel", "parallel"),
        name="outproj",
    )(o_f, o_b, gate, gain2, w, x_all, modsel)


def _gd_inproj_kernel(xp_ref, x_ref, xn_ref, mod_ref, g_ref, wqkv_ref, wg_ref, wab_ref, cw_ref,
                      alog_ref, dtb_ref, q_ref, k_ref, v_ref, gate_ref, ab_ref, zs_ref,
                      *, nc, nblk, n_heads, tm, cw):
    t = pl.program_id(1)
    first = (t == 0) | (t == nc)
    last = (t == nc - 1) | (t == nblk - 1)
    m = mod_ref[0, 0]
    halo = SUBLANES
    xe = jnp.concatenate([xp_ref[0], x_ref[0], xn_ref[0]], axis=0)
    he = _norm_mod(xe, g_ref[...], m[0:1], m[1:2]).astype(BF16)
    hc = he[halo:halo + tm]
    row = _iota((tm + 2 * halo, 1), 0)
    valid = ((row >= halo) | jnp.logical_not(first)) & ((row < tm + halo) | jnp.logical_not(last))

    gate_ref[0] = _silu(_dot(hc, wg_ref[...])).astype(BF16)

    zab = _dot(hc, wab_ref[...])
    lane = _iota(zab.shape, 1)
    loga = -jnp.exp(alog_ref[...]) * jax.nn.softplus(zab + dtb_ref[...])
    ab_ref[0] = jnp.where(lane < 2 * n_heads, loga, jax.nn.sigmoid(zab))

    qk = n_heads * HEAD_DK
    nchan = wqkv_ref.shape[1]
    pad = GD_CONV_W // 2
    for cc in range(nchan // cw):
        c0 = cc * cw
        z = _dot(he, wqkv_ref[:, c0:c0 + cw])
        zs_ref[...] = jnp.where(valid, z, 0.0)
        acc = cw_ref[0:1, c0:c0 + cw] * zs_ref[halo - pad:halo - pad + tm, :]
        for j in range(1, GD_CONV_W):
            acc = acc + cw_ref[j:j + 1, c0:c0 + cw] * zs_ref[halo - pad + j:halo - pad + j + tm, :]
        u = _silu(acc)
        if c0 < 2 * qk:
            dst, base, scale = (q_ref, c0, HEAD_DK ** -0.5) if c0 < qk else (k_ref, c0 - qk, 1.0)
            for j in range(cw // HEAD_DK):
                tt = u[:, j * HEAD_DK:(j + 1) * HEAD_DK]
                n = tt * lax.rsqrt(jnp.sum(tt * tt, axis=-1, keepdims=True) + EPS)
                dst[0, base // HEAD_DK + j] = (n * scale).astype(BF16)
        else:
            base = c0 - 2 * qk
            for j in range(cw // GD_DV):
                v_ref[0, base // GD_DV + j] = u[:, j * GD_DV:(j + 1) * GD_DV].astype(BF16)


def _gd_inproj(x_all, modsel, g, w_in, conv_w, a_log, dt_bias, *, nc, tm):
    b, t, d = x_all.shape
    n_heads = a_log.shape[-1]
    qk = n_heads * HEAD_DK
    vd = n_heads * GD_DV
    nchan = 2 * qk + vd
    nblk = t // tm
    cw = min(512, qk)
    wqkv = w_in[:, :nchan].astype(BF16)
    wg = w_in[:, nchan:nchan + vd].astype(BF16)
    wab = jnp.pad(w_in[:, nchan + vd:], ((0, 0), (0, LANES - 4 * n_heads))).astype(BF16)
    alog = jnp.pad(a_log.reshape(1, 2 * n_heads), ((0, 0), (0, LANES - 2 * n_heads)))
    dtb = jnp.pad(dt_bias.reshape(1, 2 * n_heads), ((0, 0), (0, LANES - 2 * n_heads)))
    g2 = g.reshape(1, d)
    full = lambda a: pl.BlockSpec(a.shape, lambda i, j: (0,) * a.ndim)
    spb = tm // SUBLANES
    nslab = t // SUBLANES
    return pl.pallas_call(
        functools.partial(_gd_inproj_kernel, nc=nc, nblk=nblk, n_heads=n_heads, tm=tm, cw=cw),
        grid=(b, nblk),
        in_specs=[pl.BlockSpec((1, SUBLANES, d), lambda i, j: (i, jnp.maximum(j * spb - 1, 0), 0)),
                  pl.BlockSpec((1, tm, d), lambda i, j: (i, j, 0)),
                  pl.BlockSpec((1, SUBLANES, d),
                               lambda i, j: (i, jnp.minimum((j + 1) * spb, nslab - 1), 0)),
                  pl.BlockSpec((1, 1, 6, d), lambda i, j: (i, jnp.where(j >= nc, 1, 0), 0, 0)),
                  full(g2), full(wqkv), full(wg), full(wab), full(conv_w), full(alog), full(dtb)],
        out_specs=[pl.BlockSpec((1, n_heads, tm, HEAD_DK), lambda i, j: (i, 0, j, 0)),
                   pl.BlockSpec((1, n_heads, tm, HEAD_DK), lambda i, j: (i, 0, j, 0)),
                   pl.BlockSpec((1, n_heads, tm, GD_DV), lambda i, j: (i, 0, j, 0)),
                   pl.BlockSpec((1, tm, vd), lambda i, j: (i, j, 0)),
                   pl.BlockSpec((1, tm, LANES), lambda i, j: (i, j, 0))],
        out_shape=[jax.ShapeDtypeStruct((b, n_heads, t, HEAD_DK), BF16),
                   jax.ShapeDtypeStruct((b, n_heads, t, HEAD_DK), BF16),
                   jax.ShapeDtypeStruct((b, n_heads, t, GD_DV), BF16),
                   jax.ShapeDtypeStruct((b, t, vd), BF16),
                   jax.ShapeDtypeStruct((b, t, LANES), F32)],
        scratch_shapes=[pltpu.VMEM((tm + 2 * SUBLANES, cw), F32)],
        compiler_params=_cp("parallel", "parallel"),
        name="gd_inproj",
    )(x_all, x_all, x_all, modsel, g2, wqkv, wg, wab, conv_w, alog, dtb)


def _neumann_inverse(a, eye):
    n = a.shape[0]
    t = eye - a
    ab = a.astype(BF16)
    p = _dot(ab, ab)
    lvl = 2
    while lvl < CHUNK:
        pb = p.astype(BF16)
        if 2 * lvl >= CHUNK:
            t = t + _dot(t.astype(BF16), pb)
        else:
            out = _dot(jnp.concatenate([pb, t.astype(BF16)], axis=0), pb)
            p = out[:n]
            t = t + out[n:]
        lvl *= 2
    return t


def _gdn_scan_kernel(q_ref, k_ref, v_ref, ab_ref, o_ref, s_ref, u_ref, w_ref, qkg_ref, qg_ref,
                     kd_ref, weff_ref, *, rev, d, n_heads, nck, tm, group):
    @pl.when(pl.program_id(1) == 0)
    def _():
        s_ref[...] = jnp.zeros(s_ref.shape, F32)

    r = _iota((tm, tm), 0)
    c = _iota((tm, tm), 1)
    same = (r // CHUNK) == (c // CHUNK)
    incl = same & ((c >= r) if rev else (c <= r))
    strict = same & ((c > r) if rev else (c < r))
    eye = jnp.where(r == c, 1.0, 0.0)
    dv = v_ref.shape[-1]

    ab = ab_ref[0]
    g3 = _dot(jnp.where(incl, 1.0, 0.0).astype(BF16), jnp.concatenate(_split3(ab), axis=1))
    gcol = g3[:, :LANES] + g3[:, LANES:2 * LANES] + g3[:, 2 * LANES:]
    eye_l = jnp.where(_iota((LANES, LANES), 0) == _iota((LANES, LANES), 1), 1.0, 0.0).astype(BF16)
    gr3 = _dot_nt(eye_l, jnp.concatenate(_split3(gcol), axis=0))
    grow = gr3[:, :tm] + gr3[:, tm:2 * tm] + gr3[:, 2 * tm:]

    egl = {}
    for h0 in range(0, n_heads, group):
        hs = range(h0, h0 + group)
        gc, bc, gam, a = {}, {}, {}, {}
        for h in hs:
            ca = d * n_heads + h
            gc[h] = gcol[:, ca:ca + 1]
            bc[h] = ab[:, 2 * n_heads + ca:2 * n_heads + ca + 1]
            gam[h] = jnp.where(incl, jnp.exp(jnp.minimum(gc[h] - grow[ca:ca + 1, :], 0.0)), 0.0)
        for h in hs:
            k = k_ref[0, h]
            qkk = _dot_nt(jnp.concatenate([q_ref[0, h], k], axis=0), k)
            qkg_ref[h] = (qkk[:tm] * gam[h]).astype(BF16)
            a[h] = jnp.where(strict, bc[h] * qkk[tm:] * gam[h], 0.0)
        tinv = {h: _neumann_inverse(a[h], eye) for h in hs}
        for h in hs:
            kf = k_ref[0, h].astype(F32)
            eg = jnp.exp(gc[h])
            rhs = jnp.concatenate([(bc[h] * v_ref[0, h].astype(F32)).astype(BF16),
                                   (bc[h] * eg * kf).astype(BF16)], axis=1)
            uw = _dot(tinv[h].astype(BF16), rhs)
            u_ref[h] = uw[:, :dv]
            w_ref[h] = uw[:, dv:].astype(BF16)
            qg_ref[h] = (q_ref[0, h].astype(F32) * eg).astype(BF16)
            gls = [gc[h][ci * CHUNK:ci * CHUNK + 1] if rev
                   else gc[h][(ci + 1) * CHUNK - 1:(ci + 1) * CHUNK] for ci in range(nck)]
            glc = jnp.concatenate([jnp.broadcast_to(g, (CHUNK, 1)) for g in gls], axis=0)
            kd_ref[h] = (kf * jnp.exp(glc - gc[h])).astype(BF16)
            egl[h] = [jnp.exp(g) for g in gls]

    weff_ref[...] = jnp.zeros(weff_ref.shape, BF16)
    for step in range(nck):
        ci = nck - 1 - step if rev else step
        rows = pl.ds(ci * CHUNK, CHUNK)
        for h in range(n_heads):
            s = s_ref[h]
            sb = s.astype(BF16)
            weff = (u_ref[h, rows, :] - _dot(w_ref[h, rows, :], sb)).astype(BF16)
            weff_ref[h, rows, :] = weff
            o = _dot(qkg_ref[h, rows, :], weff_ref[h]) + _dot(qg_ref[h, rows, :], sb)
            o_ref[0, h, rows, :] = o.astype(o_ref.dtype)
            s_ref[h] = s * egl[h][ci] + _dot_tn(kd_ref[h, rows, :], weff)


def _gdn_scan(q, k, v, ab, *, rev, nc, tm, out_dtype):
    b, n_heads, t, dk = q.shape
    dv = v.shape[-1]
    nblk = t // tm
    nl = nblk - nc
    nck = tm // CHUNK
    d = 1 if rev else 0
    group = math.gcd(n_heads, 4)
    blk = lambda s: _scan_block_index(s, nc, nl, rev)
    return pl.pallas_call(
        functools.partial(_gdn_scan_kernel, rev=rev, d=d, n_heads=n_heads, nck=nck, tm=tm,
                          group=group),
        grid=(b, nblk),
        in_specs=[pl.BlockSpec((1, n_heads, tm, dk), lambda i, s: (i, 0, blk(s), 0)),
                  pl.BlockSpec((1, n_heads, tm, dk), lambda i, s: (i, 0, blk(s), 0)),
                  pl.BlockSpec((1, n_heads, tm, dv), lambda i, s: (i, 0, blk(s), 0)),
                  pl.BlockSpec((1, tm, LANES), lambda i, s: (i, blk(s), 0))],
        out_specs=pl.BlockSpec((1, n_heads, tm, dv), lambda i, s: (i, 0, blk(s), 0)),
        out_shape=jax.ShapeDtypeStruct((b, n_heads, t, dv), out_dtype),
        scratch_shapes=[pltpu.VMEM((n_heads, dk, dv), F32),
                        pltpu.VMEM((n_heads, tm, dv), F32),
                        pltpu.VMEM((n_heads, tm, dk), BF16),
                        pltpu.VMEM((n_heads, tm, tm), BF16),
                        pltpu.VMEM((n_heads, tm, dk), BF16),
                        pltpu.VMEM((n_heads, tm, dk), BF16),
                        pltpu.VMEM((n_heads, tm, dv), BF16)],
        compiler_params=_cp("parallel", "arbitrary"),
        name="gdn_scan_bwd" if rev else "gdn_scan_fwd",
    )(q, k, v, ab)


def _route_kernel(x_ref, mod_ref, g_ref, wrt_ref, h_ref, slot_ref, aff_ref, lg_ref,
                  *, n_ctx, cap_c, cap_l, tr):
    n_exp, t = lg_ref.shape
    for rt in range(t // tr):
        m = mod_ref[0, 0 if rt * tr < n_ctx else 1]
        rows = slice(rt * tr, (rt + 1) * tr)
        h = _norm_mod(x_ref[0, rows, :], g_ref[...], m[3:4], m[4:5])
        h_ref[0, rows, :] = h.astype(BF16)
        lg_ref[:, rows] = _dot_nt_hi(wrt_ref[...], h)
    lg = lg_ref[...]
    e = jnp.exp(lg - jnp.max(lg, axis=0, keepdims=True))
    aff = e / jnp.sum(e, axis=0, keepdims=True)
    aff_ref[0] = aff
    bits = lax.bitcast_convert_type(aff, jnp.int32)
    lane = _iota((n_exp, t), 1)

    if n_ctx:
        is_ctx = lane < n_ctx
        regions = [(is_ctx, cap_c), (jnp.logical_not(is_ctx), cap_l)]
    else:
        regions = [(None, cap_l)]

    def count(pred, mask):
        p = pred if mask is None else (pred & mask)
        return jnp.sum(jnp.where(p, 1.0, 0.0), axis=1, keepdims=True)

    def thr_body(i, thrs):
        bit = lax.shift_left(jnp.int32(1), AFF_BITS - 1 - i)
        out = []
        for (mask, cap), thr in zip(regions, thrs):
            cand = thr | bit
            out.append(jnp.where(count(bits >= cand, mask) >= cap, cand, thr))
        return tuple(out)

    zero = jnp.zeros((n_exp, 1), jnp.int32)
    thrs = lax.fori_loop(0, AFF_BITS, thr_body, tuple(zero for _ in regions))

    idx_bits = t.bit_length()
    sel = None
    for (mask, cap), thr in zip(regions, thrs):
        gt = bits > thr
        tie = bits == thr
        need = cap - count(gt, mask)

        def j_body(i, j, tie=tie, mask=mask, need=need):
            cand = j | lax.shift_left(jnp.int32(1), idx_bits - 1 - i)
            return jnp.where(count(tie & (lane < cand), mask) <= need, cand, j)

        jmax = lax.fori_loop(0, idx_bits, j_body, zero)
        s = gt | (tie & (lane < jmax))
        if mask is not None:
            s = s & mask
        sel = s if sel is None else (sel | s)

    lt = LANES if t % LANES == 0 else CHUNK
    ut = jnp.where(_iota((lt, lt), 0) < _iota((lt, lt), 1), 1.0, 0.0).astype(BF16)
    self = jnp.where(sel, 1.0, 0.0)
    run = jnp.zeros((n_exp, 1), F32)
    pres = []
    for i in range(t // lt):
        tile = self[:, i * lt:(i + 1) * lt]
        pres.append(_dot(tile.astype(BF16), ut) + run)
        run = run + jnp.sum(tile, axis=1, keepdims=True)
    pre = jnp.concatenate(pres, axis=1)
    if n_ctx:
        slot = jnp.where(is_ctx, pre + cap_l, pre - cap_c)
    else:
        slot = pre
    slot_ref[0] = jnp.where(sel, slot, -1.0)


def _route(x, modsel, g, w_router, *, n_ctx, cap_c, cap_l, tr):
    b, t, d = x.shape
    n_exp = w_router.shape[-1]
    wrt = w_router.T
    g2 = g.reshape(1, d)
    return pl.pallas_call(
        functools.partial(_route_kernel, n_ctx=n_ctx, cap_c=cap_c, cap_l=cap_l, tr=tr),
        grid=(b,),
        in_specs=[pl.BlockSpec((1, t, d), lambda i: (i, 0, 0)),
                  pl.BlockSpec((1, 2, 6, d), lambda i: (i, 0, 0, 0)),
                  pl.BlockSpec((1, d), lambda i: (0, 0)),
                  pl.BlockSpec((n_exp, d), lambda i: (0, 0))],
        out_specs=[pl.BlockSpec((1, t, d), lambda i: (i, 0, 0)),
                   pl.BlockSpec((1, n_exp, t), lambda i: (i, 0, 0)),
                   pl.BlockSpec((1, n_exp, t), lambda i: (i, 0, 0))],
        out_shape=[jax.ShapeDtypeStruct((b, t, d), BF16),
                   jax.ShapeDtypeStruct((b, n_exp, t), F32),
                   jax.ShapeDtypeStruct((b, n_exp, t), F32)],
        scratch_shapes=[pltpu.VMEM((n_exp, t), F32)],
        compiler_params=_cp("parallel"),
        name="moe_route",
    )(x, modsel, g2, wrt)


def _gather_kernel(slot_ref, h_ref, x_ref, *, n_ctx, cap_c, cap_l):
    slot = slot_ref[0, 0]
    t = slot.shape[1]
    sl = slot[:, n_ctx:]
    p = jnp.where(sl == _iota((cap_l, t - n_ctx), 0).astype(F32), 1.0, 0.0).astype(BF16)
    x_ref[0, 0, 0:cap_l, :] = _dot(p, h_ref[0, n_ctx:, :]).astype(BF16)
    if n_ctx:
        sc = slot[:, :n_ctx] - float(cap_l)
        p = jnp.where(sc == _iota((cap_c, n_ctx), 0).astype(F32), 1.0, 0.0).astype(BF16)
        x_ref[0, 0, cap_l:, :] = _dot(p, h_ref[0, :n_ctx, :]).astype(BF16)


def _gather(h, slot_row, *, n_ctx, cap_c, cap_l):
    b, t, d = h.shape
    n_exp = slot_row.shape[1]
    m = cap_l + (cap_c if n_ctx else 0)
    return pl.pallas_call(
        functools.partial(_gather_kernel, n_ctx=n_ctx, cap_c=cap_c, cap_l=cap_l),
        grid=(b, n_exp),
        in_specs=[pl.BlockSpec((1, 1, 1, t), lambda i, e: (i, e, 0, 0)),
                  pl.BlockSpec((1, t, d), lambda i, e: (i, 0, 0))],
        out_specs=pl.BlockSpec((1, 1, m, d), lambda i, e: (e, i, 0, 0)),
        out_shape=jax.ShapeDtypeStruct((n_exp, b, m, d), BF16),
        compiler_params=_cp("parallel", "parallel"),
        name="moe_gather",
    )(slot_row.reshape(b, n_exp, 1, t), h)


def _ffn_kernel(x_ref, wg_ref, wu_ref, wd_ref, y_ref, acc_ref, wgb_ref, wub_ref, wdb_ref, *, tr):
    ft = pl.program_id(1)

    @pl.when(ft == 0)
    def _():
        acc_ref[...] = jnp.zeros(acc_ref.shape, F32)

    wgb_ref[...] = wg_ref[0].astype(BF16)
    wub_ref[...] = wu_ref[0].astype(BF16)
    wdb_ref[...] = wd_ref[0].astype(BF16)

    def body(i, carry):
        rows = pl.ds(pl.multiple_of(i * tr, tr), tr)
        x = x_ref[0, rows, :]
        hid = (_silu(_dot(x, wgb_ref[...])) * _dot(x, wub_ref[...])).astype(BF16)
        acc_ref[rows, :] += _dot(hid, wdb_ref[...])
        return carry

    lax.fori_loop(0, x_ref.shape[1] // tr, body, 0)

    @pl.when(ft == pl.num_programs(1) - 1)
    def _():
        y_ref[0] = acc_ref[...].astype(y_ref.dtype)


def _ffn(xs, w_gate, w_up, w_down, *, tr):
    n_exp, bm, d = xs.shape
    f = w_gate.shape[-1]
    tf = FFN_TILE if f % FFN_TILE == 0 else f
    return pl.pallas_call(
        functools.partial(_ffn_kernel, tr=tr),
        grid=(n_exp, f // tf),
        in_specs=[pl.BlockSpec((1, bm, d), lambda e, j: (e, 0, 0)),
                  pl.BlockSpec((1, d, tf), lambda e, j: (e, 0, j)),
                  pl.BlockSpec((1, d, tf), lambda e, j: (e, 0, j)),
                  pl.BlockSpec((1, tf, d), lambda e, j: (e, j, 0))],
        out_specs=pl.BlockSpec((1, bm, d), lambda e, j: (e, 0, 0)),
        out_shape=jax.ShapeDtypeStruct((n_exp, bm, d), BF16),
        scratch_shapes=[pltpu.VMEM((bm, d), F32),
                        pltpu.VMEM((d, tf), BF16),
                        pltpu.VMEM((d, tf), BF16),
                        pltpu.VMEM((tf, d), BF16)],
        compiler_params=_cp("parallel", "arbitrary"),
        name="moe_ffn",
    )(xs, w_gate, w_up, w_down)


def _combine_kernel(*refs, nc, cap_c, cap_l, final):
    if final:
        x_ref, ys_ref, sc_ref, ac_ref, mod_ref, fg_ref, o_ref, acc_ref = refs
    else:
        x_ref, ys_ref, sc_ref, ac_ref, mod_ref, o_ref, acc_ref = refs
    n_exp = ys_ref.shape[0]
    tm = x_ref.shape[1]

    def run(k0, kk):
        sc = sc_ref[0]
        ac = ac_ref[0]
        lane = (_iota((tm, kk), 1) + k0).astype(F32)
        acc_ref[...] = jnp.zeros(acc_ref.shape, F32)
        for e in range(n_exp):
            pt = jnp.where(sc[:, e:e + 1] == lane, 1.0, 0.0).astype(BF16)
            acc_ref[...] += ac[:, e:e + 1] * _dot(pt, ys_ref[e, 0, k0:k0 + kk, :])
        x2 = x_ref[0] + mod_ref[0, 0][5:6] * acc_ref[...]
        if final:
            x2 = x2 * lax.rsqrt(jnp.mean(x2 * x2, axis=-1, keepdims=True) + EPS) * fg_ref[...]
        o_ref[0] = x2

    if nc:
        t = pl.program_id(1)

        @pl.when(t < nc)
        def _():
            run(cap_l, cap_c)

        @pl.when(t >= nc)
        def _():
            run(0, cap_l)
    else:
        run(0, cap_l)


def _combine(x, ys, slot_col, aff_col, modsel, final_g, *, nc, cap_c, cap_l, tm):
    b, t, d = x.shape
    n_exp, _, m, _ = ys.shape
    seg = (lambda j: jnp.where(j >= nc, 1, 0)) if nc else (lambda j: 1)
    in_specs = [pl.BlockSpec((1, tm, d), lambda i, j: (i, j, 0)),
                pl.BlockSpec((n_exp, 1, m, d), lambda i, j: (0, i, 0, 0)),
                pl.BlockSpec((1, tm, n_exp), lambda i, j: (i, j, 0)),
                pl.BlockSpec((1, tm, n_exp), lambda i, j: (i, j, 0)),
                pl.BlockSpec((1, 1, 6, d), lambda i, j: (i, seg(j), 0, 0))]
    args = [x, ys, slot_col, aff_col, modsel]
    if final_g is not None:
        in_specs.append(pl.BlockSpec((1, d), lambda i, j: (0, 0)))
        args.append(final_g.reshape(1, d))
    return pl.pallas_call(
        functools.partial(_combine_kernel, nc=nc, cap_c=cap_c, cap_l=cap_l,
                          final=final_g is not None),
        grid=(b, t // tm),
        in_specs=in_specs,
        out_specs=pl.BlockSpec((1, tm, d), lambda i, j: (i, j, 0)),
        out_shape=jax.ShapeDtypeStruct((b, t, d), F32),
        scratch_shapes=[pltpu.VMEM((tm, d), F32)],
        compiler_params=_cp("parallel", "parallel"),
        name="moe_combine",
    )(*args)


def _moe(x, modsel, g, w_router, w_gate, w_up, w_down, final_g, *, n_ctx, n_lat, tm):
    b, t, d = x.shape
    n_exp = w_router.shape[-1]
    cap_l = EC_CAPACITY_FACTOR * n_lat // n_exp
    cap_c = EC_CAPACITY_FACTOR * n_ctx // n_exp
    h, slot_row, aff_row = _route(x, modsel, g, w_router, n_ctx=n_ctx, cap_c=cap_c, cap_l=cap_l,
                                  tr=tm)
    xs = _gather(h, slot_row, n_ctx=n_ctx, cap_c=cap_c, cap_l=cap_l)
    m = xs.shape[2]
    ys = _ffn(xs.reshape(n_exp, b * m, d), w_gate, w_up, w_down, tr=min(MAX_BLOCK_ROWS, m))
    return _combine(x, ys.reshape(n_exp, b, m, d), jnp.swapaxes(slot_row, 1, 2),
                    jnp.swapaxes(aff_row, 1, 2), modsel, final_g,
                    nc=n_ctx // tm, cap_c=cap_c, cap_l=cap_l, tm=tm)


def _raster_to_column(t):
    b, l, d = t.shape
    rows = l // GRID_W
    return t.reshape(b, rows, GRID_W, d).transpose(0, 2, 1, 3).reshape(b, l, d)


def _column_to_raster(t):
    b, l, d = t.shape
    rows = l // GRID_W
    return t.reshape(b, GRID_W, rows, d).transpose(0, 2, 1, 3).reshape(b, l, d)


def kernel(x, c, ctx, c_ctx, w_mod, b_mod, norm_g, hg_w_in, hg_lb, hg_onorm, hg_w_out, gd_w_in,
           gd_conv, gd_a_log, gd_dt_bias, gd_onorm, gd_w_out, moe_router, moe_w_gate, moe_w_up,
           moe_w_down, final_g):
    bsz, n_lat, d = x.shape
    n_ctx = ctx.shape[1]
    depth = w_mod.shape[0]
    assert depth == 2, "layer 0 = HGRN2, layer 1 = gated DeltaNet"
    tm = math.gcd(math.gcd(n_ctx, n_lat), MAX_BLOCK_ROWS)
    assert tm % CHUNK == 0
    nc = n_ctx // tm
    o_dt = BF16

    rows = -(-(bsz + 1) // SUBLANES) * SUBLANES
    cond = jnp.zeros((rows, d), F32).at[:bsz].set(c).at[bsz].set(c_ctx)
    mod = _adaln(cond, w_mod, b_mod).reshape(depth, rows, 6, d)

    def modsel(i):
        ctx_mod = jnp.broadcast_to(mod[i, bsz][None], (bsz, 6, d))
        return jnp.stack([ctx_mod, mod[i, :bsz]], axis=1)

    x_all = jnp.concatenate([ctx, x], axis=1)

    ms = modsel(0)
    q, lf, v, gate = _hg_inproj(x_all, ms, norm_g[0, 0], hg_lb, hg_w_in[0], layer=0, nc=nc, tm=tm)
    o_f = _gla_scan(q, lf, v, rev=False, nc=nc, tm=tm, out_dtype=o_dt)
    o_b = _gla_scan(q, lf, v, rev=True, nc=nc, tm=tm, out_dtype=o_dt)
    x_all = _outproj(o_f, o_b, gate, hg_onorm[0], hg_w_out[0], x_all, ms, nc=nc, tm=tm,
                     skip_ctx=False)
    x_all = _moe(x_all, ms, norm_g[0, 1], moe_router[0], moe_w_gate[0], moe_w_up[0],
                 moe_w_down[0], None, n_ctx=n_ctx, n_lat=n_lat, tm=tm)

    ms = modsel(1)
    x_cm = jnp.concatenate([x_all[:, :n_ctx], _raster_to_column(x_all[:, n_ctx:])], axis=1)
    q, k, v, gate, ab = _gd_inproj(x_cm, ms, norm_g[1, 0], gd_w_in[0], gd_conv[0], gd_a_log[0],
                                   gd_dt_bias[0], nc=nc, tm=tm)
    o_f = _gdn_scan(q, k, v, ab, rev=False, nc=nc, tm=tm, out_dtype=o_dt)
    o_b = _gdn_scan(q, k, v, ab, rev=True, nc=nc, tm=tm, out_dtype=o_dt)
    x_lat = _outproj(o_f, o_b, gate, gd_onorm[0], gd_w_out[0], x_cm, ms, nc=nc, tm=tm,
                     skip_ctx=True)
    x_lat = _column_to_raster(x_lat)
    return _moe(x_lat, ms, norm_g[1, 1], moe_router[1], moe_w_gate[1], moe_w_up[1],
                moe_w_down[1], final_g, n_ctx=0, n_lat=n_lat, tm=tm)
```

```python
import functools
import math

import jax
import jax.numpy as jnp
from jax import lax
from jax.experimental import pallas as pl
from jax.experimental.pallas import tpu as pltpu

F32 = jnp.float32
BF16 = jnp.bfloat16
HIGHEST = lax.Precision.HIGHEST

EPS = 1e-6
LANES = 128
SUBLANES = 8
V7X_VMEM_BYTES = 64 * 1024 * 1024
VMEM_LIMIT = V7X_VMEM_BYTES * 7 // 8

HEAD_DK = 128
GD_DV = 256
GD_CONV_W = 5
GRID_W = 64
EC_CAPACITY_FACTOR = 2
CHUNK = 64
SUB = 16
EXP_CLAMP = 60.0
MAX_BLOCK_ROWS = 256
FFN_TILE = 256
FFN_MAX_ROWS = 1152
AFF_BITS = 31


def _cp(*sem):
    return pltpu.CompilerParams(dimension_semantics=sem, vmem_limit_bytes=VMEM_LIMIT)


def _dot(a, b):
    return jnp.dot(a, b, preferred_element_type=F32)


def _dot_nt(a, b):
    return lax.dot_general(a, b, (((1,), (1,)), ((), ())), preferred_element_type=F32)


def _dot_tn(a, b):
    return lax.dot_general(a, b, (((0,), (0,)), ((), ())), preferred_element_type=F32)


def _dot_hi(a, b):
    return jnp.dot(a, b, preferred_element_type=F32, precision=HIGHEST)


def _dot_nt_hi(a, b):
    return lax.dot_general(a, b, (((1,), (1,)), ((), ())), preferred_element_type=F32,
                           precision=HIGHEST)


def _silu(x):
    return x * jax.nn.sigmoid(x)


def _norm_mod(x, g, shift, scale):
    y = x * lax.rsqrt(jnp.mean(x * x, axis=-1, keepdims=True) + EPS)
    return y * g * (1.0 + scale) + shift


def _iota(shape, dim):
    return lax.broadcasted_iota(jnp.int32, shape, dim)


def _split3(x):
    hi = x.astype(BF16)
    r1 = x - hi.astype(F32)
    mid = r1.astype(BF16)
    lo = (r1 - mid.astype(F32)).astype(BF16)
    return [hi, mid, lo]


def _adaln_kernel(c_ref, w_ref, b_ref, o_ref):
    o_ref[0] = _dot_hi(_silu(c_ref[...]), w_ref[0]) + b_ref[0]


def _adaln(cond, w_mod, b_mod):
    depth, d, n = w_mod.shape
    r = cond.shape[0]
    tn = n // 12
    return pl.pallas_call(
        _adaln_kernel,
        grid=(depth, n // tn),
        in_specs=[pl.BlockSpec((r, d), lambda i, j: (0, 0)),
                  pl.BlockSpec((1, d, tn), lambda i, j: (i, 0, j)),
                  pl.BlockSpec((1, 1, tn), lambda i, j: (i, 0, j))],
        out_specs=pl.BlockSpec((1, r, tn), lambda i, j: (i, 0, j)),
        out_shape=jax.ShapeDtypeStruct((depth, r, n), F32),
        compiler_params=_cp("parallel", "parallel"),
        name="adaln",
    )(cond, w_mod, b_mod.reshape(depth, 1, n))


def _hg_inproj_kernel(x_ref, mod_ref, g_ref, lb_ref, wq_ref, wf_ref, wi_ref, wg_ref,
                      q_ref, lf_ref, v_ref, gate_ref, *, layer, n_heads):
    m = mod_ref[0, 0]
    hb = _norm_mod(x_ref[0], g_ref[...], m[0:1], m[1:2]).astype(BF16)
    fdim = wq_ref.shape[1]
    dk = fdim // n_heads
    dv = wi_ref.shape[1] // n_heads

    q = _silu(_dot(hb, wq_ref[...]))
    for h in range(n_heads):
        q_ref[0, h] = q[:, h * dk:(h + 1) * dk].astype(BF16)

    lbp = lb_ref[...]
    e = jnp.exp(lbp - jnp.max(lbp, axis=0))
    lb = jnp.sum(e[:layer + 1], axis=0) / jnp.sum(e, axis=0)

    zf = _dot(hb, wf_ref[...])
    for d in range(2):
        lbd = lb[d:d + 1]
        f = lbd + (1.0 - lbd) * jax.nn.sigmoid(zf[:, d * fdim:(d + 1) * fdim])
        lf = jnp.log(f)
        for h in range(n_heads):
            lf_ref[d, 0, h] = lf[:, h * dk:(h + 1) * dk]

    v = _dot(hb, wi_ref[...])
    for h in range(n_heads):
        v_ref[0, h] = v[:, h * dv:(h + 1) * dv].astype(BF16)

    gate_ref[0] = _silu(_dot(hb, wg_ref[...])).astype(BF16)


def _hg_inproj(x_all, modsel, g, hg_lb, w_in, *, layer, nc, tm):
    b, t, d = x_all.shape
    fdim = hg_lb.shape[-1]
    n_heads = fdim // HEAD_DK
    dv = d // n_heads
    wq = w_in[:, :fdim].astype(BF16)
    wf = w_in[:, fdim:3 * fdim].astype(BF16)
    wi = w_in[:, 3 * fdim:3 * fdim + d].astype(BF16)
    wg = w_in[:, 3 * fdim + d:].astype(BF16)
    full = lambda a: pl.BlockSpec(a.shape, lambda i, j: (0,) * a.ndim)
    g2 = g.reshape(1, d)
    return pl.pallas_call(
        functools.partial(_hg_inproj_kernel, layer=layer, n_heads=n_heads),
        grid=(b, t // tm),
        in_specs=[pl.BlockSpec((1, tm, d), lambda i, j: (i, j, 0)),
                  pl.BlockSpec((1, 1, 6, d), lambda i, j: (i, jnp.where(j >= nc, 1, 0), 0, 0)),
                  full(g2), full(hg_lb), full(wq), full(wf), full(wi), full(wg)],
        out_specs=[pl.BlockSpec((1, n_heads, tm, HEAD_DK), lambda i, j: (i, 0, j, 0)),
                   pl.BlockSpec((2, 1, n_heads, tm, HEAD_DK), lambda i, j: (0, i, 0, j, 0)),
                   pl.BlockSpec((1, n_heads, tm, dv), lambda i, j: (i, 0, j, 0)),
                   pl.BlockSpec((1, tm, d), lambda i, j: (i, j, 0))],
        out_shape=[jax.ShapeDtypeStruct((b, n_heads, t, HEAD_DK), BF16),
                   jax.ShapeDtypeStruct((2, b, n_heads, t, HEAD_DK), F32),
                   jax.ShapeDtypeStruct((b, n_heads, t, dv), BF16),
                   jax.ShapeDtypeStruct((b, t, d), BF16)],
        compiler_params=_cp("parallel", "parallel"),
        name="hg_inproj",
    )(x_all, modsel, g2, hg_lb, wq, wf, wi, wg)


def _scan_block_index(step, nc, nl, rev):
    if not rev:
        return step
    return jnp.where(step < nc, nc - 1 - step, nc + nl - 1 - (step - nc))


def _chunk_masks(rev):
    r = _iota((CHUNK, CHUNK), 0)
    c = _iota((CHUNK, CHUNK), 1)
    incl = (c >= r) if rev else (c <= r)
    strict = (c > r) if rev else (c < r)
    return incl, strict


def _gla_scan_kernel(q_ref, lf_ref, v_ref, o_ref, st_ref, oin_ref, qc_ref, kh_ref,
                     *, rev, n_heads, nck, tm, group):
    @pl.when(pl.program_id(1) == 0)
    def _():
        st_ref[...] = jnp.zeros(st_ref.shape, F32)

    incl, _ = _chunk_masks(rev)
    dk = q_ref.shape[-1]
    r = _iota((tm, tm), 0)
    c = _iota((tm, tm), 1)
    tri = ((r // CHUNK) == (c // CHUNK)) & ((c >= r) if rev else (c <= r))
    lmb = jnp.where(tri, 1.0, 0.0).astype(BF16)

    dec = {}
    for h0 in range(0, n_heads, group):
        hs = range(h0, h0 + group)
        b = {}
        for h in hs:
            g3 = _dot(lmb, jnp.concatenate(_split3(lf_ref[0, 0, h]), axis=1))
            b[h] = g3[:, :dk] + g3[:, dk:2 * dk] + g3[:, 2 * dk:]
        for h in hs:
            g = lf_ref[0, 0, h]
            q = q_ref[0, h].astype(F32)
            k = 1.0 - jnp.exp(g)
            be = b[h] - g
            bls = [b[h][ci * CHUNK:ci * CHUNK + 1] if rev
                   else b[h][(ci + 1) * CHUNK - 1:(ci + 1) * CHUNK] for ci in range(nck)]
            blc = jnp.concatenate([jnp.broadcast_to(x, (CHUNK, dk)) for x in bls], axis=0)
            qc_ref[h] = (q * jnp.exp(b[h])).astype(BF16)
            kh_ref[h] = (k * jnp.exp(blc - b[h])).astype(BF16)
            dec[h] = [jnp.exp(x) for x in bls]
            for ci in range(nck):
                c0 = ci * CHUNK
                bc = b[h][c0:c0 + CHUNK]
                kc = k[c0:c0 + CHUNK]
                atts = []
                for i in range(CHUNK // SUB):
                    r0 = c0 + i * SUB
                    bref = be[r0 + SUB - 1:r0 + SUB] if rev else be[r0:r0 + 1]
                    qi = (q[r0:r0 + SUB] * jnp.exp(b[h][r0:r0 + SUB] - bref)).astype(BF16)
                    kt = (kc * jnp.exp(jnp.minimum(bref - bc, EXP_CLAMP))).astype(BF16)
                    atts.append(_dot_nt(qi, kt))
                att = jnp.where(incl, jnp.concatenate(atts, axis=0), 0.0).astype(BF16)
                oin_ref[h, c0:c0 + CHUNK, :] = _dot(att, v_ref[0, h, c0:c0 + CHUNK, :])

    for step in range(nck):
        ci = nck - 1 - step if rev else step
        rows = pl.ds(ci * CHUNK, CHUNK)
        for h in range(n_heads):
            st = st_ref[h]
            o = oin_ref[h, rows, :] + _dot_nt(qc_ref[h, rows, :], st.astype(BF16))
            o_ref[0, h, rows, :] = o.astype(o_ref.dtype)
            st_ref[h] = st * dec[h][ci] + _dot_tn(v_ref[0, h, rows, :], kh_ref[h, rows, :])


def _gla_scan(q, lf, v, *, rev, nc, tm, out_dtype):
    b, n_heads, t, dk = q.shape
    dv = v.shape[-1]
    nblk = t // tm
    nl = nblk - nc
    nck = tm // CHUNK
    d = 1 if rev else 0
    blk = lambda i, s: _scan_block_index(s, nc, nl, rev)
    return pl.pallas_call(
        functools.partial(_gla_scan_kernel, rev=rev, n_heads=n_heads, nck=nck, tm=tm,
                          group=math.gcd(n_heads, 4)),
        grid=(b, nblk),
        in_specs=[pl.BlockSpec((1, n_heads, tm, dk), lambda i, s: (i, 0, blk(i, s), 0)),
                  pl.BlockSpec((1, 1, n_heads, tm, dk), lambda i, s: (d, i, 0, blk(i, s), 0)),
                  pl.BlockSpec((1, n_heads, tm, dv), lambda i, s: (i, 0, blk(i, s), 0))],
        out_specs=pl.BlockSpec((1, n_heads, tm, dv), lambda i, s: (i, 0, blk(i, s), 0)),
        out_shape=jax.ShapeDtypeStruct((b, n_heads, t, dv), out_dtype),
        scratch_shapes=[pltpu.VMEM((n_heads, dv, dk), F32),
                        pltpu.VMEM((n_heads, tm, dv), F32),
                        pltpu.VMEM((n_heads, tm, dk), BF16),
                        pltpu.VMEM((n_heads, tm, dk), BF16)],
        compiler_params=_cp("parallel", "arbitrary"),
        name="gla_scan_bwd" if rev else "gla_scan_fwd",
    )(q, lf, v)


def _outproj_kernel(of_ref, ob_ref, gate_ref, gain_ref, w_ref, x_ref, mod_ref, o_ref, y_ref,
                    *, n_heads):
    dv = of_ref.shape[-1]
    for h in range(n_heads):
        o = of_ref[0, h].astype(F32) + ob_ref[0, h].astype(F32)
        cs = slice(h * dv, (h + 1) * dv)
        o = o * lax.rsqrt(jnp.mean(o * o, axis=-1, keepdims=True) + EPS) * gain_ref[:, cs]
        y_ref[:, cs] = (o * gate_ref[0, :, cs].astype(F32)).astype(BF16)
    m = mod_ref[0, 0]
    o_ref[0] = x_ref[0] + m[2:3] * _dot(y_ref[...], w_ref[...])


def _outproj(o_f, o_b, gate, gain, w_out, x_all, modsel, *, nc, tm, skip_ctx):
    b, n_heads, t, dv = o_f.shape
    d = x_all.shape[-1]
    hv = n_heads * dv
    off = nc if skip_ctx else 0
    nblk = t // tm - off
    seg = (lambda j: 1) if skip_ctx else (lambda j: jnp.where(j >= nc, 1, 0))
    w = w_out.astype(BF16)
    gain2 = gain.reshape(1, hv)
    return pl.pallas_call(
        functools.partial(_outproj_kernel, n_heads=n_heads),
        grid=(b, nblk),
        in_specs=[pl.BlockSpec((1, n_heads, tm, dv), lambda i, j: (i, 0, j + off, 0)),
                  pl.BlockSpec((1, n_heads, tm, dv), lambda i, j: (i, 0, j + off, 0)),
                  pl.BlockSpec((1, tm, hv), lambda i, j: (i, j + off, 0)),
                  pl.BlockSpec((1, hv), lambda i, j: (0, 0)),
                  pl.BlockSpec((hv, d), lambda i, j: (0, 0)),
                  pl.BlockSpec((1, tm, d), lambda i, j: (i, j + off, 0)),
                  pl.BlockSpec((1, 1, 6, d), lambda i, j: (i, seg(j), 0, 0))],
        out_specs=pl.BlockSpec((1, tm, d), lambda i, j: (i, j, 0)),
        out_shape=jax.ShapeDtypeStruct((b, nblk * tm, d), F32),
        scratch_shapes=[pltpu.VMEM((tm, hv), BF16)],
        compiler_params=_cp("parallel", "parallel"),
        name="outproj",
    )(o_f, o_b, gate, gain2, w, x_all, modsel)


def _gd_inproj_kernel(xp_ref, x_ref, xn_ref, mod_ref, g_ref, wqkv_ref, wg_ref, wab_ref, cw_ref,
                      alog_ref, dtb_ref, q_ref, k_ref, v_ref, gate_ref, ab_ref, zs_ref,
                      *, nc, nblk, n_heads, tm, cw):
    t = pl.program_id(1)
    first = (t == 0) | (t == nc)
    last = (t == nc - 1) | (t == nblk - 1)
    m = mod_ref[0, 0]
    halo = SUBLANES
    xe = jnp.concatenate([xp_ref[0], x_ref[0], xn_ref[0]], axis=0)
    he = _norm_mod(xe, g_ref[...], m[0:1], m[1:2]).astype(BF16)
    hc = he[halo:halo + tm]
    row = _iota((tm + 2 * halo, 1), 0)
    valid = ((row >= halo) | jnp.logical_not(first)) & ((row < tm + halo) | jnp.logical_not(last))

    gate_ref[0] = _silu(_dot(hc, wg_ref[...])).astype(BF16)

    zab = _dot(hc, wab_ref[...])
    lane = _iota(zab.shape, 1)
    loga = -jnp.exp(alog_ref[...]) * jax.nn.softplus(zab + dtb_ref[...])
    ab_ref[0] = jnp.where(lane < 2 * n_heads, loga, jax.nn.sigmoid(zab))

    qk = n_heads * HEAD_DK
    nchan = wqkv_ref.shape[1]
    pad = GD_CONV_W // 2
    for cc in range(nchan // cw):
        c0 = cc * cw
        z = _dot(he, wqkv_ref[:, c0:c0 + cw])
        zs_ref[...] = jnp.where(valid, z, 0.0)
        acc = cw_ref[0:1, c0:c0 + cw] * zs_ref[halo - pad:halo - pad + tm, :]
        for j in range(1, GD_CONV_W):
            acc = acc + cw_ref[j:j + 1, c0:c0 + cw] * zs_ref[halo - pad + j:halo - pad + j + tm, :]
        u = _silu(acc)
        if c0 < 2 * qk:
            dst, base, scale = (q_ref, c0, HEAD_DK ** -0.5) if c0 < qk else (k_ref, c0 - qk, 1.0)
            for j in range(cw // HEAD_DK):
                tt = u[:, j * HEAD_DK:(j + 1) * HEAD_DK]
                n = tt * lax.rsqrt(jnp.sum(tt * tt, axis=-1, keepdims=True) + EPS)
                dst[0, base // HEAD_DK + j] = (n * scale).astype(BF16)
        else:
            base = c0 - 2 * qk
            for j in range(cw // GD_DV):
                v_ref[0, base // GD_DV + j] = u[:, j * GD_DV:(j + 1) * GD_DV].astype(BF16)


def _gd_inproj(x_all, modsel, g, w_in, conv_w, a_log, dt_bias, *, nc, tm):
    b, t, d = x_all.shape
    n_heads = a_log.shape[-1]
    qk = n_heads * HEAD_DK
    vd = n_heads * GD_DV
    nchan = 2 * qk + vd
    nblk = t // tm
    cw = min(512, qk)
    wqkv = w_in[:, :nchan].astype(BF16)
    wg = w_in[:, nchan:nchan + vd].astype(BF16)
    wab = jnp.pad(w_in[:, nchan + vd:], ((0, 0), (0, LANES - 4 * n_heads))).astype(BF16)
    alog = jnp.pad(a_log.reshape(1, 2 * n_heads), ((0, 0), (0, LANES - 2 * n_heads)))
    dtb = jnp.pad(dt_bias.reshape(1, 2 * n_heads), ((0, 0), (0, LANES - 2 * n_heads)))
    g2 = g.reshape(1, d)
    full = lambda a: pl.BlockSpec(a.shape, lambda i, j: (0,) * a.ndim)
    spb = tm // SUBLANES
    nslab = t // SUBLANES
    return pl.pallas_call(
        functools.partial(_gd_inproj_kernel, nc=nc, nblk=nblk, n_heads=n_heads, tm=tm, cw=cw),
        grid=(b, nblk),
        in_specs=[pl.BlockSpec((1, SUBLANES, d), lambda i, j: (i, jnp.maximum(j * spb - 1, 0), 0)),
                  pl.BlockSpec((1, tm, d), lambda i, j: (i, j, 0)),
                  pl.BlockSpec((1, SUBLANES, d),
                               lambda i, j: (i, jnp.minimum((j + 1) * spb, nslab - 1), 0)),
                  pl.BlockSpec((1, 1, 6, d), lambda i, j: (i, jnp.where(j >= nc, 1, 0), 0, 0)),
                  full(g2), full(wqkv), full(wg), full(wab), full(conv_w), full(alog), full(dtb)],
        out_specs=[pl.BlockSpec((1, n_heads, tm, HEAD_DK), lambda i, j: (i, 0, j, 0)),
                   pl.BlockSpec((1, n_heads, tm, HEAD_DK), lambda i, j: (i, 0, j, 0)),
                   pl.BlockSpec((1, n_heads, tm, GD_DV), lambda i, j: (i, 0, j, 0)),
                   pl.BlockSpec((1, tm, vd), lambda i, j: (i, j, 0)),
                   pl.BlockSpec((1, tm, LANES), lambda i, j: (i, j, 0))],
        out_shape=[jax.ShapeDtypeStruct((b, n_heads, t, HEAD_DK), BF16),
                   jax.ShapeDtypeStruct((b, n_heads, t, HEAD_DK), BF16),
                   jax.ShapeDtypeStruct((b, n_heads, t, GD_DV), BF16),
                   jax.ShapeDtypeStruct((b, t, vd), BF16),
                   jax.ShapeDtypeStruct((b, t, LANES), F32)],
        scratch_shapes=[pltpu.VMEM((tm + 2 * SUBLANES, cw), F32)],
        compiler_params=_cp("parallel", "parallel"),
        name="gd_inproj",
    )(x_all, x_all, x_all, modsel, g2, wqkv, wg, wab, conv_w, alog, dtb)


def _neumann_inverse(a, eye):
    n = a.shape[0]
    t = eye - a
    ab = a.astype(BF16)
    p = _dot(ab, ab)
    lvl = 2
    while lvl < CHUNK:
        pb = p.astype(BF16)
        tb = t.astype(BF16)
        if 2 * lvl >= CHUNK:
            t = t + _dot(pb, tb)
        else:
            out = _dot(pb, jnp.concatenate([pb, tb], axis=1))
            p = out[:, :n]
            t = t + out[:, n:]
        lvl *= 2
    return t


def _gdn_scan_kernel(q_ref, k_ref, v_ref, ab_ref, o_ref, s_ref, u_ref, l1_ref, l2_ref,
                     *, rev, d, n_heads, nck, tm):
    @pl.when(pl.program_id(1) == 0)
    def _():
        s_ref[...] = jnp.zeros(s_ref.shape, F32)

    r = _iota((tm, tm), 0)
    c = _iota((tm, tm), 1)
    same = (r // CHUNK) == (c // CHUNK)
    incl = same & ((c >= r) if rev else (c <= r))
    strict = same & ((c > r) if rev else (c < r))
    eye = jnp.where(r == c, 1.0, 0.0)
    dv = v_ref.shape[-1]

    ab = ab_ref[0]
    g3 = _dot(jnp.where(incl, 1.0, 0.0).astype(BF16), jnp.concatenate(_split3(ab), axis=1))
    gcol = g3[:, :LANES] + g3[:, LANES:2 * LANES] + g3[:, 2 * LANES:]
    eye_l = jnp.where(_iota((LANES, LANES), 0) == _iota((LANES, LANES), 1), 1.0, 0.0).astype(BF16)
    gr3 = _dot_nt(eye_l, jnp.concatenate(_split3(gcol), axis=0))
    grow = gr3[:, :tm] + gr3[:, tm:2 * tm] + gr3[:, 2 * tm:]

    incl_c, _ = _chunk_masks(rev)
    egl = {}
    for h in range(n_heads):
        ca = d * n_heads + h
        gc = gcol[:, ca:ca + 1]
        bc = ab[:, 2 * n_heads + ca:2 * n_heads + ca + 1]
        gam = jnp.where(incl, jnp.exp(jnp.minimum(gc - grow[ca:ca + 1, :], 0.0)), 0.0)
        q = q_ref[0, h]
        k = k_ref[0, h]
        a = jnp.where(strict, bc * _dot_nt(k, k) * gam, 0.0)
        tinv = _neumann_inverse(a, eye)
        kf = k.astype(F32)
        eg = jnp.exp(gc)
        rhs = jnp.concatenate([(bc * v_ref[0, h].astype(F32)).astype(BF16),
                               (bc * eg * kf).astype(BF16)], axis=1)
        uw = _dot(tinv.astype(BF16), rhs)
        u_ref[h] = uw[:, :dv]
        wb = uw[:, dv:].astype(BF16)
        qg = (q.astype(F32) * eg).astype(BF16)
        gls = [gc[ci * CHUNK:ci * CHUNK + 1] if rev
               else gc[(ci + 1) * CHUNK - 1:(ci + 1) * CHUNK] for ci in range(nck)]
        glc = jnp.concatenate([jnp.broadcast_to(g, (CHUNK, 1)) for g in gls], axis=0)
        kd = kf * jnp.exp(glc - gc)
        egl[h] = [jnp.exp(g) for g in gls]
        for ci in range(nck):
            c0 = ci * CHUNK
            l1_ref[h, ci, :CHUNK] = wb[c0:c0 + CHUNK]
            l1_ref[h, ci, CHUNK:] = qg[c0:c0 + CHUNK]
            gam_c = jnp.where(incl_c, jnp.exp(jnp.minimum(
                gc[c0:c0 + CHUNK] - grow[ca:ca + 1, c0:c0 + CHUNK], 0.0)), 0.0)
            qk_c = _dot_nt(q[c0:c0 + CHUNK], k[c0:c0 + CHUNK])
            l2_ref[h, ci, :CHUNK] = (qk_c * gam_c).astype(BF16)
            l2_ref[h, ci, CHUNK:] = kd[c0:c0 + CHUNK].T.astype(BF16)

    for step in range(nck):
        ci = nck - 1 - step if rev else step
        rows = pl.ds(ci * CHUNK, CHUNK)
        for h in range(n_heads):
            s = s_ref[h]
            r1 = _dot(l1_ref[h, ci], s.astype(BF16))
            weff = (u_ref[h, rows, :] - r1[:CHUNK]).astype(BF16)
            r2 = _dot(l2_ref[h, ci], weff)
            o_ref[0, h, rows, :] = (r2[:CHUNK] + r1[CHUNK:]).astype(o_ref.dtype)
            s_ref[h] = s * egl[h][ci] + r2[CHUNK:]


def _gdn_scan(q, k, v, ab, *, rev, nc, tm, out_dtype):
    b, n_heads, t, dk = q.shape
    dv = v.shape[-1]
    nblk = t // tm
    nl = nblk - nc
    nck = tm // CHUNK
    d = 1 if rev else 0
    blk = lambda s: _scan_block_index(s, nc, nl, rev)
    return pl.pallas_call(
        functools.partial(_gdn_scan_kernel, rev=rev, d=d, n_heads=n_heads, nck=nck, tm=tm),
        grid=(b, nblk),
        in_specs=[pl.BlockSpec((1, n_heads, tm, dk), lambda i, s: (i, 0, blk(s), 0)),
                  pl.BlockSpec((1, n_heads, tm, dk), lambda i, s: (i, 0, blk(s), 0)),
                  pl.BlockSpec((1, n_heads, tm, dv), lambda i, s: (i, 0, blk(s), 0)),
                  pl.BlockSpec((1, tm, LANES), lambda i, s: (i, blk(s), 0))],
        out_specs=pl.BlockSpec((1, n_heads, tm, dv), lambda i, s: (i, 0, blk(s), 0)),
        out_shape=jax.ShapeDtypeStruct((b, n_heads, t, dv), out_dtype),
        scratch_shapes=[pltpu.VMEM((n_heads, dk, dv), F32),
                        pltpu.VMEM((n_heads, tm, dv), F32),
                        pltpu.VMEM((n_heads, nck, 2 * CHUNK, dk), BF16),
                        pltpu.VMEM((n_heads, nck, CHUNK + dk, CHUNK), BF16)],
        compiler_params=_cp("parallel", "arbitrary"),
        name="gdn_scan_bwd" if rev else "gdn_scan_fwd",
    )(q, k, v, ab)


def _route_kernel(x_ref, mod_ref, g_ref, wrt_ref, h_ref, slot_ref, aff_ref, lg_ref,
                  *, n_ctx, cap_c, cap_l, tr):
    n_exp, t = lg_ref.shape
    for rt in range(t // tr):
        m = mod_ref[0, 0 if rt * tr < n_ctx else 1]
        rows = slice(rt * tr, (rt + 1) * tr)
        h = _norm_mod(x_ref[0, rows, :], g_ref[...], m[3:4], m[4:5])
        h_ref[0, rows, :] = h.astype(BF16)
        lg_ref[:, rows] = _dot_nt_hi(wrt_ref[...], h)
    lg = lg_ref[...]
    e = jnp.exp(lg - jnp.max(lg, axis=0, keepdims=True))
    aff = e / jnp.sum(e, axis=0, keepdims=True)
    aff_ref[0] = aff
    bits = lax.bitcast_convert_type(aff, jnp.int32)
    lane = _iota((n_exp, t), 1)

    if n_ctx:
        is_ctx = lane < n_ctx
        regions = [(is_ctx, cap_c), (jnp.logical_not(is_ctx), cap_l)]
    else:
        regions = [(None, cap_l)]

    def count(pred, mask):
        p = pred if mask is None else (pred & mask)
        return jnp.sum(jnp.where(p, 1.0, 0.0), axis=1, keepdims=True)

    def thr_body(i, thrs):
        bit = lax.shift_left(jnp.int32(1), AFF_BITS - 1 - i)
        out = []
        for (mask, cap), thr in zip(regions, thrs):
            cand = thr | bit
            out.append(jnp.where(count(bits >= cand, mask) >= cap, cand, thr))
        return tuple(out)

    zero = jnp.zeros((n_exp, 1), jnp.int32)
    thrs = lax.fori_loop(0, AFF_BITS, thr_body, tuple(zero for _ in regions))

    idx_bits = t.bit_length()
    sel = None
    for (mask, cap), thr in zip(regions, thrs):
        gt = bits > thr
        tie = bits == thr
        need = cap - count(gt, mask)

        def j_body(i, j, tie=tie, mask=mask, need=need):
            cand = j | lax.shift_left(jnp.int32(1), idx_bits - 1 - i)
            return jnp.where(count(tie & (lane < cand), mask) <= need, cand, j)

        jmax = lax.fori_loop(0, idx_bits, j_body, zero)
        s = gt | (tie & (lane < jmax))
        if mask is not None:
            s = s & mask
        sel = s if sel is None else (sel | s)

    lt = LANES if t % LANES == 0 else CHUNK
    ut = jnp.where(_iota((lt, lt), 0) < _iota((lt, lt), 1), 1.0, 0.0).astype(BF16)
    self = jnp.where(sel, 1.0, 0.0)
    run = jnp.zeros((n_exp, 1), F32)
    pres = []
    for i in range(t // lt):
        tile = self[:, i * lt:(i + 1) * lt]
        pres.append(_dot(tile.astype(BF16), ut) + run)
        run = run + jnp.sum(tile, axis=1, keepdims=True)
    pre = jnp.concatenate(pres, axis=1)
    if n_ctx:
        slot = jnp.where(is_ctx, pre + cap_l, pre - cap_c)
    else:
        slot = pre
    slot_ref[0] = jnp.where(sel, slot, -1.0)


def _route(x, modsel, g, w_router, *, n_ctx, cap_c, cap_l, tr):
    b, t, d = x.shape
    n_exp = w_router.shape[-1]
    wrt = w_router.T
    g2 = g.reshape(1, d)
    return pl.pallas_call(
        functools.partial(_route_kernel, n_ctx=n_ctx, cap_c=cap_c, cap_l=cap_l, tr=tr),
        grid=(b,),
        in_specs=[pl.BlockSpec((1, t, d), lambda i: (i, 0, 0)),
                  pl.BlockSpec((1, 2, 6, d), lambda i: (i, 0, 0, 0)),
                  pl.BlockSpec((1, d), lambda i: (0, 0)),
                  pl.BlockSpec((n_exp, d), lambda i: (0, 0))],
        out_specs=[pl.BlockSpec((1, t, d), lambda i: (i, 0, 0)),
                   pl.BlockSpec((1, n_exp, t), lambda i: (i, 0, 0)),
                   pl.BlockSpec((1, n_exp, t), lambda i: (i, 0, 0))],
        out_shape=[jax.ShapeDtypeStruct((b, t, d), BF16),
                   jax.ShapeDtypeStruct((b, n_exp, t), F32),
                   jax.ShapeDtypeStruct((b, n_exp, t), F32)],
        scratch_shapes=[pltpu.VMEM((n_exp, t), F32)],
        compiler_params=_cp("parallel"),
        name="moe_route",
    )(x, modsel, g2, wrt)


def _gather_kernel(slot_ref, h_ref, x_ref, *, n_ctx, cap_c, cap_l):
    slot = slot_ref[0, 0]
    t = slot.shape[1]
    sl = slot[:, n_ctx:]
    p = jnp.where(sl == _iota((cap_l, t - n_ctx), 0).astype(F32), 1.0, 0.0).astype(BF16)
    x_ref[0, 0, 0:cap_l, :] = _dot(p, h_ref[0, n_ctx:, :]).astype(BF16)
    if n_ctx:
        sc = slot[:, :n_ctx] - float(cap_l)
        p = jnp.where(sc == _iota((cap_c, n_ctx), 0).astype(F32), 1.0, 0.0).astype(BF16)
        x_ref[0, 0, cap_l:, :] = _dot(p, h_ref[0, :n_ctx, :]).astype(BF16)


def _gather(h, slot_row, *, n_ctx, cap_c, cap_l):
    b, t, d = h.shape
    n_exp = slot_row.shape[1]
    m = cap_l + (cap_c if n_ctx else 0)
    return pl.pallas_call(
        functools.partial(_gather_kernel, n_ctx=n_ctx, cap_c=cap_c, cap_l=cap_l),
        grid=(b, n_exp),
        in_specs=[pl.BlockSpec((1, 1, 1, t), lambda i, e: (i, e, 0, 0)),
                  pl.BlockSpec((1, t, d), lambda i, e: (i, 0, 0))],
        out_specs=pl.BlockSpec((1, 1, m, d), lambda i, e: (e, i, 0, 0)),
        out_shape=jax.ShapeDtypeStruct((n_exp, b, m, d), BF16),
        compiler_params=_cp("parallel", "parallel"),
        name="moe_gather",
    )(slot_row.reshape(b, n_exp, 1, t), h)


def _ffn_kernel(x_ref, wg_ref, wu_ref, wd_ref, y_ref, acc_ref, wgb_ref, wub_ref, wdb_ref, *, tr):
    ft = pl.program_id(1)

    @pl.when(ft == 0)
    def _():
        acc_ref[...] = jnp.zeros(acc_ref.shape, F32)

    wgb_ref[...] = wg_ref[...].astype(BF16)
    wub_ref[...] = wu_ref[...].astype(BF16)
    wdb_ref[...] = wd_ref[...].astype(BF16)

    def body(i, carry):
        rows = pl.ds(pl.multiple_of(i * tr, tr), tr)
        x = x_ref[0, rows, :]
        hid = (_silu(_dot(x, wgb_ref[...])) * _dot(x, wub_ref[...])).astype(BF16)
        acc_ref[rows, :] += _dot(hid, wdb_ref[...])
        return carry

    lax.fori_loop(0, x_ref.shape[1] // tr, body, 0)

    @pl.when(ft == pl.num_programs(1) - 1)
    def _():
        y_ref[0] = acc_ref[...].astype(y_ref.dtype)


def _ffn(xs, w_gate, w_up, w_down, *, layer):
    n_exp, bm, d = xs.shape
    f = w_gate.shape[-1]
    tf = FFN_TILE if f % FFN_TILE == 0 else f
    tr = bm // (-(-bm // FFN_MAX_ROWS))
    return pl.pallas_call(
        functools.partial(_ffn_kernel, tr=tr),
        grid=(n_exp, f // tf),
        in_specs=[pl.BlockSpec((1, bm, d), lambda e, j: (e, 0, 0)),
                  pl.BlockSpec((None, None, d, tf), lambda e, j: (layer, e, 0, j)),
                  pl.BlockSpec((None, None, d, tf), lambda e, j: (layer, e, 0, j)),
                  pl.BlockSpec((None, None, tf, d), lambda e, j: (layer, e, j, 0))],
        out_specs=pl.BlockSpec((1, bm, d), lambda e, j: (e, 0, 0)),
        out_shape=jax.ShapeDtypeStruct((n_exp, bm, d), BF16),
        scratch_shapes=[pltpu.VMEM((bm, d), F32),
                        pltpu.VMEM((d, tf), BF16),
                        pltpu.VMEM((d, tf), BF16),
                        pltpu.VMEM((tf, d), BF16)],
        compiler_params=_cp("parallel", "arbitrary"),
        name="moe_ffn",
    )(xs, w_gate, w_up, w_down)


def _combine_kernel(*refs, nc, cap_c, cap_l, final):
    if final:
        x_ref, ys_ref, sc_ref, ac_ref, mod_ref, fg_ref, o_ref, acc_ref = refs
    else:
        x_ref, ys_ref, sc_ref, ac_ref, mod_ref, o_ref, acc_ref = refs
    n_exp = ys_ref.shape[0]
    tm = x_ref.shape[1]

    def run(k0, kk):
        sc = sc_ref[0]
        ac = ac_ref[0]
        lane = (_iota((tm, kk), 1) + k0).astype(F32)
        acc_ref[...] = jnp.zeros(acc_ref.shape, F32)
        for e in range(n_exp):
            pt = jnp.where(sc[:, e:e + 1] == lane, 1.0, 0.0).astype(BF16)
            acc_ref[...] += ac[:, e:e + 1] * _dot(pt, ys_ref[e, 0, k0:k0 + kk, :])
        x2 = x_ref[0] + mod_ref[0, 0][5:6] * acc_ref[...]
        if final:
            x2 = x2 * lax.rsqrt(jnp.mean(x2 * x2, axis=-1, keepdims=True) + EPS) * fg_ref[...]
        o_ref[0] = x2

    if nc:
        t = pl.program_id(1)

        @pl.when(t < nc)
        def _():
            run(cap_l, cap_c)

        @pl.when(t >= nc)
        def _():
            run(0, cap_l)
    else:
        run(0, cap_l)


def _combine(x, ys, slot_col, aff_col, modsel, final_g, *, nc, cap_c, cap_l, tm):
    b, t, d = x.shape
    n_exp, _, m, _ = ys.shape
    seg = (lambda j: jnp.where(j >= nc, 1, 0)) if nc else (lambda j: 1)
    in_specs = [pl.BlockSpec((1, tm, d), lambda i, j: (i, j, 0)),
                pl.BlockSpec((n_exp, 1, m, d), lambda i, j: (0, i, 0, 0)),
                pl.BlockSpec((1, tm, n_exp), lambda i, j: (i, j, 0)),
                pl.BlockSpec((1, tm, n_exp), lambda i, j: (i, j, 0)),
                pl.BlockSpec((1, 1, 6, d), lambda i, j: (i, seg(j), 0, 0))]
    args = [x, ys, slot_col, aff_col, modsel]
    if final_g is not None:
        in_specs.append(pl.BlockSpec((1, d), lambda i, j: (0, 0)))
        args.append(final_g.reshape(1, d))
    return pl.pallas_call(
        functools.partial(_combine_kernel, nc=nc, cap_c=cap_c, cap_l=cap_l,
                          final=final_g is not None),
        grid=(b, t // tm),
        in_specs=in_specs,
        out_specs=pl.BlockSpec((1, tm, d), lambda i, j: (i, j, 0)),
        out_shape=jax.ShapeDtypeStruct((b, t, d), F32),
        scratch_shapes=[pltpu.VMEM((tm, d), F32)],
        compiler_params=_cp("parallel", "parallel"),
        name="moe_combine",
    )(*args)


def _moe(x, modsel, g, w_router, w_gate, w_up, w_down, final_g, *, layer, n_ctx, n_lat, tm):
    b, t, d = x.shape
    n_exp = w_router.shape[-1]
    cap_l = EC_CAPACITY_FACTOR * n_lat // n_exp
    cap_c = EC_CAPACITY_FACTOR * n_ctx // n_exp
    h, slot_row, aff_row = _route(x, modsel, g, w_router, n_ctx=n_ctx, cap_c=cap_c, cap_l=cap_l,
                                  tr=tm)
    xs = _gather(h, slot_row, n_ctx=n_ctx, cap_c=cap_c, cap_l=cap_l)
    m = xs.shape[2]
    ys = _ffn(xs.reshape(n_exp, b * m, d), w_gate, w_up, w_down, layer=layer)
    return _combine(x, ys.reshape(n_exp, b, m, d), jnp.swapaxes(slot_row, 1, 2),
                    jnp.swapaxes(aff_row, 1, 2), modsel, final_g,
                    nc=n_ctx // tm, cap_c=cap_c, cap_l=cap_l, tm=tm)


def _raster_to_column(t):
    b, l, d = t.shape
    rows = l // GRID_W
    return t.reshape(b, rows, GRID_W, d).transpose(0, 2, 1, 3).reshape(b, l, d)


def _column_to_raster(t):
    b, l, d = t.shape
    rows = l // GRID_W
    return t.reshape(b, GRID_W, rows, d).transpose(0, 2, 1, 3).reshape(b, l, d)


def kernel(x, c, ctx, c_ctx, w_mod, b_mod, norm_g, hg_w_in, hg_lb, hg_onorm, hg_w_out, gd_w_in,
           gd_conv, gd_a_log, gd_dt_bias, gd_onorm, gd_w_out, moe_router, moe_w_gate, moe_w_up,
           moe_w_down, final_g):
    bsz, n_lat, d = x.shape
    n_ctx = ctx.shape[1]
    depth = w_mod.shape[0]
    assert depth == 2, "layer 0 = HGRN2, layer 1 = gated DeltaNet"
    tm = math.gcd(math.gcd(n_ctx, n_lat), MAX_BLOCK_ROWS)
    assert tm % CHUNK == 0
    nc = n_ctx // tm
    o_dt = BF16

    rows = -(-(bsz + 1) // SUBLANES) * SUBLANES
    cond = jnp.zeros((rows, d), F32).at[:bsz].set(c).at[bsz].set(c_ctx)
    mod = _adaln(cond, w_mod, b_mod).reshape(depth, rows, 6, d)

    def modsel(i):
        ctx_mod = jnp.broadcast_to(mod[i, bsz][None], (bsz, 6, d))
        return jnp.stack([ctx_mod, mod[i, :bsz]], axis=1)

    x_all = jnp.concatenate([ctx, x], axis=1)

    ms = modsel(0)
    q, lf, v, gate = _hg_inproj(x_all, ms, norm_g[0, 0], hg_lb, hg_w_in[0], layer=0, nc=nc, tm=tm)
    o_f = _gla_scan(q, lf, v, rev=False, nc=nc, tm=tm, out_dtype=o_dt)
    o_b = _gla_scan(q, lf, v, rev=True, nc=nc, tm=tm, out_dtype=o_dt)
    x_all = _outproj(o_f, o_b, gate, hg_onorm[0], hg_w_out[0], x_all, ms, nc=nc, tm=tm,
                     skip_ctx=False)
    x_all = _moe(x_all, ms, norm_g[0, 1], moe_router[0], moe_w_gate, moe_w_up, moe_w_down, None,
                 layer=0, n_ctx=n_ctx, n_lat=n_lat, tm=tm)

    ms = modsel(1)
    x_cm = jnp.concatenate([x_all[:, :n_ctx], _raster_to_column(x_all[:, n_ctx:])], axis=1)
    q, k, v, gate, ab = _gd_inproj(x_cm, ms, norm_g[1, 0], gd_w_in[0], gd_conv[0], gd_a_log[0],
                                   gd_dt_bias[0], nc=nc, tm=tm)
    o_f = _gdn_scan(q, k, v, ab, rev=False, nc=nc, tm=tm, out_dtype=o_dt)
    o_b = _gdn_scan(q, k, v, ab, rev=True, nc=nc, tm=tm, out_dtype=o_dt)
    x_lat = _outproj(o_f, o_b, gate, gd_onorm[0], gd_w_out[0], x_cm, ms, nc=nc, tm=tm,
                     skip_ctx=True)
    x_lat = _column_to_raster(x_lat)
    return _moe(x_lat, ms, norm_g[1, 1], moe_router[1], moe_w_gate, moe_w_up, moe_w_down, final_g,
                layer=1, n_ctx=0, n_lat=n_lat, tm=tm)
```

```python
import functools
import math

import jax
import jax.numpy as jnp
from jax import lax
from jax.experimental import pallas as pl
from jax.experimental.pallas import tpu as pltpu

F32 = jnp.float32
BF16 = jnp.bfloat16
HIGHEST = lax.Precision.HIGHEST

EPS = 1e-6
LANES = 128
SUBLANES = 8
V7X_VMEM_BYTES = 64 * 1024 * 1024
VMEM_LIMIT = V7X_VMEM_BYTES * 7 // 8

HEAD_DK = 128
GD_DV = 256
GD_CONV_W = 5
GRID_W = 64
EC_CAPACITY_FACTOR = 2
CHUNK = 64
HALF = CHUNK // 2
GD_SOLVE_ROWS = 4 * CHUNK
EXP_CLAMP = 38.0
MAX_BLOCK_ROWS = 256
FFN_TILE = 256
FFN_MAX_ROWS = 1152
AFF_BITS = 31


def _cp(*sem):
    return pltpu.CompilerParams(dimension_semantics=sem, vmem_limit_bytes=VMEM_LIMIT)


def _dot(a, b):
    return jnp.dot(a, b, preferred_element_type=F32)


def _dot_nt(a, b):
    return lax.dot_general(a, b, (((1,), (1,)), ((), ())), preferred_element_type=F32)


def _dot_tn(a, b):
    return lax.dot_general(a, b, (((0,), (0,)), ((), ())), preferred_element_type=F32)


def _dot_hi(a, b):
    return jnp.dot(a, b, preferred_element_type=F32, precision=HIGHEST)


def _dot_nt_hi(a, b):
    return lax.dot_general(a, b, (((1,), (1,)), ((), ())), preferred_element_type=F32,
                           precision=HIGHEST)


def _silu(x):
    return x * jax.nn.sigmoid(x)


def _norm_mod(x, g, shift, scale):
    y = x * lax.rsqrt(jnp.mean(x * x, axis=-1, keepdims=True) + EPS)
    return y * g * (1.0 + scale) + shift


def _iota(shape, dim):
    return lax.broadcasted_iota(jnp.int32, shape, dim)


def _split3(x):
    hi = x.astype(BF16)
    r1 = x - hi.astype(F32)
    mid = r1.astype(BF16)
    lo = (r1 - mid.astype(F32)).astype(BF16)
    return [hi, mid, lo]


def _adaln_kernel(c_ref, w_ref, b_ref, o_ref):
    o_ref[0] = _dot_hi(_silu(c_ref[...]), w_ref[0]) + b_ref[0]


def _adaln(cond, w_mod, b_mod):
    depth, d, n = w_mod.shape
    r = cond.shape[0]
    tn = n // 12
    return pl.pallas_call(
        _adaln_kernel,
        grid=(depth, n // tn),
        in_specs=[pl.BlockSpec((r, d), lambda i, j: (0, 0)),
                  pl.BlockSpec((1, d, tn), lambda i, j: (i, 0, j)),
                  pl.BlockSpec((1, 1, tn), lambda i, j: (i, 0, j))],
        out_specs=pl.BlockSpec((1, r, tn), lambda i, j: (i, 0, j)),
        out_shape=jax.ShapeDtypeStruct((depth, r, n), F32),
        compiler_params=_cp("parallel", "parallel"),
        name="adaln",
    )(cond, w_mod, b_mod.reshape(depth, 1, n))


def _hg_inproj_kernel(x_ref, mod_ref, g_ref, lb_ref, wq_ref, wf_ref, wi_ref, wg_ref,
                      q_ref, lf_ref, v_ref, gate_ref, *, layer, n_heads):
    m = mod_ref[0, 0]
    hb = _norm_mod(x_ref[0], g_ref[...], m[0:1], m[1:2]).astype(BF16)
    fdim = wq_ref.shape[1]
    dk = fdim // n_heads
    dv = wi_ref.shape[1] // n_heads

    q = _silu(_dot(hb, wq_ref[...]))
    for h in range(n_heads):
        q_ref[0, h] = q[:, h * dk:(h + 1) * dk].astype(BF16)

    lbp = lb_ref[...]
    e = jnp.exp(lbp - jnp.max(lbp, axis=0))
    lb = jnp.sum(e[:layer + 1], axis=0) / jnp.sum(e, axis=0)

    zf = _dot(hb, wf_ref[...])
    for d in range(2):
        lbd = lb[d:d + 1]
        f = lbd + (1.0 - lbd) * jax.nn.sigmoid(zf[:, d * fdim:(d + 1) * fdim])
        lf = jnp.log(f)
        for h in range(n_heads):
            lf_ref[d, 0, h] = lf[:, h * dk:(h + 1) * dk]

    v = _dot(hb, wi_ref[...])
    for h in range(n_heads):
        v_ref[0, h] = v[:, h * dv:(h + 1) * dv].astype(BF16)

    gate_ref[0] = _silu(_dot(hb, wg_ref[...])).astype(BF16)


def _hg_inproj(x_all, modsel, g, hg_lb, w_in, *, layer, nc, tm):
    b, t, d = x_all.shape
    fdim = hg_lb.shape[-1]
    n_heads = fdim // HEAD_DK
    dv = d // n_heads
    wq = w_in[:, :fdim].astype(BF16)
    wf = w_in[:, fdim:3 * fdim].astype(BF16)
    wi = w_in[:, 3 * fdim:3 * fdim + d].astype(BF16)
    wg = w_in[:, 3 * fdim + d:].astype(BF16)
    full = lambda a: pl.BlockSpec(a.shape, lambda i, j: (0,) * a.ndim)
    g2 = g.reshape(1, d)
    return pl.pallas_call(
        functools.partial(_hg_inproj_kernel, layer=layer, n_heads=n_heads),
        grid=(b, t // tm),
        in_specs=[pl.BlockSpec((1, tm, d), lambda i, j: (i, j, 0)),
                  pl.BlockSpec((1, 1, 6, d), lambda i, j: (i, jnp.where(j >= nc, 1, 0), 0, 0)),
                  full(g2), full(hg_lb), full(wq), full(wf), full(wi), full(wg)],
        out_specs=[pl.BlockSpec((1, n_heads, tm, HEAD_DK), lambda i, j: (i, 0, j, 0)),
                   pl.BlockSpec((2, 1, n_heads, tm, HEAD_DK), lambda i, j: (0, i, 0, j, 0)),
                   pl.BlockSpec((1, n_heads, tm, dv), lambda i, j: (i, 0, j, 0)),
                   pl.BlockSpec((1, tm, d), lambda i, j: (i, j, 0))],
        out_shape=[jax.ShapeDtypeStruct((b, n_heads, t, HEAD_DK), BF16),
                   jax.ShapeDtypeStruct((2, b, n_heads, t, HEAD_DK), F32),
                   jax.ShapeDtypeStruct((b, n_heads, t, dv), BF16),
                   jax.ShapeDtypeStruct((b, t, d), BF16)],
        compiler_params=_cp("parallel", "parallel"),
        name="hg_inproj",
    )(x_all, modsel, g2, hg_lb, wq, wf, wi, wg)


def _scan_block_index(step, nc, nl, rev):
    if not rev:
        return step
    return jnp.where(step < nc, nc - 1 - step, nc + nl - 1 - (step - nc))


def _chunk_masks(rev):
    r = _iota((CHUNK, CHUNK), 0)
    c = _iota((CHUNK, CHUNK), 1)
    incl = (c >= r) if rev else (c <= r)
    strict = (c > r) if rev else (c < r)
    return incl, strict


def _gla_scan_kernel(q_ref, lf_ref, v_ref, o_ref, st_ref, oin_ref, qc_ref, kh_ref,
                     *, rev, n_heads, nck, tm):
    @pl.when(pl.program_id(1) == 0)
    def _():
        st_ref[...] = jnp.zeros(st_ref.shape, F32)

    dk = q_ref.shape[-1]
    r = _iota((tm, tm), 0)
    c = _iota((tm, tm), 1)
    same_chunk = (r // CHUNK) == (c // CHUNK)
    causal = (c >= r) if rev else (c <= r)
    lmb = jnp.where(same_chunk & causal, 1.0, 0.0).astype(BF16)
    mask_d = ((r // HALF) == (c // HALF)) & causal
    second = 0 if rev else 1
    mask_x = same_chunk & ((r // HALF) % 2 == second) & ((c // HALF) % 2 == 1 - second)

    def per_rows(bh, row_of, span):
        return jnp.concatenate([jnp.broadcast_to(bh[row_of(x0):row_of(x0) + 1], (span, dk))
                                for x0 in range(0, tm, span)], axis=0)

    dec = {}
    for h in range(n_heads):
        g3 = _dot(lmb, jnp.concatenate(_split3(lf_ref[0, 0, h]), axis=1))
        bh = g3[:, :dk] + g3[:, dk:2 * dk] + g3[:, 2 * dk:]
        q = q_ref[0, h].astype(F32)
        k = 1.0 - jnp.exp(lf_ref[0, 0, h])
        bls = [bh[ci * CHUNK:ci * CHUNK + 1] if rev
               else bh[(ci + 1) * CHUNK - 1:(ci + 1) * CHUNK] for ci in range(nck)]
        blc = jnp.concatenate([jnp.broadcast_to(x, (CHUNK, dk)) for x in bls], axis=0)
        qc_ref[h] = (q * jnp.exp(bh)).astype(BF16)
        kh_ref[h] = (k * jnp.exp(blc - bh)).astype(BF16)
        dec[h] = [jnp.exp(x) for x in bls]
        bm = per_rows(bh, lambda x0: x0 + HALF // 2, HALF)
        att_d = _dot_nt((q * jnp.exp(jnp.minimum(bh - bm, EXP_CLAMP))).astype(BF16),
                        (k * jnp.exp(jnp.minimum(bm - bh, EXP_CLAMP))).astype(BF16))
        be = per_rows(bh, lambda x0: x0 + (HALF if rev else HALF - 1), CHUNK)
        att_x = _dot_nt((q * jnp.exp(jnp.minimum(bh - be, EXP_CLAMP))).astype(BF16),
                        (k * jnp.exp(jnp.minimum(be - bh, EXP_CLAMP))).astype(BF16))
        att = jnp.where(mask_d, att_d, 0.0) + jnp.where(mask_x, att_x, 0.0)
        oin_ref[h] = _dot(att.astype(BF16), v_ref[0, h])

    for step in range(nck):
        ci = nck - 1 - step if rev else step
        rows = pl.ds(ci * CHUNK, CHUNK)
        for h in range(n_heads):
            st = st_ref[h]
            o = oin_ref[h, rows, :] + _dot_nt(qc_ref[h, rows, :], st.astype(BF16))
            o_ref[0, h, rows, :] = o.astype(o_ref.dtype)
            st_ref[h] = st * dec[h][ci] + _dot_tn(v_ref[0, h, rows, :], kh_ref[h, rows, :])


def _gla_scan(q, lf, v, *, rev, nc, tm, out_dtype):
    b, n_heads, t, dk = q.shape
    dv = v.shape[-1]
    nblk = t // tm
    nl = nblk - nc
    nck = tm // CHUNK
    d = 1 if rev else 0
    blk = lambda i, s: _scan_block_index(s, nc, nl, rev)
    return pl.pallas_call(
        functools.partial(_gla_scan_kernel, rev=rev, n_heads=n_heads, nck=nck, tm=tm),
        grid=(b, nblk),
        in_specs=[pl.BlockSpec((1, n_heads, tm, dk), lambda i, s: (i, 0, blk(i, s), 0)),
                  pl.BlockSpec((1, 1, n_heads, tm, dk), lambda i, s: (d, i, 0, blk(i, s), 0)),
                  pl.BlockSpec((1, n_heads, tm, dv), lambda i, s: (i, 0, blk(i, s), 0))],
        out_specs=pl.BlockSpec((1, n_heads, tm, dv), lambda i, s: (i, 0, blk(i, s), 0)),
        out_shape=jax.ShapeDtypeStruct((b, n_heads, t, dv), out_dtype),
        scratch_shapes=[pltpu.VMEM((n_heads, dv, dk), F32),
                        pltpu.VMEM((n_heads, tm, dv), F32),
                        pltpu.VMEM((n_heads, tm, dk), BF16),
                        pltpu.VMEM((n_heads, tm, dk), BF16)],
        compiler_params=_cp("parallel", "arbitrary"),
        name="gla_scan_bwd" if rev else "gla_scan_fwd",
    )(q, lf, v)


def _outproj_kernel(of_ref, ob_ref, gate_ref, gain_ref, w_ref, x_ref, mod_ref, o_ref, y_ref,
                    *, n_heads, rpc):
    dv = of_ref.shape[-1]
    for h in range(n_heads):
        o = of_ref[0, h].astype(F32) + ob_ref[0, h].astype(F32)
        cs = slice(h * dv, (h + 1) * dv)
        o = o * lax.rsqrt(jnp.mean(o * o, axis=-1, keepdims=True) + EPS) * gain_ref[:, cs]
        y_ref[:, cs] = (o * gate_ref[0, :, cs].astype(F32)).astype(BF16)
    y = mod_ref[0, 0][2:3] * _dot(y_ref[...], w_ref[...])
    if not rpc:
        o_ref[0] = x_ref[0] + y
    else:
        d = y.shape[1]
        for j in range(y.shape[0] // rpc):
            o_ref[0, :, j * d:(j + 1) * d] = x_ref[0, :, j * d:(j + 1) * d] + y[j * rpc:(j + 1) * rpc]


def _outproj(o_f, o_b, gate, gain, w_out, x_res, modsel, *, nc, tm, rpc=0):
    b, n_heads, t, dv = o_f.shape
    d = w_out.shape[-1]
    hv = n_heads * dv
    off = nc if rpc else 0
    nblk = t // tm - off
    seg = (lambda j: 1) if rpc else (lambda j: jnp.where(j >= nc, 1, 0))
    w = w_out.astype(BF16)
    gain2 = gain.reshape(1, hv)
    if rpc:
        x_spec = pl.BlockSpec((1, rpc, tm // rpc * d), lambda i, j: (i, 0, j))
    else:
        x_spec = pl.BlockSpec((1, tm, d), lambda i, j: (i, j, 0))
    return pl.pallas_call(
        functools.partial(_outproj_kernel, n_heads=n_heads, rpc=rpc),
        grid=(b, nblk),
        in_specs=[pl.BlockSpec((1, n_heads, tm, dv), lambda i, j: (i, 0, j + off, 0)),
                  pl.BlockSpec((1, n_heads, tm, dv), lambda i, j: (i, 0, j + off, 0)),
                  pl.BlockSpec((1, tm, hv), lambda i, j: (i, j + off, 0)),
                  pl.BlockSpec((1, hv), lambda i, j: (0, 0)),
                  pl.BlockSpec((hv, d), lambda i, j: (0, 0)),
                  x_spec,
                  pl.BlockSpec((1, 1, 6, d), lambda i, j: (i, seg(j), 0, 0))],
        out_specs=x_spec,
        out_shape=jax.ShapeDtypeStruct(x_res.shape, F32),
        scratch_shapes=[pltpu.VMEM((tm, hv), BF16)],
        compiler_params=_cp("parallel", "parallel"),
        name="outproj",
    )(o_f, o_b, gate, gain2, w, x_res, modsel)


def _gd_inproj_kernel(cp_ref, cc_ref, cn_ref, lp_ref, lc_ref, ln_ref, mod_ref, g_ref, wqkv_ref,
                      wg_ref, wab_ref, cw_ref, alog_ref, dtb_ref, q_ref, k_ref, v_ref, gate_ref,
                      ab_ref, xs_ref, ys_ref, *, nc, nblk, n_heads, tm, cw, rpc):
    t = pl.program_id(1)
    first = (t == 0) | (t == nc)
    last = (t == nc - 1) | (t == nblk - 1)
    m = mod_ref[0, 0]
    halo = SUBLANES
    nlt = xs_ref.shape[0]
    d = nlt * LANES
    tile = lambda c: slice(c * LANES, (c + 1) * LANES)

    @pl.when(t < nc)
    def _():
        for c in range(nlt):
            xs_ref[c, 0:halo] = cp_ref[0, :, tile(c)]
            xs_ref[c, halo:halo + tm] = cc_ref[0, :, tile(c)]
            xs_ref[c, halo + tm:] = cn_ref[0, :, tile(c)]

    @pl.when(t >= nc)
    def _():
        for c in range(nlt):
            xs_ref[c, 0:halo] = lp_ref[0, :, tile(c)]
            for j in range(tm // rpc):
                xs_ref[c, halo + j * rpc:halo + (j + 1) * rpc] = (
                    lc_ref[0, :, j * d + c * LANES:j * d + (c + 1) * LANES])
            xs_ref[c, halo + tm:] = ln_ref[0, :, tile(c)]

    xc = jnp.concatenate([xs_ref[c, halo:halo + tm] for c in range(nlt)], axis=1)
    hc = _norm_mod(xc, g_ref[...], m[0:1], m[1:2]).astype(BF16)

    rows = tm + 2 * halo
    ni = rows // SUBLANES
    xp = jnp.concatenate(
        [jnp.concatenate([xs_ref[c, pl.ds(i, SUBLANES, stride=ni), :] for i in range(ni)], axis=0)
         for c in range(nlt)], axis=1)
    he = _norm_mod(xp, g_ref[...], m[0:1], m[1:2]).astype(BF16)
    row = _iota((rows, 1), 0)
    tok = ni * (row % SUBLANES) + row // SUBLANES
    valid = ((tok >= halo) | jnp.logical_not(first)) & ((tok < tm + halo) | jnp.logical_not(last))

    gate_ref[0] = _silu(_dot(hc, wg_ref[...])).astype(BF16)

    zab = _dot(hc, wab_ref[...])
    lane = _iota(zab.shape, 1)
    loga = -jnp.exp(alog_ref[...]) * jax.nn.softplus(zab + dtb_ref[...])
    ab_ref[0] = jnp.where(lane < 2 * n_heads, loga, jax.nn.sigmoid(zab))

    qk = n_heads * HEAD_DK
    nchan = wqkv_ref.shape[1]
    pad = GD_CONV_W // 2
    for cc in range(nchan // cw):
        c0 = cc * cw
        z = jnp.where(valid, _dot(he, wqkv_ref[:, c0:c0 + cw]), 0.0)
        z3 = z.reshape(ni, SUBLANES, cw)
        zz = jnp.concatenate([pltpu.roll(z3[ni - pad:], 1, 1), z3,
                              pltpu.roll(z3[:pad], SUBLANES - 1, 1)], axis=0)
        acc = cw_ref[0:1, c0:c0 + cw] * zz[0:ni]
        for j in range(1, GD_CONV_W):
            acc = acc + cw_ref[j:j + 1, c0:c0 + cw] * zz[j:j + ni]
        u = _silu(acc)
        if c0 < 2 * qk:
            scale = HEAD_DK ** -0.5 if c0 < qk else 1.0
            parts = []
            for j in range(cw // HEAD_DK):
                tt = u[:, :, j * HEAD_DK:(j + 1) * HEAD_DK]
                parts.append(tt * (lax.rsqrt(jnp.sum(tt * tt, axis=-1, keepdims=True) + EPS)
                                   * scale))
            u = jnp.concatenate(parts, axis=-1)
        for c in range(cw // LANES):
            for i in range(ni):
                ys_ref[c, pl.ds(i, SUBLANES, stride=ni), :] = u[i][:, tile(c)]

        def centre(lane0, width):
            return jnp.concatenate([ys_ref[c, halo:halo + tm] for c in
                                    range(lane0 // LANES, (lane0 + width) // LANES)], axis=1)

        if c0 < 2 * qk:
            dst, base = (q_ref, c0) if c0 < qk else (k_ref, c0 - qk)
            for j in range(cw // HEAD_DK):
                dst[0, base // HEAD_DK + j] = centre(j * HEAD_DK, HEAD_DK).astype(BF16)
        else:
            base = c0 - 2 * qk
            for j in range(cw // GD_DV):
                v_ref[0, base // GD_DV + j] = centre(j * GD_DV, GD_DV).astype(BF16)


def _gd_inproj(x_ctx, x_lat, modsel, g, w_in, conv_w, a_log, dt_bias, *, tm):
    b, n_ctx, d = x_ctx.shape
    n_lat = x_lat.shape[1]
    rpc = n_lat // GRID_W
    assert rpc % SUBLANES == 0 and tm % rpc == 0 and n_ctx % tm == 0 and n_ctx > 0
    t = n_ctx + n_lat
    nc = n_ctx // tm
    cpb = tm // rpc
    lat_v = x_lat.reshape(b, rpc, GRID_W * d)
    n_heads = a_log.shape[-1]
    qk = n_heads * HEAD_DK
    vd = n_heads * GD_DV
    nchan = 2 * qk + vd
    nblk = t // tm
    cw = min(512, qk)
    wqkv = w_in[:, :nchan].astype(BF16)
    wg = w_in[:, nchan:nchan + vd].astype(BF16)
    wab = jnp.pad(w_in[:, nchan + vd:], ((0, 0), (0, LANES - 4 * n_heads))).astype(BF16)
    alog = jnp.pad(a_log.reshape(1, 2 * n_heads), ((0, 0), (0, LANES - 2 * n_heads)))
    dtb = jnp.pad(dt_bias.reshape(1, 2 * n_heads), ((0, 0), (0, LANES - 2 * n_heads)))
    g2 = g.reshape(1, d)
    full = lambda a: pl.BlockSpec(a.shape, lambda i, j: (0,) * a.ndim)
    spb = tm // SUBLANES
    last_slab = n_ctx // SUBLANES - 1
    last_col = GRID_W - 1
    return pl.pallas_call(
        functools.partial(_gd_inproj_kernel, nc=nc, nblk=nblk, n_heads=n_heads, tm=tm, cw=cw,
                          rpc=rpc),
        grid=(b, nblk),
        in_specs=[pl.BlockSpec((1, SUBLANES, d),
                               lambda i, j: (i, jnp.clip(j * spb - 1, 0, last_slab), 0)),
                  pl.BlockSpec((1, tm, d), lambda i, j: (i, jnp.minimum(j, nc - 1), 0)),
                  pl.BlockSpec((1, SUBLANES, d),
                               lambda i, j: (i, jnp.clip((j + 1) * spb, 0, last_slab), 0)),
                  pl.BlockSpec((1, SUBLANES, d),
                               lambda i, j: (i, rpc // SUBLANES - 1,
                                             jnp.clip((j - nc) * cpb - 1, 0, last_col))),
                  pl.BlockSpec((1, rpc, cpb * d), lambda i, j: (i, 0, jnp.maximum(j - nc, 0))),
                  pl.BlockSpec((1, SUBLANES, d),
                               lambda i, j: (i, 0, jnp.clip((j - nc + 1) * cpb, 0, last_col))),
                  pl.BlockSpec((1, 1, 6, d), lambda i, j: (i, jnp.where(j >= nc, 1, 0), 0, 0)),
                  full(g2), full(wqkv), full(wg), full(wab), full(conv_w), full(alog), full(dtb)],
        out_specs=[pl.BlockSpec((1, n_heads, tm, HEAD_DK), lambda i, j: (i, 0, j, 0)),
                   pl.BlockSpec((1, n_heads, tm, HEAD_DK), lambda i, j: (i, 0, j, 0)),
                   pl.BlockSpec((1, n_heads, tm, GD_DV), lambda i, j: (i, 0, j, 0)),
                   pl.BlockSpec((1, tm, vd), lambda i, j: (i, j, 0)),
                   pl.BlockSpec((1, tm, LANES), lambda i, j: (i, j, 0))],
        out_shape=[jax.ShapeDtypeStruct((b, n_heads, t, HEAD_DK), BF16),
                   jax.ShapeDtypeStruct((b, n_heads, t, HEAD_DK), BF16),
                   jax.ShapeDtypeStruct((b, n_heads, t, GD_DV), BF16),
                   jax.ShapeDtypeStruct((b, t, vd), BF16),
                   jax.ShapeDtypeStruct((b, t, LANES), F32)],
        scratch_shapes=[pltpu.VMEM((d // LANES, tm + 2 * SUBLANES, LANES), F32),
                        pltpu.VMEM((cw // LANES, tm + 2 * SUBLANES, LANES), F32)],
        compiler_params=_cp("parallel", "parallel"),
        name="gd_inproj",
    )(x_ctx, x_ctx, x_ctx, lat_v, lat_v, lat_v, modsel, g2, wqkv, wg, wab, conv_w, alog, dtb)


def _neumann_inverse(a, eye):
    n = a.shape[0]
    t = eye - a
    ab = a.astype(BF16)
    p = _dot(ab, ab)
    lvl = 2
    while lvl < CHUNK:
        pb = p.astype(BF16)
        tb = t.astype(BF16)
        if 2 * lvl >= CHUNK:
            t = t + _dot(pb, tb)
        else:
            out = _dot(pb, jnp.concatenate([pb, tb], axis=1))
            p = out[:, :n]
            t = t + out[:, n:]
        lvl *= 2
    return t


def _gdn_scan_kernel(q_ref, k_ref, v_ref, ab_ref, o_ref, s_ref, u_ref, l1_ref, l2_ref,
                     *, rev, d, n_heads, nck, tm):
    @pl.when(pl.program_id(1) == 0)
    def _():
        s_ref[...] = jnp.zeros(s_ref.shape, F32)

    def masks(n):
        r = _iota((n, n), 0)
        c = _iota((n, n), 1)
        same = (r // CHUNK) == (c // CHUNK)
        return (same & ((c >= r) if rev else (c <= r)), same & ((c > r) if rev else (c < r)),
                jnp.where(r == c, 1.0, 0.0))

    pb = min(tm, GD_SOLVE_ROWS)
    incl, strict, eye = masks(pb)
    dv = v_ref.shape[-1]

    ab = ab_ref[0]
    g3 = _dot(jnp.where(masks(tm)[0], 1.0, 0.0).astype(BF16),
              jnp.concatenate(_split3(ab), axis=1))
    gcol = g3[:, :LANES] + g3[:, LANES:2 * LANES] + g3[:, 2 * LANES:]
    eye_l = jnp.where(_iota((LANES, LANES), 0) == _iota((LANES, LANES), 1), 1.0, 0.0).astype(BF16)
    gr3 = _dot_nt(eye_l, jnp.concatenate(_split3(gcol), axis=0))
    grow = gr3[:, :tm] + gr3[:, tm:2 * tm] + gr3[:, 2 * tm:]

    incl_c, _ = _chunk_masks(rev)
    egl = {}
    for h in range(n_heads):
        ca = d * n_heads + h
        gc = gcol[:, ca:ca + 1]
        bc = ab[:, 2 * n_heads + ca:2 * n_heads + ca + 1]
        q = q_ref[0, h]
        k = k_ref[0, h]
        kf = k.astype(F32)
        eg = jnp.exp(gc)
        rhs = jnp.concatenate([(bc * v_ref[0, h].astype(F32)).astype(BF16),
                               (bc * eg * kf).astype(BF16)], axis=1)
        uws = []
        for p0 in range(0, tm, pb):
            gam = jnp.where(incl, jnp.exp(jnp.minimum(
                gc[p0:p0 + pb] - grow[ca:ca + 1, p0:p0 + pb], 0.0)), 0.0)
            kp = k[p0:p0 + pb]
            a = jnp.where(strict, bc[p0:p0 + pb] * _dot_nt(kp, kp) * gam, 0.0)
            uws.append(_dot(_neumann_inverse(a, eye).astype(BF16), rhs[p0:p0 + pb]))
        uw = jnp.concatenate(uws, axis=0)
        u_ref[h] = uw[:, :dv]
        wb = uw[:, dv:].astype(BF16)
        qg = (q.astype(F32) * eg).astype(BF16)
        gls = [gc[ci * CHUNK:ci * CHUNK + 1] if rev
               else gc[(ci + 1) * CHUNK - 1:(ci + 1) * CHUNK] for ci in range(nck)]
        glc = jnp.concatenate([jnp.broadcast_to(g, (CHUNK, 1)) for g in gls], axis=0)
        kd = kf * jnp.exp(glc - gc)
        egl[h] = [jnp.exp(g) for g in gls]
        for ci in range(nck):
            c0 = ci * CHUNK
            l1_ref[h, ci, :CHUNK] = wb[c0:c0 + CHUNK]
            l1_ref[h, ci, CHUNK:] = qg[c0:c0 + CHUNK]
            gam_c = jnp.where(incl_c, jnp.exp(jnp.minimum(
                gc[c0:c0 + CHUNK] - grow[ca:ca + 1, c0:c0 + CHUNK], 0.0)), 0.0)
            qk_c = _dot_nt(q[c0:c0 + CHUNK], k[c0:c0 + CHUNK])
            l2_ref[h, ci, :CHUNK] = (qk_c * gam_c).astype(BF16)
            l2_ref[h, ci, CHUNK:] = kd[c0:c0 + CHUNK].T.astype(BF16)

    for step in range(nck):
        ci = nck - 1 - step if rev else step
        rows = pl.ds(ci * CHUNK, CHUNK)
        for h in range(n_heads):
            s = s_ref[h]
            r1 = _dot(l1_ref[h, ci], s.astype(BF16))
            weff = (u_ref[h, rows, :] - r1[:CHUNK]).astype(BF16)
            r2 = _dot(l2_ref[h, ci], weff)
            o_ref[0, h, rows, :] = (r2[:CHUNK] + r1[CHUNK:]).astype(o_ref.dtype)
            s_ref[h] = s * egl[h][ci] + r2[CHUNK:]


def _gdn_scan(q, k, v, ab, *, rev, nc, tm, out_dtype):
    b, n_heads, t, dk = q.shape
    dv = v.shape[-1]
    nblk = t // tm
    nl = nblk - nc
    nck = tm // CHUNK
    d = 1 if rev else 0
    blk = lambda s: _scan_block_index(s, nc, nl, rev)
    return pl.pallas_call(
        functools.partial(_gdn_scan_kernel, rev=rev, d=d, n_heads=n_heads, nck=nck, tm=tm),
        grid=(b, nblk),
        in_specs=[pl.BlockSpec((1, n_heads, tm, dk), lambda i, s: (i, 0, blk(s), 0)),
                  pl.BlockSpec((1, n_heads, tm, dk), lambda i, s: (i, 0, blk(s), 0)),
                  pl.BlockSpec((1, n_heads, tm, dv), lambda i, s: (i, 0, blk(s), 0)),
                  pl.BlockSpec((1, tm, LANES), lambda i, s: (i, blk(s), 0))],
        out_specs=pl.BlockSpec((1, n_heads, tm, dv), lambda i, s: (i, 0, blk(s), 0)),
        out_shape=jax.ShapeDtypeStruct((b, n_heads, t, dv), out_dtype),
        scratch_shapes=[pltpu.VMEM((n_heads, dk, dv), F32),
                        pltpu.VMEM((n_heads, tm, dv), F32),
                        pltpu.VMEM((n_heads, nck, 2 * CHUNK, dk), BF16),
                        pltpu.VMEM((n_heads, nck, CHUNK + dk, CHUNK), BF16)],
        compiler_params=_cp("parallel", "arbitrary"),
        name="gdn_scan_bwd" if rev else "gdn_scan_fwd",
    )(q, k, v, ab)


def _route_kernel(x_ref, mod_ref, g_ref, wrt_ref, h_ref, slot_ref, aff_ref, lg_ref,
                  *, n_ctx, cap_c, cap_l, tr):
    n_exp, t = lg_ref.shape
    for rt in range(t // tr):
        m = mod_ref[0, 0 if rt * tr < n_ctx else 1]
        rows = slice(rt * tr, (rt + 1) * tr)
        h = _norm_mod(x_ref[0, rows, :], g_ref[...], m[3:4], m[4:5])
        h_ref[0, rows, :] = h.astype(BF16)
        lg_ref[:, rows] = _dot_nt_hi(wrt_ref[...], h)
    lg = lg_ref[...]
    e = jnp.exp(lg - jnp.max(lg, axis=0, keepdims=True))
    aff = e / jnp.sum(e, axis=0, keepdims=True)
    aff_ref[0] = aff
    bits = lax.bitcast_convert_type(aff, jnp.int32)
    lane = _iota((n_exp, t), 1)

    if n_ctx:
        is_ctx = lane < n_ctx
        regions = [(is_ctx, cap_c), (jnp.logical_not(is_ctx), cap_l)]
    else:
        regions = [(None, cap_l)]

    def count(pred, mask):
        p = pred if mask is None else (pred & mask)
        return jnp.sum(jnp.where(p, 1.0, 0.0), axis=1, keepdims=True)

    def thr_body(i, thrs):
        bit = lax.shift_left(jnp.int32(1), AFF_BITS - 1 - i)
        out = []
        for (mask, cap), thr in zip(regions, thrs):
            cand = thr | bit
            out.append(jnp.where(count(bits >= cand, mask) >= cap, cand, thr))
        return tuple(out)

    zero = jnp.zeros((n_exp, 1), jnp.int32)
    thrs = lax.fori_loop(0, AFF_BITS, thr_body, tuple(zero for _ in regions))

    idx_bits = t.bit_length()
    sel = None
    for (mask, cap), thr in zip(regions, thrs):
        gt = bits > thr
        tie = bits == thr
        need = cap - count(gt, mask)

        def j_body(i, j, tie=tie, mask=mask, need=need):
            cand = j | lax.shift_left(jnp.int32(1), idx_bits - 1 - i)
            return jnp.where(count(tie & (lane < cand), mask) <= need, cand, j)

        jmax = lax.fori_loop(0, idx_bits, j_body, zero)
        s = gt | (tie & (lane < jmax))
        if mask is not None:
            s = s & mask
        sel = s if sel is None else (sel | s)

    lt = LANES if t % LANES == 0 else CHUNK
    ut = jnp.where(_iota((lt, lt), 0) < _iota((lt, lt), 1), 1.0, 0.0).astype(BF16)
    self = jnp.where(sel, 1.0, 0.0)
    run = jnp.zeros((n_exp, 1), F32)
    pres = []
    for i in range(t // lt):
        tile = self[:, i * lt:(i + 1) * lt]
        pres.append(_dot(tile.astype(BF16), ut) + run)
        run = run + jnp.sum(tile, axis=1, keepdims=True)
    pre = jnp.concatenate(pres, axis=1)
    if n_ctx:
        slot = jnp.where(is_ctx, pre + cap_l, pre - cap_c)
    else:
        slot = pre
    slot_ref[0] = jnp.where(sel, slot, -1.0)


def _route(x, modsel, g, w_router, *, n_ctx, cap_c, cap_l, tr):
    b, t, d = x.shape
    n_exp = w_router.shape[-1]
    wrt = w_router.T
    g2 = g.reshape(1, d)
    return pl.pallas_call(
        functools.partial(_route_kernel, n_ctx=n_ctx, cap_c=cap_c, cap_l=cap_l, tr=tr),
        grid=(b,),
        in_specs=[pl.BlockSpec((1, t, d), lambda i: (i, 0, 0)),
                  pl.BlockSpec((1, 2, 6, d), lambda i: (i, 0, 0, 0)),
                  pl.BlockSpec((1, d), lambda i: (0, 0)),
                  pl.BlockSpec((n_exp, d), lambda i: (0, 0))],
        out_specs=[pl.BlockSpec((1, t, d), lambda i: (i, 0, 0)),
                   pl.BlockSpec((1, n_exp, t), lambda i: (i, 0, 0)),
                   pl.BlockSpec((1, n_exp, t), lambda i: (i, 0, 0))],
        out_shape=[jax.ShapeDtypeStruct((b, t, d), BF16),
                   jax.ShapeDtypeStruct((b, n_exp, t), F32),
                   jax.ShapeDtypeStruct((b, n_exp, t), F32)],
        scratch_shapes=[pltpu.VMEM((n_exp, t), F32)],
        compiler_params=_cp("parallel"),
        name="moe_route",
    )(x, modsel, g2, wrt)


def _gather_kernel(slot_ref, h_ref, x_ref, *, n_ctx, cap_c, cap_l):
    slot = slot_ref[0, 0]
    t = slot.shape[1]
    sl = slot[:, n_ctx:]
    p = jnp.where(sl == _iota((cap_l, t - n_ctx), 0).astype(F32), 1.0, 0.0).astype(BF16)
    x_ref[0, 0, 0:cap_l, :] = _dot(p, h_ref[0, n_ctx:, :]).astype(BF16)
    if n_ctx:
        sc = slot[:, :n_ctx] - float(cap_l)
        p = jnp.where(sc == _iota((cap_c, n_ctx), 0).astype(F32), 1.0, 0.0).astype(BF16)
        x_ref[0, 0, cap_l:, :] = _dot(p, h_ref[0, :n_ctx, :]).astype(BF16)


def _gather(h, slot_row, *, n_ctx, cap_c, cap_l):
    b, t, d = h.shape
    n_exp = slot_row.shape[1]
    m = cap_l + (cap_c if n_ctx else 0)
    return pl.pallas_call(
        functools.partial(_gather_kernel, n_ctx=n_ctx, cap_c=cap_c, cap_l=cap_l),
        grid=(b, n_exp),
        in_specs=[pl.BlockSpec((1, 1, 1, t), lambda i, e: (i, e, 0, 0)),
                  pl.BlockSpec((1, t, d), lambda i, e: (i, 0, 0))],
        out_specs=pl.BlockSpec((1, 1, m, d), lambda i, e: (e, i, 0, 0)),
        out_shape=jax.ShapeDtypeStruct((n_exp, b, m, d), BF16),
        compiler_params=_cp("parallel", "parallel"),
        name="moe_gather",
    )(slot_row.reshape(b, n_exp, 1, t), h)


def _ffn_kernel(x_ref, wg_ref, wu_ref, wd_ref, y_ref, acc_ref, wgb_ref, wub_ref, wdb_ref, *, tr):
    ft = pl.program_id(1)

    @pl.when(ft == 0)
    def _():
        acc_ref[...] = jnp.zeros(acc_ref.shape, F32)

    wgb_ref[...] = wg_ref[...].astype(BF16)
    wub_ref[...] = wu_ref[...].astype(BF16)
    wdb_ref[...] = wd_ref[...].astype(BF16)

    def body(i, carry):
        rows = pl.ds(pl.multiple_of(i * tr, tr), tr)
        x = x_ref[0, rows, :]
        hid = (_silu(_dot(x, wgb_ref[...])) * _dot(x, wub_ref[...])).astype(BF16)
        acc_ref[rows, :] += _dot(hid, wdb_ref[...])
        return carry

    lax.fori_loop(0, x_ref.shape[1] // tr, body, 0)

    @pl.when(ft == pl.num_programs(1) - 1)
    def _():
        y_ref[0] = acc_ref[...].astype(y_ref.dtype)


def _ffn(xs, w_gate, w_up, w_down, *, layer):
    n_exp, bm, d = xs.shape
    f = w_gate.shape[-1]
    tf = FFN_TILE if f % FFN_TILE == 0 else f
    tr = bm // (-(-bm // FFN_MAX_ROWS))
    return pl.pallas_call(
        functools.partial(_ffn_kernel, tr=tr),
        grid=(n_exp, f // tf),
        in_specs=[pl.BlockSpec((1, bm, d), lambda e, j: (e, 0, 0)),
                  pl.BlockSpec((None, None, d, tf), lambda e, j: (layer, e, 0, j)),
                  pl.BlockSpec((None, None, d, tf), lambda e, j: (layer, e, 0, j)),
                  pl.BlockSpec((None, None, tf, d), lambda e, j: (layer, e, j, 0))],
        out_specs=pl.BlockSpec((1, bm, d), lambda e, j: (e, 0, 0)),
        out_shape=jax.ShapeDtypeStruct((n_exp, bm, d), BF16),
        scratch_shapes=[pltpu.VMEM((bm, d), F32),
                        pltpu.VMEM((d, tf), BF16),
                        pltpu.VMEM((d, tf), BF16),
                        pltpu.VMEM((tf, d), BF16)],
        compiler_params=_cp("parallel", "arbitrary"),
        name="moe_ffn",
    )(xs, w_gate, w_up, w_down)


def _combine_kernel(*refs, nc, cap_c, cap_l, final):
    x_ref, ys_ref, sc_ref, ac_ref, mod_ref = refs[:5]
    fg_ref = refs[5] if final else None
    outs = refs[5 + bool(final):-1]
    acc_ref = refs[-1]
    n_exp = ys_ref.shape[0]
    tm = x_ref.shape[1]

    def run(k0, kk, o_ref):
        sc = sc_ref[0]
        ac = ac_ref[0]
        lane = (_iota((tm, kk), 1) + k0).astype(F32)
        acc_ref[...] = jnp.zeros(acc_ref.shape, F32)
        for e in range(n_exp):
            pt = jnp.where(sc[:, e:e + 1] == lane, 1.0, 0.0).astype(BF16)
            acc_ref[...] += ac[:, e:e + 1] * _dot(pt, ys_ref[e, 0, k0:k0 + kk, :])
        x2 = x_ref[0] + mod_ref[0, 0][5:6] * acc_ref[...]
        if final:
            x2 = x2 * lax.rsqrt(jnp.mean(x2 * x2, axis=-1, keepdims=True) + EPS) * fg_ref[...]
        o_ref[0] = x2

    if nc:
        t = pl.program_id(1)

        @pl.when(t < nc)
        def _():
            run(cap_l, cap_c, outs[0])

        @pl.when(t >= nc)
        def _():
            run(0, cap_l, outs[1])
    else:
        run(0, cap_l, outs[0])


def _combine(x, ys, slot_col, aff_col, modsel, final_g, *, nc, cap_c, cap_l, tm):
    b, t, d = x.shape
    n_exp, _, m, _ = ys.shape
    seg = (lambda j: jnp.where(j >= nc, 1, 0)) if nc else (lambda j: 1)
    if nc:
        out_specs = [pl.BlockSpec((1, tm, d), lambda i, j: (i, jnp.minimum(j, nc - 1), 0)),
                     pl.BlockSpec((1, tm, d), lambda i, j: (i, jnp.maximum(j - nc, 0), 0))]
        out_shape = [jax.ShapeDtypeStruct((b, nc * tm, d), F32),
                     jax.ShapeDtypeStruct((b, t - nc * tm, d), F32)]
    else:
        out_specs = pl.BlockSpec((1, tm, d), lambda i, j: (i, j, 0))
        out_shape = jax.ShapeDtypeStruct((b, t, d), F32)
    in_specs = [pl.BlockSpec((1, tm, d), lambda i, j: (i, j, 0)),
                pl.BlockSpec((n_exp, 1, m, d), lambda i, j: (0, i, 0, 0)),
                pl.BlockSpec((1, tm, n_exp), lambda i, j: (i, j, 0)),
                pl.BlockSpec((1, tm, n_exp), lambda i, j: (i, j, 0)),
                pl.BlockSpec((1, 1, 6, d), lambda i, j: (i, seg(j), 0, 0))]
    args = [x, ys, slot_col, aff_col, modsel]
    if final_g is not None:
        in_specs.append(pl.BlockSpec((1, d), lambda i, j: (0, 0)))
        args.append(final_g.reshape(1, d))
    return pl.pallas_call(
        functools.partial(_combine_kernel, nc=nc, cap_c=cap_c, cap_l=cap_l,
                          final=final_g is not None),
        grid=(b, t // tm),
        in_specs=in_specs,
        out_specs=out_specs,
        out_shape=out_shape,
        scratch_shapes=[pltpu.VMEM((tm, d), F32)],
        compiler_params=_cp("parallel", "arbitrary"),
        name="moe_combine",
    )(*args)


def _moe(x, modsel, g, w_router, w_gate, w_up, w_down, final_g, *, layer, n_ctx, n_lat, tm):
    b, t, d = x.shape
    n_exp = w_router.shape[-1]
    cap_l = EC_CAPACITY_FACTOR * n_lat // n_exp
    cap_c = EC_CAPACITY_FACTOR * n_ctx // n_exp
    h, slot_row, aff_row = _route(x, modsel, g, w_router, n_ctx=n_ctx, cap_c=cap_c, cap_l=cap_l,
                                  tr=tm)
    xs = _gather(h, slot_row, n_ctx=n_ctx, cap_c=cap_c, cap_l=cap_l)
    m = xs.shape[2]
    ys = _ffn(xs.reshape(n_exp, b * m, d), w_gate, w_up, w_down, layer=layer)
    return _combine(x, ys.reshape(n_exp, b, m, d), jnp.swapaxes(slot_row, 1, 2),
                    jnp.swapaxes(aff_row, 1, 2), modsel, final_g,
                    nc=n_ctx // tm, cap_c=cap_c, cap_l=cap_l, tm=tm)


def kernel(x, c, ctx, c_ctx, w_mod, b_mod, norm_g, hg_w_in, hg_lb, hg_onorm, hg_w_out, gd_w_in,
           gd_conv, gd_a_log, gd_dt_bias, gd_onorm, gd_w_out, moe_router, moe_w_gate, moe_w_up,
           moe_w_down, final_g):
    bsz, n_lat, d = x.shape
    n_ctx = ctx.shape[1]
    depth = w_mod.shape[0]
    assert depth == 2, "layer 0 = HGRN2, layer 1 = gated DeltaNet"
    tm = math.gcd(math.gcd(n_ctx, n_lat), MAX_BLOCK_ROWS)
    assert tm % CHUNK == 0
    nc = n_ctx // tm
    o_dt = BF16

    rows = -(-(bsz + 1) // SUBLANES) * SUBLANES
    cond = jnp.zeros((rows, d), F32).at[:bsz].set(c).at[bsz].set(c_ctx)
    mod = _adaln(cond, w_mod, b_mod).reshape(depth, rows, 6, d)

    def modsel(i):
        ctx_mod = jnp.broadcast_to(mod[i, bsz][None], (bsz, 6, d))
        return jnp.stack([ctx_mod, mod[i, :bsz]], axis=1)

    x_all = jnp.concatenate([ctx, x], axis=1)

    ms = modsel(0)
    q, lf, v, gate = _hg_inproj(x_all, ms, norm_g[0, 0], hg_lb, hg_w_in[0], layer=0, nc=nc, tm=tm)
    o_f = _gla_scan(q, lf, v, rev=False, nc=nc, tm=tm, out_dtype=o_dt)
    o_b = _gla_scan(q, lf, v, rev=True, nc=nc, tm=tm, out_dtype=o_dt)
    x_all = _outproj(o_f, o_b, gate, hg_onorm[0], hg_w_out[0], x_all, ms, nc=nc, tm=tm)
    x_ctx, x_lat = _moe(x_all, ms, norm_g[0, 1], moe_router[0], moe_w_gate, moe_w_up, moe_w_down,
                        None, layer=0, n_ctx=n_ctx, n_lat=n_lat, tm=tm)

    ms = modsel(1)
    rpc = n_lat // GRID_W
    q, k, v, gate, ab = _gd_inproj(x_ctx, x_lat, ms, norm_g[1, 0], gd_w_in[0], gd_conv[0],
                                   gd_a_log[0], gd_dt_bias[0], tm=tm)
    o_f = _gdn_scan(q, k, v, ab, rev=False, nc=nc, tm=tm, out_dtype=o_dt)
    o_b = _gdn_scan(q, k, v, ab, rev=True, nc=nc, tm=tm, out_dtype=o_dt)
    x_lat = _outproj(o_f, o_b, gate, gd_onorm[0], gd_w_out[0],
                     x_lat.reshape(bsz, rpc, GRID_W * d), ms, nc=nc, tm=tm, rpc=rpc)
    x_lat = x_lat.reshape(bsz, n_lat, d)
    return _moe(x_lat, ms, norm_g[1, 1], moe_router[1], moe_w_gate, moe_w_up, moe_w_down, final_g,
                layer=1, n_ctx=0, n_lat=n_lat, tm=tm)
```

```python
import functools
import math

import jax
import jax.numpy as jnp
from jax import lax
from jax.experimental import pallas as pl
from jax.experimental.pallas import tpu as pltpu

F32 = jnp.float32
BF16 = jnp.bfloat16
HIGHEST = lax.Precision.HIGHEST

EPS = 1e-6
LANES = 128
SUBLANES = 8
V7X_VMEM_BYTES = 64 * 1024 * 1024
VMEM_LIMIT = V7X_VMEM_BYTES * 7 // 8

HEAD_DK = 128
GD_DV = 256
GD_CONV_W = 5
GRID_W = 64
EC_CAPACITY_FACTOR = 2
CHUNK = 64
HALF = CHUNK // 2
GD_SOLVE_ROWS = 4 * CHUNK
EXP_CLAMP = 38.0
MAX_BLOCK_ROWS = 256
FFN_TILE = 256
FFN_MAX_ROWS = 1152
AFF_BITS = 31


def _cp(*sem):
    return pltpu.CompilerParams(dimension_semantics=sem, vmem_limit_bytes=VMEM_LIMIT)


def _dot(a, b):
    return jnp.dot(a, b, preferred_element_type=F32)


def _dot_nt(a, b):
    return lax.dot_general(a, b, (((1,), (1,)), ((), ())), preferred_element_type=F32)


def _dot_tn(a, b):
    return lax.dot_general(a, b, (((0,), (0,)), ((), ())), preferred_element_type=F32)


def _dot_hi(a, b):
    return jnp.dot(a, b, preferred_element_type=F32, precision=HIGHEST)


def _dot_nt_hi(a, b):
    return lax.dot_general(a, b, (((1,), (1,)), ((), ())), preferred_element_type=F32,
                           precision=HIGHEST)


def _silu(x):
    return x * jax.nn.sigmoid(x)


def _norm_mod(x, g, shift, scale):
    y = x * lax.rsqrt(jnp.mean(x * x, axis=-1, keepdims=True) + EPS)
    return y * g * (1.0 + scale) + shift


def _iota(shape, dim):
    return lax.broadcasted_iota(jnp.int32, shape, dim)


def _split3(x):
    hi = x.astype(BF16)
    r1 = x - hi.astype(F32)
    mid = r1.astype(BF16)
    lo = (r1 - mid.astype(F32)).astype(BF16)
    return [hi, mid, lo]


def _adaln_kernel(c_ref, w_ref, b_ref, o_ref):
    o_ref[0] = _dot_hi(_silu(c_ref[...]), w_ref[0]) + b_ref[0]


def _adaln(cond, w_mod, b_mod):
    depth, d, n = w_mod.shape
    r = cond.shape[0]
    tn = n // 12
    return pl.pallas_call(
        _adaln_kernel,
        grid=(depth, n // tn),
        in_specs=[pl.BlockSpec((r, d), lambda i, j: (0, 0)),
                  pl.BlockSpec((1, d, tn), lambda i, j: (i, 0, j)),
                  pl.BlockSpec((1, 1, tn), lambda i, j: (i, 0, j))],
        out_specs=pl.BlockSpec((1, r, tn), lambda i, j: (i, 0, j)),
        out_shape=jax.ShapeDtypeStruct((depth, r, n), F32),
        compiler_params=_cp("parallel", "parallel"),
        name="adaln",
    )(cond, w_mod, b_mod.reshape(depth, 1, n))


def _hg_inproj_kernel(x_ref, mod_ref, g_ref, lb_ref, wq_ref, wf_ref, wi_ref, wg_ref,
                      q_ref, lf_ref, v_ref, gate_ref, *, layer, n_heads):
    m = mod_ref[0, 0]
    hb = _norm_mod(x_ref[0], g_ref[...], m[0:1], m[1:2]).astype(BF16)
    fdim = wq_ref.shape[1]
    dk = fdim // n_heads
    dv = wi_ref.shape[1] // n_heads

    q = _silu(_dot(hb, wq_ref[...]))
    for h in range(n_heads):
        q_ref[0, h] = q[:, h * dk:(h + 1) * dk].astype(BF16)

    lbp = lb_ref[...]
    e = jnp.exp(lbp - jnp.max(lbp, axis=0))
    lb = jnp.sum(e[:layer + 1], axis=0) / jnp.sum(e, axis=0)

    zf = _dot(hb, wf_ref[...])
    for d in range(2):
        lbd = lb[d:d + 1]
        f = lbd + (1.0 - lbd) * jax.nn.sigmoid(zf[:, d * fdim:(d + 1) * fdim])
        lf = jnp.log(f)
        for h in range(n_heads):
            lf_ref[d, 0, h] = lf[:, h * dk:(h + 1) * dk]

    v = _dot(hb, wi_ref[...])
    for h in range(n_heads):
        v_ref[0, h] = v[:, h * dv:(h + 1) * dv].astype(BF16)

    gate_ref[0] = _silu(_dot(hb, wg_ref[...])).astype(BF16)


def _hg_inproj(x_all, modsel, g, hg_lb, w_in, *, layer, nc, tm):
    b, t, d = x_all.shape
    fdim = hg_lb.shape[-1]
    n_heads = fdim // HEAD_DK
    dv = d // n_heads
    wq = w_in[:, :fdim].astype(BF16)
    wf = w_in[:, fdim:3 * fdim].astype(BF16)
    wi = w_in[:, 3 * fdim:3 * fdim + d].astype(BF16)
    wg = w_in[:, 3 * fdim + d:].astype(BF16)
    full = lambda a: pl.BlockSpec(a.shape, lambda i, j: (0,) * a.ndim)
    g2 = g.reshape(1, d)
    return pl.pallas_call(
        functools.partial(_hg_inproj_kernel, layer=layer, n_heads=n_heads),
        grid=(b, t // tm),
        in_specs=[pl.BlockSpec((1, tm, d), lambda i, j: (i, j, 0)),
                  pl.BlockSpec((1, 1, 6, d), lambda i, j: (i, jnp.where(j >= nc, 1, 0), 0, 0)),
                  full(g2), full(hg_lb), full(wq), full(wf), full(wi), full(wg)],
        out_specs=[pl.BlockSpec((1, n_heads, tm, HEAD_DK), lambda i, j: (i, 0, j, 0)),
                   pl.BlockSpec((2, 1, n_heads, tm, HEAD_DK), lambda i, j: (0, i, 0, j, 0)),
                   pl.BlockSpec((1, n_heads, tm, dv), lambda i, j: (i, 0, j, 0)),
                   pl.BlockSpec((1, tm, d), lambda i, j: (i, j, 0))],
        out_shape=[jax.ShapeDtypeStruct((b, n_heads, t, HEAD_DK), BF16),
                   jax.ShapeDtypeStruct((2, b, n_heads, t, HEAD_DK), F32),
                   jax.ShapeDtypeStruct((b, n_heads, t, dv), BF16),
                   jax.ShapeDtypeStruct((b, t, d), BF16)],
        compiler_params=_cp("parallel", "parallel"),
        name="hg_inproj",
    )(x_all, modsel, g2, hg_lb, wq, wf, wi, wg)


def _scan_block_index(step, nc, nl, rev):
    if not rev:
        return step
    return jnp.where(step < nc, nc - 1 - step, nc + nl - 1 - (step - nc))


def _chunk_masks(rev):
    r = _iota((CHUNK, CHUNK), 0)
    c = _iota((CHUNK, CHUNK), 1)
    incl = (c >= r) if rev else (c <= r)
    strict = (c > r) if rev else (c < r)
    return incl, strict


def _gla_scan_kernel(q_ref, lf_ref, v_ref, o_ref, st_ref, oin_ref, qc_ref, kh_ref,
                     *, rev, n_heads, nck, tm):
    @pl.when(pl.program_id(1) == 0)
    def _():
        st_ref[...] = jnp.zeros(st_ref.shape, F32)

    dk = q_ref.shape[-1]
    r = _iota((tm, tm), 0)
    c = _iota((tm, tm), 1)
    same_chunk = (r // CHUNK) == (c // CHUNK)
    causal = (c >= r) if rev else (c <= r)
    lmb = jnp.where(same_chunk & causal, 1.0, 0.0).astype(BF16)
    mask_d = ((r // HALF) == (c // HALF)) & causal
    second = 0 if rev else 1
    mask_x = same_chunk & ((r // HALF) % 2 == second) & ((c // HALF) % 2 == 1 - second)

    def per_rows(bh, row_of, span):
        return jnp.concatenate([jnp.broadcast_to(bh[row_of(x0):row_of(x0) + 1], (span, dk))
                                for x0 in range(0, tm, span)], axis=0)

    heads = range(n_heads)
    bhs = []
    for h in heads:
        g3 = _dot(lmb, jnp.concatenate(_split3(lf_ref[0, 0, h]), axis=1))
        bhs.append(g3[:, :dk] + g3[:, dk:2 * dk] + g3[:, 2 * dk:])
    dec, atts = {}, []
    for h in heads:
        bh = bhs[h]
        q = q_ref[0, h].astype(F32)
        k = 1.0 - jnp.exp(lf_ref[0, 0, h])
        bls = [bh[ci * CHUNK:ci * CHUNK + 1] if rev
               else bh[(ci + 1) * CHUNK - 1:(ci + 1) * CHUNK] for ci in range(nck)]
        blc = jnp.concatenate([jnp.broadcast_to(x, (CHUNK, dk)) for x in bls], axis=0)
        qc_ref[h] = (q * jnp.exp(bh)).astype(BF16)
        kh_ref[h] = (k * jnp.exp(blc - bh)).astype(BF16)
        dec[h] = [jnp.exp(x) for x in bls]
        bm = per_rows(bh, lambda x0: x0 + HALF // 2, HALF)
        att_d = _dot_nt((q * jnp.exp(jnp.minimum(bh - bm, EXP_CLAMP))).astype(BF16),
                        (k * jnp.exp(jnp.minimum(bm - bh, EXP_CLAMP))).astype(BF16))
        be = per_rows(bh, lambda x0: x0 + (HALF if rev else HALF - 1), CHUNK)
        att_x = _dot_nt((q * jnp.exp(jnp.minimum(bh - be, EXP_CLAMP))).astype(BF16),
                        (k * jnp.exp(jnp.minimum(be - bh, EXP_CLAMP))).astype(BF16))
        atts.append((att_d, att_x))
    for h in heads:
        att = jnp.where(mask_d, atts[h][0], 0.0) + jnp.where(mask_x, atts[h][1], 0.0)
        oin_ref[h] = _dot(att.astype(BF16), v_ref[0, h])

    for step in range(nck):
        ci = nck - 1 - step if rev else step
        rows = pl.ds(ci * CHUNK, CHUNK)
        for h in heads:
            st = st_ref[h]
            o = oin_ref[h, rows, :] + _dot_nt(qc_ref[h, rows, :], st.astype(BF16))
            o_ref[0, h, rows, :] = o.astype(o_ref.dtype)
            st_ref[h] = st * dec[h][ci] + _dot_tn(v_ref[0, h, rows, :], kh_ref[h, rows, :])


def _gla_scan(q, lf, v, *, rev, nc, tm, out_dtype):
    b, n_heads, t, dk = q.shape
    dv = v.shape[-1]
    nblk = t // tm
    nl = nblk - nc
    nck = tm // CHUNK
    d = 1 if rev else 0
    blk = lambda i, s: _scan_block_index(s, nc, nl, rev)
    return pl.pallas_call(
        functools.partial(_gla_scan_kernel, rev=rev, n_heads=n_heads, nck=nck, tm=tm),
        grid=(b, nblk),
        in_specs=[pl.BlockSpec((1, n_heads, tm, dk), lambda i, s: (i, 0, blk(i, s), 0)),
                  pl.BlockSpec((1, 1, n_heads, tm, dk), lambda i, s: (d, i, 0, blk(i, s), 0)),
                  pl.BlockSpec((1, n_heads, tm, dv), lambda i, s: (i, 0, blk(i, s), 0))],
        out_specs=pl.BlockSpec((1, n_heads, tm, dv), lambda i, s: (i, 0, blk(i, s), 0)),
        out_shape=jax.ShapeDtypeStruct((b, n_heads, t, dv), out_dtype),
        scratch_shapes=[pltpu.VMEM((n_heads, dv, dk), F32),
                        pltpu.VMEM((n_heads, tm, dv), F32),
                        pltpu.VMEM((n_heads, tm, dk), BF16),
                        pltpu.VMEM((n_heads, tm, dk), BF16)],
        compiler_params=_cp("parallel", "arbitrary"),
        name="gla_scan_bwd" if rev else "gla_scan_fwd",
    )(q, lf, v)


def _outproj_kernel(of_ref, ob_ref, gate_ref, gain_ref, w_ref, x_ref, mod_ref, o_ref, y_ref,
                    *, n_heads, rpc):
    dv = of_ref.shape[-1]
    for h in range(n_heads):
        o = of_ref[0, h].astype(F32) + ob_ref[0, h].astype(F32)
        cs = slice(h * dv, (h + 1) * dv)
        o = o * lax.rsqrt(jnp.mean(o * o, axis=-1, keepdims=True) + EPS) * gain_ref[:, cs]
        y_ref[:, cs] = (o * gate_ref[0, :, cs].astype(F32)).astype(BF16)
    y = mod_ref[0, 0][2:3] * _dot(y_ref[...], w_ref[...])
    if not rpc:
        o_ref[0] = x_ref[0] + y
    else:
        d = y.shape[1]
        for j in range(y.shape[0] // rpc):
            o_ref[0, :, j * d:(j + 1) * d] = x_ref[0, :, j * d:(j + 1) * d] + y[j * rpc:(j + 1) * rpc]


def _outproj(o_f, o_b, gate, gain, w_out, x_res, modsel, *, nc, tm, rpc=0):
    b, n_heads, t, dv = o_f.shape
    d = w_out.shape[-1]
    hv = n_heads * dv
    off = nc if rpc else 0
    nblk = t // tm - off
    seg = (lambda j: 1) if rpc else (lambda j: jnp.where(j >= nc, 1, 0))
    w = w_out.astype(BF16)
    gain2 = gain.reshape(1, hv)
    if rpc:
        x_spec = pl.BlockSpec((1, rpc, tm // rpc * d), lambda i, j: (i, 0, j))
    else:
        x_spec = pl.BlockSpec((1, tm, d), lambda i, j: (i, j, 0))
    return pl.pallas_call(
        functools.partial(_outproj_kernel, n_heads=n_heads, rpc=rpc),
        grid=(b, nblk),
        in_specs=[pl.BlockSpec((1, n_heads, tm, dv), lambda i, j: (i, 0, j + off, 0)),
                  pl.BlockSpec((1, n_heads, tm, dv), lambda i, j: (i, 0, j + off, 0)),
                  pl.BlockSpec((1, tm, hv), lambda i, j: (i, j + off, 0)),
                  pl.BlockSpec((1, hv), lambda i, j: (0, 0)),
                  pl.BlockSpec((hv, d), lambda i, j: (0, 0)),
                  x_spec,
                  pl.BlockSpec((1, 1, 6, d), lambda i, j: (i, seg(j), 0, 0))],
        out_specs=x_spec,
        out_shape=jax.ShapeDtypeStruct(x_res.shape, F32),
        scratch_shapes=[pltpu.VMEM((tm, hv), BF16)],
        compiler_params=_cp("parallel", "parallel"),
        name="outproj",
    )(o_f, o_b, gate, gain2, w, x_res, modsel)


def _gd_inproj_kernel(cp_ref, cc_ref, cn_ref, lp_ref, lc_ref, ln_ref, mod_ref, g_ref, wqkv_ref,
                      wg_ref, wab_ref, cw_ref, alog_ref, dtb_ref, q_ref, k_ref, v_ref, gate_ref,
                      ab_ref, xs_ref, ys_ref, *, nc, nblk, n_heads, tm, cw, rpc):
    t = pl.program_id(1)
    first = (t == 0) | (t == nc)
    last = (t == nc - 1) | (t == nblk - 1)
    m = mod_ref[0, 0]
    halo = SUBLANES
    nlt = xs_ref.shape[0]
    d = nlt * LANES
    tile = lambda c: slice(c * LANES, (c + 1) * LANES)

    @pl.when(t < nc)
    def _():
        for c in range(nlt):
            xs_ref[c, 0:halo] = cp_ref[0, :, tile(c)]
            xs_ref[c, halo:halo + tm] = cc_ref[0, :, tile(c)]
            xs_ref[c, halo + tm:] = cn_ref[0, :, tile(c)]

    @pl.when(t >= nc)
    def _():
        for c in range(nlt):
            xs_ref[c, 0:halo] = lp_ref[0, :, tile(c)]
            for j in range(tm // rpc):
                xs_ref[c, halo + j * rpc:halo + (j + 1) * rpc] = (
                    lc_ref[0, :, j * d + c * LANES:j * d + (c + 1) * LANES])
            xs_ref[c, halo + tm:] = ln_ref[0, :, tile(c)]

    xc = jnp.concatenate([xs_ref[c, halo:halo + tm] for c in range(nlt)], axis=1)
    hc = _norm_mod(xc, g_ref[...], m[0:1], m[1:2]).astype(BF16)

    rows = tm + 2 * halo
    ni = rows // SUBLANES
    xp = jnp.concatenate(
        [jnp.concatenate([xs_ref[c, pl.ds(i, SUBLANES, stride=ni), :] for i in range(ni)], axis=0)
         for c in range(nlt)], axis=1)
    he = _norm_mod(xp, g_ref[...], m[0:1], m[1:2]).astype(BF16)
    row = _iota((rows, 1), 0)
    tok = ni * (row % SUBLANES) + row // SUBLANES
    valid = ((tok >= halo) | jnp.logical_not(first)) & ((tok < tm + halo) | jnp.logical_not(last))

    gate_ref[0] = _silu(_dot(hc, wg_ref[...])).astype(BF16)

    zab = _dot(hc, wab_ref[...])
    lane = _iota(zab.shape, 1)
    loga = -jnp.exp(alog_ref[...]) * jax.nn.softplus(zab + dtb_ref[...])
    ab_ref[0] = jnp.where(lane < 2 * n_heads, loga, jax.nn.sigmoid(zab))

    qk = n_heads * HEAD_DK
    nchan = wqkv_ref.shape[1]
    pad = GD_CONV_W // 2
    for cc in range(nchan // cw):
        c0 = cc * cw
        z = jnp.where(valid, _dot(he, wqkv_ref[:, c0:c0 + cw]), 0.0)
        z3 = z.reshape(ni, SUBLANES, cw)
        zz = jnp.concatenate([pltpu.roll(z3[ni - pad:], 1, 1), z3,
                              pltpu.roll(z3[:pad], SUBLANES - 1, 1)], axis=0)
        acc = cw_ref[0:1, c0:c0 + cw] * zz[0:ni]
        for j in range(1, GD_CONV_W):
            acc = acc + cw_ref[j:j + 1, c0:c0 + cw] * zz[j:j + ni]
        u = _silu(acc)
        if c0 < 2 * qk:
            scale = HEAD_DK ** -0.5 if c0 < qk else 1.0
            parts = []
            for j in range(cw // HEAD_DK):
                tt = u[:, :, j * HEAD_DK:(j + 1) * HEAD_DK]
                parts.append(tt * (lax.rsqrt(jnp.sum(tt * tt, axis=-1, keepdims=True) + EPS)
                                   * scale))
            u = jnp.concatenate(parts, axis=-1)
        for c in range(cw // LANES):
            for i in range(ni):
                ys_ref[c, pl.ds(i, SUBLANES, stride=ni), :] = u[i][:, tile(c)]

        def centre(lane0, width):
            return jnp.concatenate([ys_ref[c, halo:halo + tm] for c in
                                    range(lane0 // LANES, (lane0 + width) // LANES)], axis=1)

        if c0 < 2 * qk:
            dst, base = (q_ref, c0) if c0 < qk else (k_ref, c0 - qk)
            for j in range(cw // HEAD_DK):
                dst[0, base // HEAD_DK + j] = centre(j * HEAD_DK, HEAD_DK).astype(BF16)
        else:
            base = c0 - 2 * qk
            for j in range(cw // GD_DV):
                v_ref[0, base // GD_DV + j] = centre(j * GD_DV, GD_DV).astype(BF16)


def _gd_inproj(x_ctx, x_lat, modsel, g, w_in, conv_w, a_log, dt_bias, *, tm):
    b, n_ctx, d = x_ctx.shape
    n_lat = x_lat.shape[1]
    rpc = n_lat // GRID_W
    assert rpc % SUBLANES == 0 and tm % rpc == 0 and n_ctx % tm == 0 and n_ctx > 0
    t = n_ctx + n_lat
    nc = n_ctx // tm
    cpb = tm // rpc
    lat_v = x_lat.reshape(b, rpc, GRID_W * d)
    n_heads = a_log.shape[-1]
    qk = n_heads * HEAD_DK
    vd = n_heads * GD_DV
    nchan = 2 * qk + vd
    nblk = t // tm
    cw = min(512, qk)
    wqkv = w_in[:, :nchan].astype(BF16)
    wg = w_in[:, nchan:nchan + vd].astype(BF16)
    wab = jnp.pad(w_in[:, nchan + vd:], ((0, 0), (0, LANES - 4 * n_heads))).astype(BF16)
    alog = jnp.pad(a_log.reshape(1, 2 * n_heads), ((0, 0), (0, LANES - 2 * n_heads)))
    dtb = jnp.pad(dt_bias.reshape(1, 2 * n_heads), ((0, 0), (0, LANES - 2 * n_heads)))
    g2 = g.reshape(1, d)
    full = lambda a: pl.BlockSpec(a.shape, lambda i, j: (0,) * a.ndim)
    spb = tm // SUBLANES
    last_slab = n_ctx // SUBLANES - 1
    last_col = GRID_W - 1
    return pl.pallas_call(
        functools.partial(_gd_inproj_kernel, nc=nc, nblk=nblk, n_heads=n_heads, tm=tm, cw=cw,
                          rpc=rpc),
        grid=(b, nblk),
        in_specs=[pl.BlockSpec((1, SUBLANES, d),
                               lambda i, j: (i, jnp.clip(j * spb - 1, 0, last_slab), 0)),
                  pl.BlockSpec((1, tm, d), lambda i, j: (i, jnp.minimum(j, nc - 1), 0)),
                  pl.BlockSpec((1, SUBLANES, d),
                               lambda i, j: (i, jnp.clip((j + 1) * spb, 0, last_slab), 0)),
                  pl.BlockSpec((1, SUBLANES, d),
                               lambda i, j: (i, rpc // SUBLANES - 1,
                                             jnp.clip((j - nc) * cpb - 1, 0, last_col))),
                  pl.BlockSpec((1, rpc, cpb * d), lambda i, j: (i, 0, jnp.maximum(j - nc, 0))),
                  pl.BlockSpec((1, SUBLANES, d),
                               lambda i, j: (i, 0, jnp.clip((j - nc + 1) * cpb, 0, last_col))),
                  pl.BlockSpec((1, 1, 6, d), lambda i, j: (i, jnp.where(j >= nc, 1, 0), 0, 0)),
                  full(g2), full(wqkv), full(wg), full(wab), full(conv_w), full(alog), full(dtb)],
        out_specs=[pl.BlockSpec((1, n_heads, tm, HEAD_DK), lambda i, j: (i, 0, j, 0)),
                   pl.BlockSpec((1, n_heads, tm, HEAD_DK), lambda i, j: (i, 0, j, 0)),
                   pl.BlockSpec((1, n_heads, tm, GD_DV), lambda i, j: (i, 0, j, 0)),
                   pl.BlockSpec((1, tm, vd), lambda i, j: (i, j, 0)),
                   pl.BlockSpec((1, tm, LANES), lambda i, j: (i, j, 0))],
        out_shape=[jax.ShapeDtypeStruct((b, n_heads, t, HEAD_DK), BF16),
                   jax.ShapeDtypeStruct((b, n_heads, t, HEAD_DK), BF16),
                   jax.ShapeDtypeStruct((b, n_heads, t, GD_DV), BF16),
                   jax.ShapeDtypeStruct((b, t, vd), BF16),
                   jax.ShapeDtypeStruct((b, t, LANES), F32)],
        scratch_shapes=[pltpu.VMEM((d // LANES, tm + 2 * SUBLANES, LANES), F32),
                        pltpu.VMEM((cw // LANES, tm + 2 * SUBLANES, LANES), F32)],
        compiler_params=_cp("parallel", "parallel"),
        name="gd_inproj",
    )(x_ctx, x_ctx, x_ctx, lat_v, lat_v, lat_v, modsel, g2, wqkv, wg, wab, conv_w, alog, dtb)


def _neumann_inverses(mats, eye):
    n = eye.shape[0]
    ts = [eye - a for a in mats]
    ps = []
    for a in mats:
        ab = a.astype(BF16)
        ps.append(_dot(ab, ab))
    lvl = 2
    while lvl < CHUNK:
        for i in range(len(mats)):
            pb = ps[i].astype(BF16)
            tb = ts[i].astype(BF16)
            if 2 * lvl >= CHUNK:
                ts[i] = ts[i] + _dot(pb, tb)
            else:
                out = _dot(pb, jnp.concatenate([pb, tb], axis=1))
                ps[i] = out[:, :n]
                ts[i] = ts[i] + out[:, n:]
        lvl *= 2
    return ts


def _gdn_scan_block(q_ref, k_ref, v_ref, ab_ref, o_ref, s_ref, u_ref, l1_ref, l2_ref,
                    *, rev, n_heads, nck, tm):
    d = 1 if rev else 0

    def masks(n):
        r = _iota((n, n), 0)
        c = _iota((n, n), 1)
        same = (r // CHUNK) == (c // CHUNK)
        return (same & ((c >= r) if rev else (c <= r)), same & ((c > r) if rev else (c < r)),
                jnp.where(r == c, 1.0, 0.0))

    pb = min(tm, GD_SOLVE_ROWS)
    incl, strict, eye = masks(pb)
    dv = v_ref.shape[-1]

    ab = ab_ref[0]
    g3 = _dot(jnp.where(masks(tm)[0], 1.0, 0.0).astype(BF16),
              jnp.concatenate(_split3(ab), axis=1))
    gcol = g3[:, :LANES] + g3[:, LANES:2 * LANES] + g3[:, 2 * LANES:]
    eye_l = jnp.where(_iota((LANES, LANES), 0) == _iota((LANES, LANES), 1), 1.0, 0.0).astype(BF16)
    gr3 = _dot_nt(eye_l, jnp.concatenate(_split3(gcol), axis=0))
    grow = gr3[:, :tm] + gr3[:, tm:2 * tm] + gr3[:, 2 * tm:]

    incl_c, _ = _chunk_masks(rev)
    wide = lambda x, n: jnp.concatenate([x] * (n // LANES), axis=1) if n > LANES else x[:, :n]
    heads = range(n_heads)
    cas = [d * n_heads + h for h in heads]
    gcbs = [jnp.broadcast_to(gcol[:, ca:ca + 1], (tm, LANES)) for ca in cas]
    bcbs = [jnp.broadcast_to(ab[:, 2 * n_heads + ca:2 * n_heads + ca + 1], (tm, LANES))
            for ca in cas]

    mats = []
    for h in heads:
        k = k_ref[0, h]
        for p0 in range(0, tm, pb):
            gam = jnp.where(incl, jnp.exp(jnp.minimum(
                wide(gcbs[h][p0:p0 + pb], pb) - grow[cas[h]:cas[h] + 1, p0:p0 + pb], 0.0)), 0.0)
            kp = k[p0:p0 + pb]
            mats.append(jnp.where(strict, wide(bcbs[h][p0:p0 + pb], pb) * _dot_nt(kp, kp) * gam,
                                  0.0))
    tinvs = _neumann_inverses(mats, eye)

    uws = []
    for h in heads:
        kf = k_ref[0, h].astype(F32)
        rhs = jnp.concatenate(
            [(wide(bcbs[h], dv) * v_ref[0, h].astype(F32)).astype(BF16),
             (bcbs[h] * jnp.exp(gcbs[h]) * kf).astype(BF16)], axis=1)
        uws.append(jnp.concatenate(
            [_dot(tinvs[h * (tm // pb) + i].astype(BF16), rhs[i * pb:(i + 1) * pb])
             for i in range(tm // pb)], axis=0))

    egl = {}
    for h in heads:
        ca, gcb, uw = cas[h], gcbs[h], uws[h]
        q = q_ref[0, h]
        k = k_ref[0, h]
        kf = k.astype(F32)
        egb = jnp.exp(gcb)
        u_ref[h] = uw[:, :dv]
        wb = uw[:, dv:].astype(BF16)
        qg = (q.astype(F32) * egb).astype(BF16)
        gls = [gcb[ci * CHUNK:ci * CHUNK + 1] if rev
               else gcb[(ci + 1) * CHUNK - 1:(ci + 1) * CHUNK] for ci in range(nck)]
        glb = jnp.concatenate([jnp.broadcast_to(g, (CHUNK, LANES)) for g in gls], axis=0)
        kd = kf * jnp.exp(glb - gcb)
        egl[h] = [jnp.exp(g[:, 0:1]) for g in gls]
        for ci in range(nck):
            c0 = ci * CHUNK
            l1_ref[h, ci, :CHUNK] = wb[c0:c0 + CHUNK]
            l1_ref[h, ci, CHUNK:] = qg[c0:c0 + CHUNK]
            gam_c = jnp.where(incl_c, jnp.exp(jnp.minimum(
                gcb[c0:c0 + CHUNK, :CHUNK] - grow[ca:ca + 1, c0:c0 + CHUNK], 0.0)), 0.0)
            qk_c = _dot_nt(q[c0:c0 + CHUNK], k[c0:c0 + CHUNK])
            l2_ref[h, ci, :CHUNK] = (qk_c * gam_c).astype(BF16)
            l2_ref[h, ci, CHUNK:] = kd[c0:c0 + CHUNK].T.astype(BF16)

    for step in range(nck):
        ci = nck - 1 - step if rev else step
        rows = pl.ds(ci * CHUNK, CHUNK)
        r1s = [_dot(l1_ref[h, ci], s_ref[h].astype(BF16)) for h in heads]
        weffs = [(u_ref[h, rows, :] - r1s[h][:CHUNK]).astype(BF16) for h in heads]
        r2s = [_dot(l2_ref[h, ci], weffs[h]) for h in heads]
        for h in heads:
            o_ref[0, h, rows, :] = (r2s[h][:CHUNK] + r1s[h][CHUNK:]).astype(o_ref.dtype)
            s_ref[h] = s_ref[h] * egl[h][ci] + r2s[h][CHUNK:]


def _gdn_scan_kernel(*refs, n_heads, nck, tm):
    fwd = refs[0:4] + refs[8:9] + refs[10:14]
    bwd = refs[4:8] + refs[9:10] + refs[14:18]

    @pl.when(pl.program_id(1) == 0)
    def _():
        for s_ref in (fwd[5], bwd[5]):
            s_ref[...] = jnp.zeros(s_ref.shape, F32)

    _gdn_scan_block(*fwd, rev=False, n_heads=n_heads, nck=nck, tm=tm)
    _gdn_scan_block(*bwd, rev=True, n_heads=n_heads, nck=nck, tm=tm)


def _gdn_scan(q, k, v, ab, *, nc, tm, out_dtype):
    b, n_heads, t, dk = q.shape
    dv = v.shape[-1]
    nblk = t // tm
    nl = nblk - nc
    nck = tm // CHUNK

    def specs(rev):
        blk = lambda s: _scan_block_index(s, nc, nl, rev)
        return [pl.BlockSpec((1, n_heads, tm, dk), lambda i, s: (i, 0, blk(s), 0)),
                pl.BlockSpec((1, n_heads, tm, dk), lambda i, s: (i, 0, blk(s), 0)),
                pl.BlockSpec((1, n_heads, tm, dv), lambda i, s: (i, 0, blk(s), 0)),
                pl.BlockSpec((1, tm, LANES), lambda i, s: (i, blk(s), 0))]

    scratch = [pltpu.VMEM((n_heads, dk, dv), F32),
               pltpu.VMEM((n_heads, tm, dv), F32),
               pltpu.VMEM((n_heads, nck, 2 * CHUNK, dk), BF16),
               pltpu.VMEM((n_heads, nck, CHUNK + dk, CHUNK), BF16)]
    o_shape = jax.ShapeDtypeStruct((b, n_heads, t, dv), out_dtype)
    return pl.pallas_call(
        functools.partial(_gdn_scan_kernel, n_heads=n_heads, nck=nck, tm=tm),
        grid=(b, nblk),
        in_specs=specs(False) + specs(True),
        out_specs=[specs(False)[2], specs(True)[2]],
        out_shape=[o_shape, o_shape],
        scratch_shapes=scratch + scratch,
        compiler_params=_cp("parallel", "arbitrary"),
        name="gdn_scan",
    )(q, k, v, ab, q, k, v, ab)


def _route_kernel(x_ref, mod_ref, g_ref, wrt_ref, h_ref, slot_ref, aff_ref, lg_ref,
                  *, n_ctx, cap_c, cap_l, tr):
    n_exp, t = lg_ref.shape
    for rt in range(t // tr):
        m = mod_ref[0, 0 if rt * tr < n_ctx else 1]
        rows = slice(rt * tr, (rt + 1) * tr)
        h = _norm_mod(x_ref[0, rows, :], g_ref[...], m[3:4], m[4:5])
        h_ref[0, rows, :] = h.astype(BF16)
        lg_ref[:, rows] = _dot_nt_hi(wrt_ref[...], h)
    lg = lg_ref[...]
    e = jnp.exp(lg - jnp.max(lg, axis=0, keepdims=True))
    aff = e / jnp.sum(e, axis=0, keepdims=True)
    aff_ref[0] = aff
    bits = lax.bitcast_convert_type(aff, jnp.int32)
    lane = _iota((n_exp, t), 1)

    if n_ctx:
        is_ctx = lane < n_ctx
        regions = [(is_ctx, cap_c), (jnp.logical_not(is_ctx), cap_l)]
    else:
        regions = [(None, cap_l)]

    def count(pred, mask):
        p = pred if mask is None else (pred & mask)
        return jnp.sum(jnp.where(p, 1.0, 0.0), axis=1, keepdims=True)

    def thr_body(i, thrs):
        bit = lax.shift_left(jnp.int32(1), AFF_BITS - 1 - i)
        out = []
        for (mask, cap), thr in zip(regions, thrs):
            cand = thr | bit
            out.append(jnp.where(count(bits >= cand, mask) >= cap, cand, thr))
        return tuple(out)

    zero = jnp.zeros((n_exp, 1), jnp.int32)
    thrs = lax.fori_loop(0, AFF_BITS, thr_body, tuple(zero for _ in regions))

    idx_bits = t.bit_length()
    sel = None
    for (mask, cap), thr in zip(regions, thrs):
        gt = bits > thr
        tie = bits == thr
        need = cap - count(gt, mask)

        def j_body(i, j, tie=tie, mask=mask, need=need):
            cand = j | lax.shift_left(jnp.int32(1), idx_bits - 1 - i)
            return jnp.where(count(tie & (lane < cand), mask) <= need, cand, j)

        jmax = lax.fori_loop(0, idx_bits, j_body, zero)
        s = gt | (tie & (lane < jmax))
        if mask is not None:
            s = s & mask
        sel = s if sel is None else (sel | s)

    lt = LANES if t % LANES == 0 else CHUNK
    ut = jnp.where(_iota((lt, lt), 0) < _iota((lt, lt), 1), 1.0, 0.0).astype(BF16)
    self = jnp.where(sel, 1.0, 0.0)
    run = jnp.zeros((n_exp, 1), F32)
    pres = []
    for i in range(t // lt):
        tile = self[:, i * lt:(i + 1) * lt]
        pres.append(_dot(tile.astype(BF16), ut) + run)
        run = run + jnp.sum(tile, axis=1, keepdims=True)
    pre = jnp.concatenate(pres, axis=1)
    if n_ctx:
        slot = jnp.where(is_ctx, pre + cap_l, pre - cap_c)
    else:
        slot = pre
    slot_ref[0] = jnp.where(sel, slot, -1.0)


def _route(x, modsel, g, w_router, *, n_ctx, cap_c, cap_l, tr):
    b, t, d = x.shape
    n_exp = w_router.shape[-1]
    wrt = w_router.T
    g2 = g.reshape(1, d)
    return pl.pallas_call(
        functools.partial(_route_kernel, n_ctx=n_ctx, cap_c=cap_c, cap_l=cap_l, tr=tr),
        grid=(b,),
        in_specs=[pl.BlockSpec((1, t, d), lambda i: (i, 0, 0)),
                  pl.BlockSpec((1, 2, 6, d), lambda i: (i, 0, 0, 0)),
                  pl.BlockSpec((1, d), lambda i: (0, 0)),
                  pl.BlockSpec((n_exp, d), lambda i: (0, 0))],
        out_specs=[pl.BlockSpec((1, t, d), lambda i: (i, 0, 0)),
                   pl.BlockSpec((1, n_exp, t), lambda i: (i, 0, 0)),
                   pl.BlockSpec((1, n_exp, t), lambda i: (i, 0, 0))],
        out_shape=[jax.ShapeDtypeStruct((b, t, d), BF16),
                   jax.ShapeDtypeStruct((b, n_exp, t), F32),
                   jax.ShapeDtypeStruct((b, n_exp, t), F32)],
        scratch_shapes=[pltpu.VMEM((n_exp, t), F32)],
        compiler_params=_cp("parallel"),
        name="moe_route",
    )(x, modsel, g2, wrt)


def _gather_kernel(slot_ref, h_ref, x_ref, *, n_ctx, cap_c, cap_l):
    slot = slot_ref[0, 0]
    t = slot.shape[1]
    sl = slot[:, n_ctx:]
    p = jnp.where(sl == _iota((cap_l, t - n_ctx), 0).astype(F32), 1.0, 0.0).astype(BF16)
    x_ref[0, 0, 0:cap_l, :] = _dot(p, h_ref[0, n_ctx:, :]).astype(BF16)
    if n_ctx:
        sc = slot[:, :n_ctx] - float(cap_l)
        p = jnp.where(sc == _iota((cap_c, n_ctx), 0).astype(F32), 1.0, 0.0).astype(BF16)
        x_ref[0, 0, cap_l:, :] = _dot(p, h_ref[0, :n_ctx, :]).astype(BF16)


def _gather(h, slot_row, *, n_ctx, cap_c, cap_l):
    b, t, d = h.shape
    n_exp = slot_row.shape[1]
    m = cap_l + (cap_c if n_ctx else 0)
    return pl.pallas_call(
        functools.partial(_gather_kernel, n_ctx=n_ctx, cap_c=cap_c, cap_l=cap_l),
        grid=(b, n_exp),
        in_specs=[pl.BlockSpec((1, 1, 1, t), lambda i, e: (i, e, 0, 0)),
                  pl.BlockSpec((1, t, d), lambda i, e: (i, 0, 0))],
        out_specs=pl.BlockSpec((1, 1, m, d), lambda i, e: (e, i, 0, 0)),
        out_shape=jax.ShapeDtypeStruct((n_exp, b, m, d), BF16),
        compiler_params=_cp("parallel", "parallel"),
        name="moe_gather",
    )(slot_row.reshape(b, n_exp, 1, t), h)


def _ffn_kernel(x_ref, wg_ref, wu_ref, wd_ref, y_ref, acc_ref, wgb_ref, wub_ref, wdb_ref, *, tr):
    ft = pl.program_id(1)

    @pl.when(ft == 0)
    def _():
        acc_ref[...] = jnp.zeros(acc_ref.shape, F32)

    wgb_ref[...] = wg_ref[...].astype(BF16)
    wub_ref[...] = wu_ref[...].astype(BF16)
    wdb_ref[...] = wd_ref[...].astype(BF16)

    def body(i, carry):
        rows = pl.ds(pl.multiple_of(i * tr, tr), tr)
        x = x_ref[0, rows, :]
        hid = (_silu(_dot(x, wgb_ref[...])) * _dot(x, wub_ref[...])).astype(BF16)
        acc_ref[rows, :] += _dot(hid, wdb_ref[...])
        return carry

    lax.fori_loop(0, x_ref.shape[1] // tr, body, 0)

    @pl.when(ft == pl.num_programs(1) - 1)
    def _():
        y_ref[0] = acc_ref[...].astype(y_ref.dtype)


def _ffn(xs, w_gate, w_up, w_down, *, layer):
    n_exp, bm, d = xs.shape
    f = w_gate.shape[-1]
    tf = FFN_TILE if f % FFN_TILE == 0 else f
    tr = bm // (-(-bm // FFN_MAX_ROWS))
    return pl.pallas_call(
        functools.partial(_ffn_kernel, tr=tr),
        grid=(n_exp, f // tf),
        in_specs=[pl.BlockSpec((1, bm, d), lambda e, j: (e, 0, 0)),
                  pl.BlockSpec((None, None, d, tf), lambda e, j: (layer, e, 0, j)),
                  pl.BlockSpec((None, None, d, tf), lambda e, j: (layer, e, 0, j)),
                  pl.BlockSpec((None, None, tf, d), lambda e, j: (layer, e, j, 0))],
        out_specs=pl.BlockSpec((1, bm, d), lambda e, j: (e, 0, 0)),
        out_shape=jax.ShapeDtypeStruct((n_exp, bm, d), BF16),
        scratch_shapes=[pltpu.VMEM((bm, d), F32),
                        pltpu.VMEM((d, tf), BF16),
                        pltpu.VMEM((d, tf), BF16),
                        pltpu.VMEM((tf, d), BF16)],
        compiler_params=_cp("parallel", "arbitrary"),
        name="moe_ffn",
    )(xs, w_gate, w_up, w_down)


def _combine_kernel(*refs, nc, cap_c, cap_l, final):
    x_ref, ys_ref, sc_ref, ac_ref, mod_ref = refs[:5]
    fg_ref = refs[5] if final else None
    outs = refs[5 + bool(final):-1]
    acc_ref = refs[-1]
    n_exp = ys_ref.shape[0]
    tm = x_ref.shape[1]

    def run(k0, kk, o_ref):
        sc = sc_ref[0]
        ac = ac_ref[0]
        lane = (_iota((tm, kk), 1) + k0).astype(F32)
        acc_ref[...] = jnp.zeros(acc_ref.shape, F32)
        for e in range(n_exp):
            pt = jnp.where(sc[:, e:e + 1] == lane, 1.0, 0.0).astype(BF16)
            acc_ref[...] += ac[:, e:e + 1] * _dot(pt, ys_ref[e, 0, k0:k0 + kk, :])
        x2 = x_ref[0] + mod_ref[0, 0][5:6] * acc_ref[...]
        if final:
            x2 = x2 * lax.rsqrt(jnp.mean(x2 * x2, axis=-1, keepdims=True) + EPS) * fg_ref[...]
        o_ref[0] = x2

    if nc:
        t = pl.program_id(1)

        @pl.when(t < nc)
        def _():
            run(cap_l, cap_c, outs[0])

        @pl.when(t >= nc)
        def _():
            run(0, cap_l, outs[1])
    else:
        run(0, cap_l, outs[0])


def _combine(x, ys, slot_col, aff_col, modsel, final_g, *, nc, cap_c, cap_l, tm):
    b, t, d = x.shape
    n_exp, _, m, _ = ys.shape
    seg = (lambda j: jnp.where(j >= nc, 1, 0)) if nc else (lambda j: 1)
    if nc:
        out_specs = [pl.BlockSpec((1, tm, d), lambda i, j: (i, jnp.minimum(j, nc - 1), 0)),
                     pl.BlockSpec((1, tm, d), lambda i, j: (i, jnp.maximum(j - nc, 0), 0))]
        out_shape = [jax.ShapeDtypeStruct((b, nc * tm, d), F32),
                     jax.ShapeDtypeStruct((b, t - nc * tm, d), F32)]
    else:
        out_specs = pl.BlockSpec((1, tm, d), lambda i, j: (i, j, 0))
        out_shape = jax.ShapeDtypeStruct((b, t, d), F32)
    in_specs = [pl.BlockSpec((1, tm, d), lambda i, j: (i, j, 0)),
                pl.BlockSpec((n_exp, 1, m, d), lambda i, j: (0, i, 0, 0)),
                pl.BlockSpec((1, tm, n_exp), lambda i, j: (i, j, 0)),
                pl.BlockSpec((1, tm, n_exp), lambda i, j: (i, j, 0)),
                pl.BlockSpec((1, 1, 6, d), lambda i, j: (i, seg(j), 0, 0))]
    args = [x, ys, slot_col, aff_col, modsel]
    if final_g is not None:
        in_specs.append(pl.BlockSpec((1, d), lambda i, j: (0, 0)))
        args.append(final_g.reshape(1, d))
    return pl.pallas_call(
        functools.partial(_combine_kernel, nc=nc, cap_c=cap_c, cap_l=cap_l,
                          final=final_g is not None),
        grid=(b, t // tm),
        in_specs=in_specs,
        out_specs=out_specs,
        out_shape=out_shape,
        scratch_shapes=[pltpu.VMEM((tm, d), F32)],
        compiler_params=_cp("parallel", "arbitrary"),
        name="moe_combine",
    )(*args)


def _moe(x, modsel, g, w_router, w_gate, w_up, w_down, final_g, *, layer, n_ctx, n_lat, tm):
    b, t, d = x.shape
    n_exp = w_router.shape[-1]
    cap_l = EC_CAPACITY_FACTOR * n_lat // n_exp
    cap_c = EC_CAPACITY_FACTOR * n_ctx // n_exp
    h, slot_row, aff_row = _route(x, modsel, g, w_router, n_ctx=n_ctx, cap_c=cap_c, cap_l=cap_l,
                                  tr=tm)
    xs = _gather(h, slot_row, n_ctx=n_ctx, cap_c=cap_c, cap_l=cap_l)
    m = xs.shape[2]
    ys = _ffn(xs.reshape(n_exp, b * m, d), w_gate, w_up, w_down, layer=layer)
    return _combine(x, ys.reshape(n_exp, b, m, d), jnp.swapaxes(slot_row, 1, 2),
                    jnp.swapaxes(aff_row, 1, 2), modsel, final_g,
                    nc=n_ctx // tm, cap_c=cap_c, cap_l=cap_l, tm=tm)


def kernel(x, c, ctx, c_ctx, w_mod, b_mod, norm_g, hg_w_in, hg_lb, hg_onorm, hg_w_out, gd_w_in,
           gd_conv, gd_a_log, gd_dt_bias, gd_onorm, gd_w_out, moe_router, moe_w_gate, moe_w_up,
           moe_w_down, final_g):
    bsz, n_lat, d = x.shape
    n_ctx = ctx.shape[1]
    depth = w_mod.shape[0]
    assert depth == 2, "layer 0 = HGRN2, layer 1 = gated DeltaNet"
    tm = math.gcd(math.gcd(n_ctx, n_lat), MAX_BLOCK_ROWS)
    assert tm % CHUNK == 0
    nc = n_ctx // tm
    o_dt = BF16

    rows = -(-(bsz + 1) // SUBLANES) * SUBLANES
    cond = jnp.zeros((rows, d), F32).at[:bsz].set(c).at[bsz].set(c_ctx)
    mod = _adaln(cond, w_mod, b_mod).reshape(depth, rows, 6, d)

    def modsel(i):
        ctx_mod = jnp.broadcast_to(mod[i, bsz][None], (bsz, 6, d))
        return jnp.stack([ctx_mod, mod[i, :bsz]], axis=1)

    x_all = jnp.concatenate([ctx, x], axis=1)

    ms = modsel(0)
    q, lf, v, gate = _hg_inproj(x_all, ms, norm_g[0, 0], hg_lb, hg_w_in[0], layer=0, nc=nc, tm=tm)
    o_f = _gla_scan(q, lf, v, rev=False, nc=nc, tm=tm, out_dtype=o_dt)
    o_b = _gla_scan(q, lf, v, rev=True, nc=nc, tm=tm, out_dtype=o_dt)
    x_all = _outproj(o_f, o_b, gate, hg_onorm[0], hg_w_out[0], x_all, ms, nc=nc, tm=tm)
    x_ctx, x_lat = _moe(x_all, ms, norm_g[0, 1], moe_router[0], moe_w_gate, moe_w_up, moe_w_down,
                        None, layer=0, n_ctx=n_ctx, n_lat=n_lat, tm=tm)

    ms = modsel(1)
    rpc = n_lat // GRID_W
    q, k, v, gate, ab = _gd_inproj(x_ctx, x_lat, ms, norm_g[1, 0], gd_w_in[0], gd_conv[0],
                                   gd_a_log[0], gd_dt_bias[0], tm=tm)
    o_f, o_b = _gdn_scan(q, k, v, ab, nc=nc, tm=tm, out_dtype=o_dt)
    x_lat = _outproj(o_f, o_b, gate, gd_onorm[0], gd_w_out[0],
                     x_lat.reshape(bsz, rpc, GRID_W * d), ms, nc=nc, tm=tm, rpc=rpc)
    x_lat = x_lat.reshape(bsz, n_lat, d)
    return _moe(x_lat, ms, norm_g[1, 1], moe_router[1], moe_w_gate, moe_w_up, moe_w_down, final_g,
                layer=1, n_ctx=0, n_lat=n_lat, tm=tm)
```

```python
import functools
import math

import jax
import jax.numpy as jnp
from jax import lax
from jax.experimental import pallas as pl
from jax.experimental.pallas import tpu as pltpu

F32 = jnp.float32
BF16 = jnp.bfloat16
HIGHEST = lax.Precision.HIGHEST

EPS = 1e-6
LANES = 128
SUBLANES = 8
V7X_VMEM_BYTES = 64 * 1024 * 1024
VMEM_LIMIT = V7X_VMEM_BYTES * 7 // 8

HEAD_DK = 128
GD_DV = 256
GD_CONV_W = 5
GRID_W = 64
EC_CAPACITY_FACTOR = 2
CHUNK = 64
HALF = CHUNK // 2
GD_SOLVE_ROWS = 4 * CHUNK
EXP_CLAMP = 38.0
MAX_BLOCK_ROWS = 256
FFN_TILE = 256
FFN_MAX_ROWS = 1152
AFF_BITS = 31


def _cp(*sem):
    return pltpu.CompilerParams(dimension_semantics=sem, vmem_limit_bytes=VMEM_LIMIT)


def _dot(a, b):
    return jnp.dot(a, b, preferred_element_type=F32)


def _dot_nt(a, b):
    return lax.dot_general(a, b, (((1,), (1,)), ((), ())), preferred_element_type=F32)


def _dot_tn(a, b):
    return lax.dot_general(a, b, (((0,), (0,)), ((), ())), preferred_element_type=F32)


def _dot_hi(a, b):
    return jnp.dot(a, b, preferred_element_type=F32, precision=HIGHEST)


def _dot_nt_hi(a, b):
    return lax.dot_general(a, b, (((1,), (1,)), ((), ())), preferred_element_type=F32,
                           precision=HIGHEST)


def _silu(x):
    return x * jax.nn.sigmoid(x)


def _norm_mod(x, g, shift, scale):
    y = x * lax.rsqrt(jnp.mean(x * x, axis=-1, keepdims=True) + EPS)
    return y * (g * (1.0 + scale)) + shift


def _iota(shape, dim):
    return lax.broadcasted_iota(jnp.int32, shape, dim)


def _split3(x):
    hi = x.astype(BF16)
    r1 = x - hi.astype(F32)
    mid = r1.astype(BF16)
    lo = (r1 - mid.astype(F32)).astype(BF16)
    return [hi, mid, lo]


def _adaln_kernel(c_ref, w_ref, b_ref, o_ref):
    o_ref[0] = _dot_hi(_silu(c_ref[...]), w_ref[0]) + b_ref[0]


def _adaln(cond, w_mod, b_mod):
    depth, d, n = w_mod.shape
    r = cond.shape[0]
    tn = n // 12
    return pl.pallas_call(
        _adaln_kernel,
        grid=(depth, n // tn),
        in_specs=[pl.BlockSpec((r, d), lambda i, j: (0, 0)),
                  pl.BlockSpec((1, d, tn), lambda i, j: (i, 0, j)),
                  pl.BlockSpec((1, 1, tn), lambda i, j: (i, 0, j))],
        out_specs=pl.BlockSpec((1, r, tn), lambda i, j: (i, 0, j)),
        out_shape=jax.ShapeDtypeStruct((depth, r, n), F32),
        compiler_params=_cp("parallel", "parallel"),
        name="adaln",
    )(cond, w_mod, b_mod.reshape(depth, 1, n))


def _hg_inproj_kernel(c_ref, x_ref, mod_ref, g_ref, lb_ref, wq_ref, wf_ref, wi_ref, wg_ref,
                      q_ref, lf_ref, v_ref, gate_ref, *, layer, n_heads, nc):
    m = mod_ref[0, 0]
    xb = jnp.where(pl.program_id(1) < nc, c_ref[0], x_ref[0])
    hb = _norm_mod(xb, g_ref[...], m[0:1], m[1:2]).astype(BF16)
    fdim = wq_ref.shape[1]
    dk = fdim // n_heads
    dv = wi_ref.shape[1] // n_heads

    q = _silu(_dot(hb, wq_ref[...]))
    for h in range(n_heads):
        q_ref[0, h] = q[:, h * dk:(h + 1) * dk].astype(BF16)

    lbp = lb_ref[...]
    e = jnp.exp(lbp - jnp.max(lbp, axis=0))
    lb = jnp.sum(e[:layer + 1], axis=0) / jnp.sum(e, axis=0)

    zf = _dot(hb, wf_ref[...])
    for d in range(2):
        lbd = lb[d:d + 1]
        f = lbd + (1.0 - lbd) * jax.nn.sigmoid(zf[:, d * fdim:(d + 1) * fdim])
        lf = jnp.log(f)
        for h in range(n_heads):
            lf_ref[d, 0, h] = lf[:, h * dk:(h + 1) * dk]

    v = _dot(hb, wi_ref[...])
    for h in range(n_heads):
        v_ref[0, h] = v[:, h * dv:(h + 1) * dv].astype(BF16)

    gate_ref[0] = _silu(_dot(hb, wg_ref[...])).astype(BF16)


def _hg_inproj(ctx, x, modsel, g, hg_lb, w_in, *, layer, nc, tm):
    b, n_lat, d = x.shape
    t = ctx.shape[1] + n_lat
    fdim = hg_lb.shape[-1]
    n_heads = fdim // HEAD_DK
    dv = d // n_heads
    wq = w_in[:, :fdim].astype(BF16)
    wf = w_in[:, fdim:3 * fdim].astype(BF16)
    wi = w_in[:, 3 * fdim:3 * fdim + d].astype(BF16)
    wg = w_in[:, 3 * fdim + d:].astype(BF16)
    full = lambda a: pl.BlockSpec(a.shape, lambda i, j: (0,) * a.ndim)
    g2 = g.reshape(1, d)
    return pl.pallas_call(
        functools.partial(_hg_inproj_kernel, layer=layer, n_heads=n_heads, nc=nc),
        grid=(b, t // tm),
        in_specs=[pl.BlockSpec((1, tm, d), lambda i, j: (i, jnp.minimum(j, nc - 1), 0)),
                  pl.BlockSpec((1, tm, d), lambda i, j: (i, jnp.maximum(j - nc, 0), 0)),
                  pl.BlockSpec((1, 1, 6, d), lambda i, j: (i, jnp.where(j >= nc, 1, 0), 0, 0)),
                  full(g2), full(hg_lb), full(wq), full(wf), full(wi), full(wg)],
        out_specs=[pl.BlockSpec((1, n_heads, tm, HEAD_DK), lambda i, j: (i, 0, j, 0)),
                   pl.BlockSpec((2, 1, n_heads, tm, HEAD_DK), lambda i, j: (0, i, 0, j, 0)),
                   pl.BlockSpec((1, n_heads, tm, dv), lambda i, j: (i, 0, j, 0)),
                   pl.BlockSpec((1, tm, d), lambda i, j: (i, j, 0))],
        out_shape=[jax.ShapeDtypeStruct((b, n_heads, t, HEAD_DK), BF16),
                   jax.ShapeDtypeStruct((2, b, n_heads, t, HEAD_DK), F32),
                   jax.ShapeDtypeStruct((b, n_heads, t, dv), BF16),
                   jax.ShapeDtypeStruct((b, t, d), BF16)],
        compiler_params=_cp("parallel", "parallel"),
        name="hg_inproj",
    )(ctx, x, modsel, g2, hg_lb, wq, wf, wi, wg)


def _scan_block_index(step, nc, nl, rev):
    if not rev:
        return step
    return jnp.where(step < nc, nc - 1 - step, nc + nl - 1 - (step - nc))


def _chunk_masks(rev):
    r = _iota((CHUNK, CHUNK), 0)
    c = _iota((CHUNK, CHUNK), 1)
    incl = (c >= r) if rev else (c <= r)
    strict = (c > r) if rev else (c < r)
    return incl, strict


def _gla_scan_kernel(q_ref, lf_ref, v_ref, o_ref, st_ref, oin_ref, qc_ref, kh_ref,
                     *, rev, n_heads, nck, tm):
    @pl.when(pl.program_id(1) == 0)
    def _():
        st_ref[...] = jnp.zeros(st_ref.shape, F32)

    dk = q_ref.shape[-1]
    r = _iota((tm, tm), 0)
    c = _iota((tm, tm), 1)
    same_chunk = (r // CHUNK) == (c // CHUNK)
    causal = (c >= r) if rev else (c <= r)
    lmb = jnp.where(same_chunk & causal, 1.0, 0.0).astype(BF16)
    mask_d = ((r // HALF) == (c // HALF)) & causal
    second = 0 if rev else 1
    mask_x = same_chunk & ((r // HALF) % 2 == second) & ((c // HALF) % 2 == 1 - second)

    def per_rows(bh, row_of, span):
        return jnp.concatenate([jnp.broadcast_to(bh[row_of(x0):row_of(x0) + 1], (span, dk))
                                for x0 in range(0, tm, span)], axis=0)

    heads = range(n_heads)
    bhs = []
    for h in heads:
        g3 = _dot(lmb, jnp.concatenate(_split3(lf_ref[0, 0, h]), axis=1))
        bhs.append(g3[:, :dk] + g3[:, dk:2 * dk] + g3[:, 2 * dk:])
    dec, atts = {}, []
    for h in heads:
        bh = bhs[h]
        q = q_ref[0, h].astype(F32)
        k = 1.0 - jnp.exp(lf_ref[0, 0, h])
        bls = [bh[ci * CHUNK:ci * CHUNK + 1] if rev
               else bh[(ci + 1) * CHUNK - 1:(ci + 1) * CHUNK] for ci in range(nck)]
        blc = jnp.concatenate([jnp.broadcast_to(x, (CHUNK, dk)) for x in bls], axis=0)
        qc_ref[h] = (q * jnp.exp(bh)).astype(BF16)
        kh_ref[h] = (k * jnp.exp(blc - bh)).astype(BF16)
        dec[h] = [jnp.exp(x) for x in bls]
        bm = per_rows(bh, lambda x0: x0 + HALF // 2, HALF)
        att_d = _dot_nt((q * jnp.exp(jnp.minimum(bh - bm, EXP_CLAMP))).astype(BF16),
                        (k * jnp.exp(jnp.minimum(bm - bh, EXP_CLAMP))).astype(BF16))
        be = per_rows(bh, lambda x0: x0 + (HALF if rev else HALF - 1), CHUNK)
        att_x = _dot_nt((q * jnp.exp(jnp.minimum(bh - be, EXP_CLAMP))).astype(BF16),
                        (k * jnp.exp(jnp.minimum(be - bh, EXP_CLAMP))).astype(BF16))
        atts.append((att_d, att_x))
    for h in heads:
        att = jnp.where(mask_d, atts[h][0], 0.0) + jnp.where(mask_x, atts[h][1], 0.0)
        oin_ref[h] = _dot(att.astype(BF16), v_ref[0, h])

    for step in range(nck):
        ci = nck - 1 - step if rev else step
        rows = pl.ds(ci * CHUNK, CHUNK)
        for h in heads:
            st = st_ref[h]
            o = oin_ref[h, rows, :] + _dot_nt(qc_ref[h, rows, :], st.astype(BF16))
            o_ref[0, h, rows, :] = o.astype(o_ref.dtype)
            st_ref[h] = st * dec[h][ci] + _dot_tn(v_ref[0, h, rows, :], kh_ref[h, rows, :])


def _gla_scan(q, lf, v, *, rev, nc, tm, out_dtype):
    b, n_heads, t, dk = q.shape
    dv = v.shape[-1]
    nblk = t // tm
    nl = nblk - nc
    nck = tm // CHUNK
    d = 1 if rev else 0
    blk = lambda i, s: _scan_block_index(s, nc, nl, rev)
    return pl.pallas_call(
        functools.partial(_gla_scan_kernel, rev=rev, n_heads=n_heads, nck=nck, tm=tm),
        grid=(b, nblk),
        in_specs=[pl.BlockSpec((1, n_heads, tm, dk), lambda i, s: (i, 0, blk(i, s), 0)),
                  pl.BlockSpec((1, 1, n_heads, tm, dk), lambda i, s: (d, i, 0, blk(i, s), 0)),
                  pl.BlockSpec((1, n_heads, tm, dv), lambda i, s: (i, 0, blk(i, s), 0))],
        out_specs=pl.BlockSpec((1, n_heads, tm, dv), lambda i, s: (i, 0, blk(i, s), 0)),
        out_shape=jax.ShapeDtypeStruct((b, n_heads, t, dv), out_dtype),
        scratch_shapes=[pltpu.VMEM((n_heads, dv, dk), F32),
                        pltpu.VMEM((n_heads, tm, dv), F32),
                        pltpu.VMEM((n_heads, tm, dk), BF16),
                        pltpu.VMEM((n_heads, tm, dk), BF16)],
        compiler_params=_cp("parallel", "arbitrary"),
        name="gla_scan_bwd" if rev else "gla_scan_fwd",
    )(q, lf, v)


def _outproj_kernel(of_ref, ob_ref, gate_ref, gain_ref, w_ref, c_ref, x_ref, mod_ref, o_ref, y_ref,
                    *, n_heads, nc, rpc):
    dv = of_ref.shape[-1]
    for h in range(n_heads):
        o = of_ref[0, h].astype(F32) + ob_ref[0, h].astype(F32)
        cs = slice(h * dv, (h + 1) * dv)
        o = o * lax.rsqrt(jnp.mean(o * o, axis=-1, keepdims=True) + EPS) * gain_ref[:, cs]
        y_ref[:, cs] = (o * gate_ref[0, :, cs].astype(F32)).astype(BF16)
    y = mod_ref[0, 0][2:3] * _dot(y_ref[...], w_ref[...])
    if not rpc:
        o_ref[0] = jnp.where(pl.program_id(1) < nc, c_ref[0], x_ref[0]) + y
    else:
        for j in range(y.shape[0] // rpc):
            o_ref[0, :, j, :] = x_ref[0, :, j, :] + y[j * rpc:(j + 1) * rpc]


def _outproj(o_f, o_b, gate, gain, w_out, x_ctx, x_lat, modsel, *, nc, tm, rpc=0):
    b, n_heads, t, dv = o_f.shape
    d = w_out.shape[-1]
    hv = n_heads * dv
    off = nc if rpc else 0
    nblk = t // tm - off
    seg = (lambda j: 1) if rpc else (lambda j: jnp.where(j >= nc, 1, 0))
    w = w_out.astype(BF16)
    gain2 = gain.reshape(1, hv)
    if rpc:
        x_spec = o_spec = pl.BlockSpec((1, rpc, tm // rpc, d), lambda i, j: (i, 0, j, 0))
        o_shape = x_lat.shape
    else:
        x_spec = pl.BlockSpec((1, tm, d), lambda i, j: (i, jnp.maximum(j - nc, 0), 0))
        o_spec = pl.BlockSpec((1, tm, d), lambda i, j: (i, j, 0))
        o_shape = (b, t, d)
    return pl.pallas_call(
        functools.partial(_outproj_kernel, n_heads=n_heads, nc=nc, rpc=rpc),
        grid=(b, nblk),
        in_specs=[pl.BlockSpec((1, n_heads, tm, dv), lambda i, j: (i, 0, j + off, 0)),
                  pl.BlockSpec((1, n_heads, tm, dv), lambda i, j: (i, 0, j + off, 0)),
                  pl.BlockSpec((1, tm, hv), lambda i, j: (i, j + off, 0)),
                  pl.BlockSpec((1, hv), lambda i, j: (0, 0)),
                  pl.BlockSpec((hv, d), lambda i, j: (0, 0)),
                  pl.BlockSpec((1, tm, d), lambda i, j: (i, jnp.minimum(j, nc - 1), 0)),
                  x_spec,
                  pl.BlockSpec((1, 1, 6, d), lambda i, j: (i, seg(j), 0, 0))],
        out_specs=o_spec,
        out_shape=jax.ShapeDtypeStruct(o_shape, F32),
        scratch_shapes=[pltpu.VMEM((tm, hv), BF16)],
        compiler_params=_cp("parallel", "parallel"),
        name="outproj",
    )(o_f, o_b, gate, gain2, w, x_ctx, x_lat, modsel)


def _gd_inproj_kernel(cp_ref, cc_ref, cn_ref, lp_ref, lc_ref, ln_ref, mod_ref, g_ref, wqkv_ref,
                      wg_ref, wab_ref, cw_ref, alog_ref, dtb_ref, q_ref, k_ref, v_ref, gate_ref,
                      ab_ref, xs_ref, ys_ref, *, nc, nblk, n_heads, tm, cw, rpc):
    t = pl.program_id(1)
    first = (t == 0) | (t == nc)
    last = (t == nc - 1) | (t == nblk - 1)
    m = mod_ref[0, 0]
    halo = SUBLANES
    nlt = xs_ref.shape[0]
    d = nlt * LANES
    tile = lambda c: slice(c * LANES, (c + 1) * LANES)

    @pl.when(t < nc)
    def _():
        for c in range(nlt):
            xs_ref[c, 0:halo] = cp_ref[0, :, tile(c)]
            xs_ref[c, halo:halo + tm] = cc_ref[0, :, tile(c)]
            xs_ref[c, halo + tm:] = cn_ref[0, :, tile(c)]

    @pl.when(t >= nc)
    def _():
        cpb = tm // rpc
        for c in range(nlt):
            xs_ref[c, 0:halo] = lp_ref[0, :, SUBLANES - 1, tile(c)]
            for j in range(cpb):
                xs_ref[c, halo + j * rpc:halo + (j + 1) * rpc] = lc_ref[0, :, j, tile(c)]
            xs_ref[c, halo + tm:] = ln_ref[0, :, 0, tile(c)]

    xc = jnp.concatenate([xs_ref[c, halo:halo + tm] for c in range(nlt)], axis=1)
    hc = _norm_mod(xc, g_ref[...], m[0:1], m[1:2]).astype(BF16)

    rows = tm + 2 * halo
    ni = rows // SUBLANES
    xp = jnp.concatenate(
        [jnp.concatenate([xs_ref[c, pl.ds(i, SUBLANES, stride=ni), :] for i in range(ni)], axis=0)
         for c in range(nlt)], axis=1)
    he = _norm_mod(xp, g_ref[...], m[0:1], m[1:2]).astype(BF16)
    row = _iota((rows, 1), 0)
    tok = ni * (row % SUBLANES) + row // SUBLANES
    valid = ((tok >= halo) | jnp.logical_not(first)) & ((tok < tm + halo) | jnp.logical_not(last))

    gate_ref[0] = _silu(_dot(hc, wg_ref[...])).astype(BF16)

    zab = _dot(hc, wab_ref[...])
    lane = _iota(zab.shape, 1)
    loga = -jnp.exp(alog_ref[...]) * jax.nn.softplus(zab + dtb_ref[...])
    ab_ref[0] = jnp.where(lane < 2 * n_heads, loga, jax.nn.sigmoid(zab))

    qk = n_heads * HEAD_DK
    nchan = wqkv_ref.shape[1]
    pad = GD_CONV_W // 2
    for cc in range(nchan // cw):
        c0 = cc * cw
        z = jnp.where(valid, _dot(he, wqkv_ref[:, c0:c0 + cw]), 0.0)
        z3 = z.reshape(ni, SUBLANES, cw)
        zlo = pltpu.roll(z3[ni - pad:], 1, 1)
        zhi = pltpu.roll(z3[:pad], SUBLANES - 1, 1)
        taps = [cw_ref[j:j + 1, c0:c0 + cw] for j in range(GD_CONV_W)]

        def group(i):
            return zlo[i + pad] if i < 0 else (zhi[i - ni] if i >= ni else z3[i])

        def finish(acc, i0):
            u = _silu(acc)
            if c0 < 2 * qk:
                scale = HEAD_DK ** -0.5 if c0 < qk else 1.0
                parts = []
                for j in range(cw // HEAD_DK):
                    tt = u[:, :, j * HEAD_DK:(j + 1) * HEAD_DK]
                    parts.append(tt * (lax.rsqrt(jnp.sum(tt * tt, axis=-1, keepdims=True) + EPS)
                                       * scale))
                u = jnp.concatenate(parts, axis=-1)
            for c in range(cw // LANES):
                for i in range(u.shape[0]):
                    ys_ref[c, pl.ds(i0 + i, SUBLANES, stride=ni), :] = u[i][:, tile(c)]

        inner = ni - 2 * pad
        acc = taps[0] * z3[0:inner]
        for j in range(1, GD_CONV_W):
            acc = acc + taps[j] * z3[j:j + inner]
        finish(acc, pad)
        for i0 in (0, ni - pad):
            edge = jnp.stack([sum(taps[j][0] * group(i + j - pad) for j in range(GD_CONV_W))
                              for i in range(i0, i0 + pad)], axis=0)
            finish(edge, i0)

        def centre(lane0, width):
            return jnp.concatenate([ys_ref[c, halo:halo + tm] for c in
                                    range(lane0 // LANES, (lane0 + width) // LANES)], axis=1)

        if c0 < 2 * qk:
            dst, base = (q_ref, c0) if c0 < qk else (k_ref, c0 - qk)
            for j in range(cw // HEAD_DK):
                dst[0, base // HEAD_DK + j] = centre(j * HEAD_DK, HEAD_DK).astype(BF16)
        else:
            base = c0 - 2 * qk
            for j in range(cw // GD_DV):
                v_ref[0, base // GD_DV + j] = centre(j * GD_DV, GD_DV).astype(BF16)


def _gd_inproj(x_ctx, x_lat, modsel, g, w_in, conv_w, a_log, dt_bias, *, tm):
    b, n_ctx, d = x_ctx.shape
    n_lat = x_lat.shape[1]
    rpc = n_lat // GRID_W
    assert rpc % SUBLANES == 0 and tm % rpc == 0 and n_ctx % tm == 0 and n_ctx > 0
    t = n_ctx + n_lat
    nc = n_ctx // tm
    cpb = tm // rpc
    assert cpb % SUBLANES == 0, "halo blocks take one 8-column group of the grid"
    lat_v = x_lat.reshape(b, rpc, GRID_W, d)
    n_heads = a_log.shape[-1]
    qk = n_heads * HEAD_DK
    vd = n_heads * GD_DV
    nchan = 2 * qk + vd
    nblk = t // tm
    cw = min(512, qk)
    wqkv = w_in[:, :nchan].astype(BF16)
    wg = w_in[:, nchan:nchan + vd].astype(BF16)
    wab = jnp.pad(w_in[:, nchan + vd:], ((0, 0), (0, LANES - 4 * n_heads))).astype(BF16)
    alog = jnp.pad(a_log.reshape(1, 2 * n_heads), ((0, 0), (0, LANES - 2 * n_heads)))
    dtb = jnp.pad(dt_bias.reshape(1, 2 * n_heads), ((0, 0), (0, LANES - 2 * n_heads)))
    g2 = g.reshape(1, d)
    full = lambda a: pl.BlockSpec(a.shape, lambda i, j: (0,) * a.ndim)
    spb = tm // SUBLANES
    last_slab = n_ctx // SUBLANES - 1
    gpb = cpb // SUBLANES
    last_cg = GRID_W // SUBLANES - 1
    return pl.pallas_call(
        functools.partial(_gd_inproj_kernel, nc=nc, nblk=nblk, n_heads=n_heads, tm=tm, cw=cw,
                          rpc=rpc),
        grid=(b, nblk),
        in_specs=[pl.BlockSpec((1, SUBLANES, d),
                               lambda i, j: (i, jnp.clip(j * spb - 1, 0, last_slab), 0)),
                  pl.BlockSpec((1, tm, d), lambda i, j: (i, jnp.minimum(j, nc - 1), 0)),
                  pl.BlockSpec((1, SUBLANES, d),
                               lambda i, j: (i, jnp.clip((j + 1) * spb, 0, last_slab), 0)),
                  pl.BlockSpec((1, SUBLANES, SUBLANES, d),
                               lambda i, j: (i, rpc // SUBLANES - 1,
                                             jnp.clip((j - nc) * gpb - 1, 0, last_cg), 0)),
                  pl.BlockSpec((1, rpc, cpb, d), lambda i, j: (i, 0, jnp.maximum(j - nc, 0), 0)),
                  pl.BlockSpec((1, SUBLANES, SUBLANES, d),
                               lambda i, j: (i, 0, jnp.clip((j - nc + 1) * gpb, 0, last_cg), 0)),
                  pl.BlockSpec((1, 1, 6, d), lambda i, j: (i, jnp.where(j >= nc, 1, 0), 0, 0)),
                  full(g2), full(wqkv), full(wg), full(wab), full(conv_w), full(alog), full(dtb)],
        out_specs=[pl.BlockSpec((1, n_heads, tm, HEAD_DK), lambda i, j: (i, 0, j, 0)),
                   pl.BlockSpec((1, n_heads, tm, HEAD_DK), lambda i, j: (i, 0, j, 0)),
                   pl.BlockSpec((1, n_heads, tm, GD_DV), lambda i, j: (i, 0, j, 0)),
                   pl.BlockSpec((1, tm, vd), lambda i, j: (i, j, 0)),
                   pl.BlockSpec((1, tm, LANES), lambda i, j: (i, j, 0))],
        out_shape=[jax.ShapeDtypeStruct((b, n_heads, t, HEAD_DK), BF16),
                   jax.ShapeDtypeStruct((b, n_heads, t, HEAD_DK), BF16),
                   jax.ShapeDtypeStruct((b, n_heads, t, GD_DV), BF16),
                   jax.ShapeDtypeStruct((b, t, vd), BF16),
                   jax.ShapeDtypeStruct((b, t, LANES), F32)],
        scratch_shapes=[pltpu.VMEM((d // LANES, tm + 2 * SUBLANES, LANES), F32),
                        pltpu.VMEM((cw // LANES, tm + 2 * SUBLANES, LANES), F32)],
        compiler_params=_cp("parallel", "parallel"),
        name="gd_inproj",
    )(x_ctx, x_ctx, x_ctx, lat_v, lat_v, lat_v, modsel, g2, wqkv, wg, wab, conv_w, alog, dtb)


def _neumann_inverses(mats, eye):
    n = eye.shape[0]
    ts = [eye - a for a in mats]
    ps = []
    for a in mats:
        ab = a.astype(BF16)
        ps.append(_dot(ab, ab))
    lvl = 2
    while lvl < CHUNK:
        for i in range(len(mats)):
            pb = ps[i].astype(BF16)
            tb = ts[i].astype(BF16)
            if 2 * lvl >= CHUNK:
                ts[i] = ts[i] + _dot(pb, tb)
            else:
                out = _dot(pb, jnp.concatenate([pb, tb], axis=1))
                ps[i] = out[:, :n]
                ts[i] = ts[i] + out[:, n:]
        lvl *= 2
    return ts


def _gdn_scan_block(q_ref, k_ref, v_ref, ab_ref, o_ref, s_ref, u_ref, l1_ref, l2_ref,
                    *, rev, n_heads, nck, tm):
    d = 1 if rev else 0

    def masks(n):
        r = _iota((n, n), 0)
        c = _iota((n, n), 1)
        same = (r // CHUNK) == (c // CHUNK)
        return (same & ((c >= r) if rev else (c <= r)), same & ((c > r) if rev else (c < r)),
                jnp.where(r == c, 1.0, 0.0))

    pb = min(tm, GD_SOLVE_ROWS)
    incl, strict, eye = masks(pb)
    dv = v_ref.shape[-1]

    ab = ab_ref[0]
    g3 = _dot(jnp.where(masks(tm)[0], 1.0, 0.0).astype(BF16),
              jnp.concatenate(_split3(ab), axis=1))
    gcol = g3[:, :LANES] + g3[:, LANES:2 * LANES] + g3[:, 2 * LANES:]
    eye_l = jnp.where(_iota((LANES, LANES), 0) == _iota((LANES, LANES), 1), 1.0, 0.0).astype(BF16)
    gr3 = _dot_nt(eye_l, jnp.concatenate(_split3(gcol), axis=0))
    grow = gr3[:, :tm] + gr3[:, tm:2 * tm] + gr3[:, 2 * tm:]

    incl_c, _ = _chunk_masks(rev)
    wide = lambda x, n: jnp.concatenate([x] * (n // LANES), axis=1) if n > LANES else x[:, :n]
    heads = range(n_heads)
    cas = [d * n_heads + h for h in heads]
    gcbs = [jnp.broadcast_to(gcol[:, ca:ca + 1], (tm, LANES)) for ca in cas]
    bcbs = [jnp.broadcast_to(ab[:, 2 * n_heads + ca:2 * n_heads + ca + 1], (tm, LANES))
            for ca in cas]

    mats = []
    for h in heads:
        k = k_ref[0, h]
        for p0 in range(0, tm, pb):
            gam = jnp.where(incl, jnp.exp(jnp.minimum(
                wide(gcbs[h][p0:p0 + pb], pb) - grow[cas[h]:cas[h] + 1, p0:p0 + pb], 0.0)), 0.0)
            kp = k[p0:p0 + pb]
            mats.append(jnp.where(strict, wide(bcbs[h][p0:p0 + pb], pb) * _dot_nt(kp, kp) * gam,
                                  0.0))
    tinvs = _neumann_inverses(mats, eye)

    uws = []
    for h in heads:
        kf = k_ref[0, h].astype(F32)
        rhs = jnp.concatenate(
            [(wide(bcbs[h], dv) * v_ref[0, h].astype(F32)).astype(BF16),
             (bcbs[h] * jnp.exp(gcbs[h]) * kf).astype(BF16)], axis=1)
        uws.append(jnp.concatenate(
            [_dot(tinvs[h * (tm // pb) + i].astype(BF16), rhs[i * pb:(i + 1) * pb])
             for i in range(tm // pb)], axis=0))

    egl = {}
    for h in heads:
        ca, gcb, uw = cas[h], gcbs[h], uws[h]
        q = q_ref[0, h]
        k = k_ref[0, h]
        kf = k.astype(F32)
        egb = jnp.exp(gcb)
        u_ref[h] = uw[:, :dv]
        wb = uw[:, dv:].astype(BF16)
        qg = (q.astype(F32) * egb).astype(BF16)
        gls = [gcb[ci * CHUNK:ci * CHUNK + 1] if rev
               else gcb[(ci + 1) * CHUNK - 1:(ci + 1) * CHUNK] for ci in range(nck)]
        glb = jnp.concatenate([jnp.broadcast_to(g, (CHUNK, LANES)) for g in gls], axis=0)
        kd = kf * jnp.exp(glb - gcb)
        egl[h] = [jnp.exp(g[:, 0:1]) for g in gls]
        for ci in range(nck):
            c0 = ci * CHUNK
            l1_ref[h, ci, :CHUNK] = wb[c0:c0 + CHUNK]
            l1_ref[h, ci, CHUNK:] = qg[c0:c0 + CHUNK]
            gam_c = jnp.where(incl_c, jnp.exp(jnp.minimum(
                gcb[c0:c0 + CHUNK, :CHUNK] - grow[ca:ca + 1, c0:c0 + CHUNK], 0.0)), 0.0)
            qk_c = _dot_nt(q[c0:c0 + CHUNK], k[c0:c0 + CHUNK])
            l2_ref[h, ci, :CHUNK] = (qk_c * gam_c).astype(BF16)
            l2_ref[h, ci, CHUNK:] = kd[c0:c0 + CHUNK].T.astype(BF16)

    for step in range(nck):
        ci = nck - 1 - step if rev else step
        rows = pl.ds(ci * CHUNK, CHUNK)
        r1s = [_dot(l1_ref[h, ci], s_ref[h].astype(BF16)) for h in heads]
        weffs = [(u_ref[h, rows, :] - r1s[h][:CHUNK]).astype(BF16) for h in heads]
        r2s = [_dot(l2_ref[h, ci], weffs[h]) for h in heads]
        for h in heads:
            o_ref[0, h, rows, :] = (r2s[h][:CHUNK] + r1s[h][CHUNK:]).astype(o_ref.dtype)
            s_ref[h] = s_ref[h] * egl[h][ci] + r2s[h][CHUNK:]


def _gdn_scan_kernel(*refs, n_heads, nck, tm):
    fwd = refs[0:4] + refs[8:9] + refs[10:14]
    bwd = refs[4:8] + refs[9:10] + refs[14:18]

    @pl.when(pl.program_id(1) == 0)
    def _():
        for s_ref in (fwd[5], bwd[5]):
            s_ref[...] = jnp.zeros(s_ref.shape, F32)

    _gdn_scan_block(*fwd, rev=False, n_heads=n_heads, nck=nck, tm=tm)
    _gdn_scan_block(*bwd, rev=True, n_heads=n_heads, nck=nck, tm=tm)


def _gdn_scan(q, k, v, ab, *, nc, tm, out_dtype):
    b, n_heads, t, dk = q.shape
    dv = v.shape[-1]
    nblk = t // tm
    nl = nblk - nc
    nck = tm // CHUNK

    def specs(rev):
        blk = lambda s: _scan_block_index(s, nc, nl, rev)
        return [pl.BlockSpec((1, n_heads, tm, dk), lambda i, s: (i, 0, blk(s), 0)),
                pl.BlockSpec((1, n_heads, tm, dk), lambda i, s: (i, 0, blk(s), 0)),
                pl.BlockSpec((1, n_heads, tm, dv), lambda i, s: (i, 0, blk(s), 0)),
                pl.BlockSpec((1, tm, LANES), lambda i, s: (i, blk(s), 0))]

    scratch = [pltpu.VMEM((n_heads, dk, dv), F32),
               pltpu.VMEM((n_heads, tm, dv), F32),
               pltpu.VMEM((n_heads, nck, 2 * CHUNK, dk), BF16),
               pltpu.VMEM((n_heads, nck, CHUNK + dk, CHUNK), BF16)]
    o_shape = jax.ShapeDtypeStruct((b, n_heads, t, dv), out_dtype)
    return pl.pallas_call(
        functools.partial(_gdn_scan_kernel, n_heads=n_heads, nck=nck, tm=tm),
        grid=(b, nblk),
        in_specs=specs(False) + specs(True),
        out_specs=[specs(False)[2], specs(True)[2]],
        out_shape=[o_shape, o_shape],
        scratch_shapes=scratch + scratch,
        compiler_params=_cp("parallel", "arbitrary"),
        name="gdn_scan",
    )(q, k, v, ab, q, k, v, ab)


def _route_kernel(x_ref, mod_ref, g_ref, wrt_ref, h_ref, slot_ref, aff_ref, lg_ref,
                  *, n_ctx, cap_c, cap_l, tr):
    n_exp, t = lg_ref.shape
    for rt in range(t // tr):
        m = mod_ref[0, 0 if rt * tr < n_ctx else 1]
        rows = slice(rt * tr, (rt + 1) * tr)
        h = _norm_mod(x_ref[0, rows, :], g_ref[...], m[3:4], m[4:5])
        h_ref[0, rows, :] = h.astype(BF16)
        lg_ref[:, rows] = _dot_nt_hi(wrt_ref[...], h)
    lg = lg_ref[...]
    e = jnp.exp(lg - jnp.max(lg, axis=0, keepdims=True))
    aff = e / jnp.sum(e, axis=0, keepdims=True)
    aff_ref[0] = aff
    bits = lax.bitcast_convert_type(aff, jnp.int32)
    lane = _iota((n_exp, t), 1)

    if n_ctx:
        is_ctx = lane < n_ctx
        regions = [(is_ctx, cap_c), (jnp.logical_not(is_ctx), cap_l)]
    else:
        regions = [(None, cap_l)]

    def count(pred, mask):
        p = pred if mask is None else (pred & mask)
        return jnp.sum(jnp.where(p, 1.0, 0.0), axis=1, keepdims=True)

    def thr_body(i, thrs):
        bit = lax.shift_left(jnp.int32(1), AFF_BITS - 1 - i)
        out = []
        for (mask, cap), thr in zip(regions, thrs):
            cand = thr | bit
            out.append(jnp.where(count(bits >= cand, mask) >= cap, cand, thr))
        return tuple(out)

    zero = jnp.zeros((n_exp, 1), jnp.int32)
    thrs = lax.fori_loop(0, AFF_BITS, thr_body, tuple(zero for _ in regions))

    idx_bits = t.bit_length()
    sel = None
    for (mask, cap), thr in zip(regions, thrs):
        gt = bits > thr
        tie = bits == thr
        need = cap - count(gt, mask)

        def j_body(i, j, tie=tie, mask=mask, need=need):
            cand = j | lax.shift_left(jnp.int32(1), idx_bits - 1 - i)
            return jnp.where(count(tie & (lane < cand), mask) <= need, cand, j)

        jmax = lax.fori_loop(0, idx_bits, j_body, zero)
        s = gt | (tie & (lane < jmax))
        if mask is not None:
            s = s & mask
        sel = s if sel is None else (sel | s)

    lt = LANES if t % LANES == 0 else CHUNK
    ut = jnp.where(_iota((lt, lt), 0) < _iota((lt, lt), 1), 1.0, 0.0).astype(BF16)
    self = jnp.where(sel, 1.0, 0.0)
    run = jnp.zeros((n_exp, 1), F32)
    pres = []
    for i in range(t // lt):
        tile = self[:, i * lt:(i + 1) * lt]
        pres.append(_dot(tile.astype(BF16), ut) + run)
        run = run + jnp.sum(tile, axis=1, keepdims=True)
    pre = jnp.concatenate(pres, axis=1)
    if n_ctx:
        slot = jnp.where(is_ctx, pre + cap_l, pre - cap_c)
    else:
        slot = pre
    slot_ref[0] = jnp.where(sel, slot, -1.0)


def _route(x, modsel, g, w_router, *, n_ctx, cap_c, cap_l, tr):
    b, t, d = x.shape
    n_exp = w_router.shape[-1]
    wrt = w_router.T
    g2 = g.reshape(1, d)
    return pl.pallas_call(
        functools.partial(_route_kernel, n_ctx=n_ctx, cap_c=cap_c, cap_l=cap_l, tr=tr),
        grid=(b,),
        in_specs=[pl.BlockSpec((1, t, d), lambda i: (i, 0, 0)),
                  pl.BlockSpec((1, 2, 6, d), lambda i: (i, 0, 0, 0)),
                  pl.BlockSpec((1, d), lambda i: (0, 0)),
                  pl.BlockSpec((n_exp, d), lambda i: (0, 0))],
        out_specs=[pl.BlockSpec((1, t, d), lambda i: (i, 0, 0)),
                   pl.BlockSpec((1, n_exp, t), lambda i: (i, 0, 0)),
                   pl.BlockSpec((1, n_exp, t), lambda i: (i, 0, 0))],
        out_shape=[jax.ShapeDtypeStruct((b, t, d), BF16),
                   jax.ShapeDtypeStruct((b, n_exp, t), F32),
                   jax.ShapeDtypeStruct((b, n_exp, t), F32)],
        scratch_shapes=[pltpu.VMEM((n_exp, t), F32)],
        compiler_params=_cp("parallel"),
        name="moe_route",
    )(x, modsel, g2, wrt)


def _gather_kernel(slot_ref, h_ref, x_ref, *, n_ctx, cap_c, cap_l):
    slot = slot_ref[0, 0]
    t = slot.shape[1]
    sl = slot[:, n_ctx:]
    p = jnp.where(sl == _iota((cap_l, t - n_ctx), 0).astype(F32), 1.0, 0.0).astype(BF16)
    x_ref[0, 0, 0:cap_l, :] = _dot(p, h_ref[0, n_ctx:, :]).astype(BF16)
    if n_ctx:
        sc = slot[:, :n_ctx] - float(cap_l)
        p = jnp.where(sc == _iota((cap_c, n_ctx), 0).astype(F32), 1.0, 0.0).astype(BF16)
        x_ref[0, 0, cap_l:, :] = _dot(p, h_ref[0, :n_ctx, :]).astype(BF16)


def _gather(h, slot_row, *, n_ctx, cap_c, cap_l):
    b, t, d = h.shape
    n_exp = slot_row.shape[1]
    m = cap_l + (cap_c if n_ctx else 0)
    return pl.pallas_call(
        functools.partial(_gather_kernel, n_ctx=n_ctx, cap_c=cap_c, cap_l=cap_l),
        grid=(b, n_exp),
        in_specs=[pl.BlockSpec((1, 1, 1, t), lambda i, e: (i, e, 0, 0)),
                  pl.BlockSpec((1, t, d), lambda i, e: (i, 0, 0))],
        out_specs=pl.BlockSpec((1, 1, m, d), lambda i, e: (e, i, 0, 0)),
        out_shape=jax.ShapeDtypeStruct((n_exp, b, m, d), BF16),
        compiler_params=_cp("parallel", "parallel"),
        name="moe_gather",
    )(slot_row.reshape(b, n_exp, 1, t), h)


def _ffn_kernel(x_ref, wg_ref, wu_ref, wd_ref, y_ref, acc_ref, wgb_ref, wub_ref, wdb_ref, *, tr):
    ft = pl.program_id(1)

    @pl.when(ft == 0)
    def _():
        acc_ref[...] = jnp.zeros(acc_ref.shape, F32)

    wgb_ref[...] = wg_ref[...].astype(BF16)
    wub_ref[...] = wu_ref[...].astype(BF16)
    wdb_ref[...] = wd_ref[...].astype(BF16)

    def body(i, carry):
        rows = pl.ds(pl.multiple_of(i * tr, tr), tr)
        x = x_ref[0, rows, :]
        hid = (_silu(_dot(x, wgb_ref[...])) * _dot(x, wub_ref[...])).astype(BF16)
        acc_ref[rows, :] += _dot(hid, wdb_ref[...])
        return carry

    lax.fori_loop(0, x_ref.shape[1] // tr, body, 0)

    @pl.when(ft == pl.num_programs(1) - 1)
    def _():
        y_ref[0] = acc_ref[...].astype(y_ref.dtype)


def _ffn(xs, w_gate, w_up, w_down, *, layer):
    n_exp, bm, d = xs.shape
    f = w_gate.shape[-1]
    tf = FFN_TILE if f % FFN_TILE == 0 else f
    tr = bm // (-(-bm // FFN_MAX_ROWS))
    return pl.pallas_call(
        functools.partial(_ffn_kernel, tr=tr),
        grid=(n_exp, f // tf),
        in_specs=[pl.BlockSpec((1, bm, d), lambda e, j: (e, 0, 0)),
                  pl.BlockSpec((None, None, d, tf), lambda e, j: (layer, e, 0, j)),
                  pl.BlockSpec((None, None, d, tf), lambda e, j: (layer, e, 0, j)),
                  pl.BlockSpec((None, None, tf, d), lambda e, j: (layer, e, j, 0))],
        out_specs=pl.BlockSpec((1, bm, d), lambda e, j: (e, 0, 0)),
        out_shape=jax.ShapeDtypeStruct((n_exp, bm, d), BF16),
        scratch_shapes=[pltpu.VMEM((bm, d), F32),
                        pltpu.VMEM((d, tf), BF16),
                        pltpu.VMEM((d, tf), BF16),
                        pltpu.VMEM((tf, d), BF16)],
        compiler_params=_cp("parallel", "arbitrary"),
        name="moe_ffn",
    )(xs, w_gate, w_up, w_down)


def _combine_kernel(*refs, nc, cap_c, cap_l, final):
    x_ref, ys_ref, sc_ref, ac_ref, mod_ref = refs[:5]
    fg_ref = refs[5] if final else None
    outs = refs[5 + bool(final):-1]
    acc_ref = refs[-1]
    n_exp = ys_ref.shape[0]
    tm = x_ref.shape[1]

    def run(k0, kk, o_ref):
        sc = sc_ref[0]
        ac = ac_ref[0]
        lane = (_iota((tm, kk), 1) + k0).astype(F32)
        acc_ref[...] = jnp.zeros(acc_ref.shape, F32)
        for e in range(n_exp):
            pt = jnp.where(sc[:, e:e + 1] == lane, 1.0, 0.0).astype(BF16)
            acc_ref[...] += ac[:, e:e + 1] * _dot(pt, ys_ref[e, 0, k0:k0 + kk, :])
        x2 = x_ref[0] + mod_ref[0, 0][5:6] * acc_ref[...]
        if final:
            x2 = x2 * lax.rsqrt(jnp.mean(x2 * x2, axis=-1, keepdims=True) + EPS) * fg_ref[...]
        o_ref[0] = x2

    if nc:
        t = pl.program_id(1)

        @pl.when(t < nc)
        def _():
            run(cap_l, cap_c, outs[0])

        @pl.when(t >= nc)
        def _():
            run(0, cap_l, outs[1])
    else:
        run(0, cap_l, outs[0])


def _combine(x, ys, slot_col, aff_col, modsel, final_g, *, nc, cap_c, cap_l, tm):
    b, t, d = x.shape
    n_exp, _, m, _ = ys.shape
    seg = (lambda j: jnp.where(j >= nc, 1, 0)) if nc else (lambda j: 1)
    if nc:
        out_specs = [pl.BlockSpec((1, tm, d), lambda i, j: (i, jnp.minimum(j, nc - 1), 0)),
                     pl.BlockSpec((1, tm, d), lambda i, j: (i, jnp.maximum(j - nc, 0), 0))]
        out_shape = [jax.ShapeDtypeStruct((b, nc * tm, d), F32),
                     jax.ShapeDtypeStruct((b, t - nc * tm, d), F32)]
    else:
        out_specs = pl.BlockSpec((1, tm, d), lambda i, j: (i, j, 0))
        out_shape = jax.ShapeDtypeStruct((b, t, d), F32)
    in_specs = [pl.BlockSpec((1, tm, d), lambda i, j: (i, j, 0)),
                pl.BlockSpec((n_exp, 1, m, d), lambda i, j: (0, i, 0, 0)),
                pl.BlockSpec((1, tm, n_exp), lambda i, j: (i, j, 0)),
                pl.BlockSpec((1, tm, n_exp), lambda i, j: (i, j, 0)),
                pl.BlockSpec((1, 1, 6, d), lambda i, j: (i, seg(j), 0, 0))]
    args = [x, ys, slot_col, aff_col, modsel]
    if final_g is not None:
        in_specs.append(pl.BlockSpec((1, d), lambda i, j: (0, 0)))
        args.append(final_g.reshape(1, d))
    return pl.pallas_call(
        functools.partial(_combine_kernel, nc=nc, cap_c=cap_c, cap_l=cap_l,
                          final=final_g is not None),
        grid=(b, t // tm),
        in_specs=in_specs,
        out_specs=out_specs,
        out_shape=out_shape,
        scratch_shapes=[pltpu.VMEM((tm, d), F32)],
        compiler_params=_cp("parallel", "arbitrary"),
        name="moe_combine",
    )(*args)


def _moe(x, modsel, g, w_router, w_gate, w_up, w_down, final_g, *, layer, n_ctx, n_lat, tm):
    b, t, d = x.shape
    n_exp = w_router.shape[-1]
    cap_l = EC_CAPACITY_FACTOR * n_lat // n_exp
    cap_c = EC_CAPACITY_FACTOR * n_ctx // n_exp
    h, slot_row, aff_row = _route(x, modsel, g, w_router, n_ctx=n_ctx, cap_c=cap_c, cap_l=cap_l,
                                  tr=tm)
    xs = _gather(h, slot_row, n_ctx=n_ctx, cap_c=cap_c, cap_l=cap_l)
    m = xs.shape[2]
    ys = _ffn(xs.reshape(n_exp, b * m, d), w_gate, w_up, w_down, layer=layer)
    return _combine(x, ys.reshape(n_exp, b, m, d), jnp.swapaxes(slot_row, 1, 2),
                    jnp.swapaxes(aff_row, 1, 2), modsel, final_g,
                    nc=n_ctx // tm, cap_c=cap_c, cap_l=cap_l, tm=tm)


def kernel(x, c, ctx, c_ctx, w_mod, b_mod, norm_g, hg_w_in, hg_lb, hg_onorm, hg_w_out, gd_w_in,
           gd_conv, gd_a_log, gd_dt_bias, gd_onorm, gd_w_out, moe_router, moe_w_gate, moe_w_up,
           moe_w_down, final_g):
    bsz, n_lat, d = x.shape
    n_ctx = ctx.shape[1]
    depth = w_mod.shape[0]
    assert depth == 2, "layer 0 = HGRN2, layer 1 = gated DeltaNet"
    tm = math.gcd(math.gcd(n_ctx, n_lat), MAX_BLOCK_ROWS)
    assert tm % CHUNK == 0
    nc = n_ctx // tm
    o_dt = BF16

    rows = -(-(bsz + 1) // SUBLANES) * SUBLANES
    cond = jnp.zeros((rows, d), F32).at[:bsz].set(c).at[bsz].set(c_ctx)
    mod = _adaln(cond, w_mod, b_mod).reshape(depth, rows, 6, d)

    def modsel(i):
        ctx_mod = jnp.broadcast_to(mod[i, bsz][None], (bsz, 6, d))
        return jnp.stack([ctx_mod, mod[i, :bsz]], axis=1)

    ms = modsel(0)
    q, lf, v, gate = _hg_inproj(ctx, x, ms, norm_g[0, 0], hg_lb, hg_w_in[0], layer=0, nc=nc, tm=tm)
    o_f = _gla_scan(q, lf, v, rev=False, nc=nc, tm=tm, out_dtype=o_dt)
    o_b = _gla_scan(q, lf, v, rev=True, nc=nc, tm=tm, out_dtype=o_dt)
    x_all = _outproj(o_f, o_b, gate, hg_onorm[0], hg_w_out[0], ctx, x, ms, nc=nc, tm=tm)
    x_ctx, x_lat = _moe(x_all, ms, norm_g[0, 1], moe_router[0], moe_w_gate, moe_w_up, moe_w_down,
                        None, layer=0, n_ctx=n_ctx, n_lat=n_lat, tm=tm)

    ms = modsel(1)
    rpc = n_lat // GRID_W
    q, k, v, gate, ab = _gd_inproj(x_ctx, x_lat, ms, norm_g[1, 0], gd_w_in[0], gd_conv[0],
                                   gd_a_log[0], gd_dt_bias[0], tm=tm)
    o_f, o_b = _gdn_scan(q, k, v, ab, nc=nc, tm=tm, out_dtype=o_dt)
    x_lat = _outproj(o_f, o_b, gate, gd_onorm[0], gd_w_out[0], x_ctx,
                     x_lat.reshape(bsz, rpc, GRID_W, d), ms, nc=nc, tm=tm, rpc=rpc)
    x_lat = x_lat.reshape(bsz, n_lat, d)
    return _moe(x_lat, ms, norm_g[1, 1], moe_router[1], moe_w_gate, moe_w_up, moe_w_down, final_g,
                layer=1, n_ctx=0, n_lat=n_lat, tm=tm)
```

```python
import functools
import itertools
import math

import jax
import jax.numpy as jnp
from jax import lax
from jax.experimental import pallas as pl
from jax.experimental.pallas import tpu as pltpu

F32 = jnp.float32
BF16 = jnp.bfloat16
HIGHEST = lax.Precision.HIGHEST

EPS = 1e-6
LANES = 128
SUBLANES = 8
V7X_VMEM_BYTES = 64 * 1024 * 1024
VMEM_LIMIT = V7X_VMEM_BYTES * 7 // 8

HEAD_DK = 128
GD_DV = 256
GD_CONV_W = 5
GRID_W = 64
EC_CAPACITY_FACTOR = 2
CHUNK = 64
HALF = CHUNK // 2
GD_SOLVE_ROWS = 4 * CHUNK
EXP_CLAMP = 38.0
MAX_BLOCK_ROWS = 256
FFN_TILE = 256
FFN_MAX_ROWS = 2304
AFF_BITS = 31


def _cp(*sem):
    return pltpu.CompilerParams(dimension_semantics=sem, vmem_limit_bytes=VMEM_LIMIT)


def _dot(a, b):
    return jnp.dot(a, b, preferred_element_type=F32)


def _dot_nt(a, b):
    return lax.dot_general(a, b, (((1,), (1,)), ((), ())), preferred_element_type=F32)


def _dot_tn(a, b):
    return lax.dot_general(a, b, (((0,), (0,)), ((), ())), preferred_element_type=F32)


def _dot_hi(a, b):
    return jnp.dot(a, b, preferred_element_type=F32, precision=HIGHEST)


def _dot_nt_hi(a, b):
    return lax.dot_general(a, b, (((1,), (1,)), ((), ())), preferred_element_type=F32,
                           precision=HIGHEST)


def _silu(x):
    return x * jax.nn.sigmoid(x)


def _norm_mod(x, g, shift, scale):
    y = x * lax.rsqrt(jnp.mean(x * x, axis=-1, keepdims=True) + EPS)
    return y * (g * (1.0 + scale)) + shift


def _iota(shape, dim):
    return lax.broadcasted_iota(jnp.int32, shape, dim)


def _split3(x):
    hi = x.astype(BF16)
    r1 = x - hi.astype(F32)
    mid = r1.astype(BF16)
    lo = (r1 - mid.astype(F32)).astype(BF16)
    return [hi, mid, lo]


def _adaln_kernel(c_ref, w_ref, b_ref, o_ref):
    o_ref[0] = _dot_hi(_silu(c_ref[...]), w_ref[0]) + b_ref[0]


def _adaln(cond, w_mod, b_mod):
    depth, d, n = w_mod.shape
    r = cond.shape[0]
    tn = n // 12
    return pl.pallas_call(
        _adaln_kernel,
        grid=(depth, n // tn),
        in_specs=[pl.BlockSpec((r, d), lambda i, j: (0, 0)),
                  pl.BlockSpec((1, d, tn), lambda i, j: (i, 0, j)),
                  pl.BlockSpec((1, 1, tn), lambda i, j: (i, 0, j))],
        out_specs=pl.BlockSpec((1, r, tn), lambda i, j: (i, 0, j)),
        out_shape=jax.ShapeDtypeStruct((depth, r, n), F32),
        compiler_params=_cp("parallel", "parallel"),
        name="adaln",
    )(cond, w_mod, b_mod.reshape(depth, 1, n))


def _hg_inproj_kernel(c_ref, x_ref, mod_ref, g_ref, lb_ref, wq_ref, wf_ref, wi_ref, wg_ref,
                      q_ref, lf_ref, v_ref, gate_ref, *, layer, n_heads, nc):
    m = mod_ref[0, 0]
    xb = jnp.where(pl.program_id(1) < nc, c_ref[0], x_ref[0])
    hb = _norm_mod(xb, g_ref[...], m[0:1], m[1:2]).astype(BF16)
    fdim = wq_ref.shape[1]
    dk = fdim // n_heads
    dv = wi_ref.shape[1] // n_heads

    q = _silu(_dot(hb, wq_ref[...]))
    for h in range(n_heads):
        q_ref[0, h] = q[:, h * dk:(h + 1) * dk].astype(BF16)

    lbp = lb_ref[...]
    e = jnp.exp(lbp - jnp.max(lbp, axis=0))
    lb = jnp.sum(e[:layer + 1], axis=0) / jnp.sum(e, axis=0)

    zf = _dot(hb, wf_ref[...])
    for d in range(2):
        lbd = lb[d:d + 1]
        f = lbd + (1.0 - lbd) * jax.nn.sigmoid(zf[:, d * fdim:(d + 1) * fdim])
        lf = jnp.log(f)
        for h in range(n_heads):
            lf_ref[d, 0, h] = lf[:, h * dk:(h + 1) * dk]

    v = _dot(hb, wi_ref[...])
    for h in range(n_heads):
        v_ref[0, h] = v[:, h * dv:(h + 1) * dv].astype(BF16)

    gate_ref[0] = _silu(_dot(hb, wg_ref[...])).astype(BF16)


def _hg_inproj(ctx, x, modsel, g, hg_lb, w_in, *, layer, nc, tm):
    b, n_lat, d = x.shape
    t = ctx.shape[1] + n_lat
    fdim = hg_lb.shape[-1]
    n_heads = fdim // HEAD_DK
    dv = d // n_heads
    wq = w_in[:, :fdim].astype(BF16)
    wf = w_in[:, fdim:3 * fdim].astype(BF16)
    wi = w_in[:, 3 * fdim:3 * fdim + d].astype(BF16)
    wg = w_in[:, 3 * fdim + d:].astype(BF16)
    full = lambda a: pl.BlockSpec(a.shape, lambda i, j: (0,) * a.ndim)
    g2 = g.reshape(1, d)
    return pl.pallas_call(
        functools.partial(_hg_inproj_kernel, layer=layer, n_heads=n_heads, nc=nc),
        grid=(b, t // tm),
        in_specs=[pl.BlockSpec((1, tm, d), lambda i, j: (i, jnp.minimum(j, nc - 1), 0)),
                  pl.BlockSpec((1, tm, d), lambda i, j: (i, jnp.maximum(j - nc, 0), 0)),
                  pl.BlockSpec((1, 1, 6, d), lambda i, j: (i, jnp.where(j >= nc, 1, 0), 0, 0)),
                  full(g2), full(hg_lb), full(wq), full(wf), full(wi), full(wg)],
        out_specs=[pl.BlockSpec((1, n_heads, tm, HEAD_DK), lambda i, j: (i, 0, j, 0)),
                   pl.BlockSpec((2, 1, n_heads, tm, HEAD_DK), lambda i, j: (0, i, 0, j, 0)),
                   pl.BlockSpec((1, n_heads, tm, dv), lambda i, j: (i, 0, j, 0)),
                   pl.BlockSpec((1, tm, d), lambda i, j: (i, j, 0))],
        out_shape=[jax.ShapeDtypeStruct((b, n_heads, t, HEAD_DK), BF16),
                   jax.ShapeDtypeStruct((2, b, n_heads, t, HEAD_DK), F32),
                   jax.ShapeDtypeStruct((b, n_heads, t, dv), BF16),
                   jax.ShapeDtypeStruct((b, t, d), BF16)],
        compiler_params=_cp("parallel", "parallel"),
        name="hg_inproj",
    )(ctx, x, modsel, g2, hg_lb, wq, wf, wi, wg)


def _scan_block_index(step, nc, nl, rev):
    if not rev:
        return step
    return jnp.where(step < nc, nc - 1 - step, nc + nl - 1 - (step - nc))


def _chunk_masks(rev):
    r = _iota((CHUNK, CHUNK), 0)
    c = _iota((CHUNK, CHUNK), 1)
    incl = (c >= r) if rev else (c <= r)
    strict = (c > r) if rev else (c < r)
    return incl, strict


def _gla_scan_kernel(*refs, n_heads, nck, tm):
    dirs = [(False,) + refs[0:3] + refs[6:7] + refs[8:12],
            (True,) + refs[3:6] + refs[7:8] + refs[12:16]]

    @pl.when(pl.program_id(1) == 0)
    def _():
        for d in dirs:
            d[5][...] = jnp.zeros(d[5].shape, F32)

    dk = refs[0].shape[-1]
    r = _iota((tm, tm), 0)
    c = _iota((tm, tm), 1)
    same_chunk = (r // CHUNK) == (c // CHUNK)
    same_half = (r // HALF) == (c // HALF)

    def per_rows(bh, row_of, span):
        return jnp.concatenate([jnp.broadcast_to(bh[row_of(x0):row_of(x0) + 1], (span, dk))
                                for x0 in range(0, tm, span)], axis=0)

    heads = range(n_heads)
    jobs = [(d, h) for h in heads for d in dirs]

    bhs = []
    for (rev, q_ref, lf_ref, *_), h in jobs:
        causal = (c >= r) if rev else (c <= r)
        lmb = jnp.where(same_chunk & causal, 1.0, 0.0).astype(BF16)
        g3 = _dot(lmb, jnp.concatenate(_split3(lf_ref[0, 0, h]), axis=1))
        bhs.append(g3[:, :dk] + g3[:, dk:2 * dk] + g3[:, 2 * dk:])

    decs, atts = [], []
    for ((rev, q_ref, lf_ref, v_ref, o_ref, st_ref, oin_ref, qc_ref, kh_ref), h), bh in zip(jobs,
                                                                                            bhs):
        q = q_ref[0, h].astype(F32)
        k = 1.0 - jnp.exp(lf_ref[0, 0, h])
        bls = [bh[ci * CHUNK:ci * CHUNK + 1] if rev
               else bh[(ci + 1) * CHUNK - 1:(ci + 1) * CHUNK] for ci in range(nck)]
        blc = jnp.concatenate([jnp.broadcast_to(x, (CHUNK, dk)) for x in bls], axis=0)
        qc_ref[h] = (q * jnp.exp(bh)).astype(BF16)
        kh_ref[h] = (k * jnp.exp(blc - bh)).astype(BF16)
        decs.append([jnp.exp(x) for x in bls])
        bm = per_rows(bh, lambda x0: x0 + HALF // 2, HALF)
        att_d = _dot_nt((q * jnp.exp(jnp.minimum(bh - bm, EXP_CLAMP))).astype(BF16),
                        (k * jnp.exp(jnp.minimum(bm - bh, EXP_CLAMP))).astype(BF16))
        be = per_rows(bh, lambda x0: x0 + (HALF if rev else HALF - 1), CHUNK)
        att_x = _dot_nt((q * jnp.exp(jnp.minimum(bh - be, EXP_CLAMP))).astype(BF16),
                        (k * jnp.exp(jnp.minimum(be - bh, EXP_CLAMP))).astype(BF16))
        atts.append((att_d, att_x))

    for ((rev, _, _, v_ref, _, _, oin_ref, _, _), h), (att_d, att_x) in zip(jobs, atts):
        causal = (c >= r) if rev else (c <= r)
        second = 0 if rev else 1
        mask_x = same_chunk & ((r // HALF) % 2 == second) & ((c // HALF) % 2 == 1 - second)
        att = jnp.where(same_half & causal, att_d, 0.0) + jnp.where(mask_x, att_x, 0.0)
        oin_ref[h] = _dot(att.astype(BF16), v_ref[0, h])

    for step in range(nck):
        for ((rev, _, _, v_ref, o_ref, st_ref, oin_ref, qc_ref, kh_ref), h), dec in zip(jobs, decs):
            ci = nck - 1 - step if rev else step
            rows = pl.ds(ci * CHUNK, CHUNK)
            st = st_ref[h]
            o = oin_ref[h, rows, :] + _dot_nt(qc_ref[h, rows, :], st.astype(BF16))
            o_ref[0, h, rows, :] = o.astype(o_ref.dtype)
            st_ref[h] = st * dec[ci] + _dot_tn(v_ref[0, h, rows, :], kh_ref[h, rows, :])


def _gla_scan(q, lf, v, *, nc, tm, out_dtype):
    b, n_heads, t, dk = q.shape
    dv = v.shape[-1]
    nblk = t // tm
    nl = nblk - nc
    nck = tm // CHUNK

    def specs(rev):
        d = 1 if rev else 0
        blk = lambda s: _scan_block_index(s, nc, nl, rev)
        return [pl.BlockSpec((1, n_heads, tm, dk), lambda i, s: (i, 0, blk(s), 0)),
                pl.BlockSpec((1, 1, n_heads, tm, dk), lambda i, s: (d, i, 0, blk(s), 0)),
                pl.BlockSpec((1, n_heads, tm, dv), lambda i, s: (i, 0, blk(s), 0))]

    scratch = [pltpu.VMEM((n_heads, dv, dk), F32),
               pltpu.VMEM((n_heads, tm, dv), F32),
               pltpu.VMEM((n_heads, tm, dk), BF16),
               pltpu.VMEM((n_heads, tm, dk), BF16)]
    o_shape = jax.ShapeDtypeStruct((b, n_heads, t, dv), out_dtype)
    return pl.pallas_call(
        functools.partial(_gla_scan_kernel, n_heads=n_heads, nck=nck, tm=tm),
        grid=(b, nblk),
        in_specs=specs(False) + specs(True),
        out_specs=[specs(False)[2], specs(True)[2]],
        out_shape=[o_shape, o_shape],
        scratch_shapes=scratch + scratch,
        compiler_params=_cp("parallel", "arbitrary"),
        name="gla_scan",
    )(q, lf, v, q, lf, v)


def _outproj_kernel(of_ref, ob_ref, gate_ref, gain_ref, w_ref, c_ref, x_ref, mod_ref, o_ref, y_ref,
                    *, n_heads, nc, rpc):
    dv = of_ref.shape[-1]
    for h in range(n_heads):
        o = of_ref[0, h].astype(F32) + ob_ref[0, h].astype(F32)
        cs = slice(h * dv, (h + 1) * dv)
        o = o * lax.rsqrt(jnp.mean(o * o, axis=-1, keepdims=True) + EPS) * gain_ref[:, cs]
        y_ref[:, cs] = (o * gate_ref[0, :, cs].astype(F32)).astype(BF16)
    y = mod_ref[0, 0][2:3] * _dot(y_ref[...], w_ref[...])
    if not rpc:
        o_ref[0] = jnp.where(pl.program_id(1) < nc, c_ref[0], x_ref[0]) + y
    else:
        for j in range(y.shape[0] // rpc):
            o_ref[0, :, j, :] = x_ref[0, :, j, :] + y[j * rpc:(j + 1) * rpc]


def _outproj(o_f, o_b, gate, gain, w_out, x_ctx, x_lat, modsel, *, nc, tm, rpc=0):
    b, n_heads, t, dv = o_f.shape
    d = w_out.shape[-1]
    hv = n_heads * dv
    off = nc if rpc else 0
    nblk = t // tm - off
    seg = (lambda j: 1) if rpc else (lambda j: jnp.where(j >= nc, 1, 0))
    w = w_out.astype(BF16)
    gain2 = gain.reshape(1, hv)
    if rpc:
        x_spec = o_spec = pl.BlockSpec((1, rpc, tm // rpc, d), lambda i, j: (i, 0, j, 0))
        o_shape = x_lat.shape
    else:
        x_spec = pl.BlockSpec((1, tm, d), lambda i, j: (i, jnp.maximum(j - nc, 0), 0))
        o_spec = pl.BlockSpec((1, tm, d), lambda i, j: (i, j, 0))
        o_shape = (b, t, d)
    return pl.pallas_call(
        functools.partial(_outproj_kernel, n_heads=n_heads, nc=nc, rpc=rpc),
        grid=(b, nblk),
        in_specs=[pl.BlockSpec((1, n_heads, tm, dv), lambda i, j: (i, 0, j + off, 0)),
                  pl.BlockSpec((1, n_heads, tm, dv), lambda i, j: (i, 0, j + off, 0)),
                  pl.BlockSpec((1, tm, hv), lambda i, j: (i, j + off, 0)),
                  pl.BlockSpec((1, hv), lambda i, j: (0, 0)),
                  pl.BlockSpec((hv, d), lambda i, j: (0, 0)),
                  pl.BlockSpec((1, tm, d), lambda i, j: (i, jnp.minimum(j, nc - 1), 0)),
                  x_spec,
                  pl.BlockSpec((1, 1, 6, d), lambda i, j: (i, seg(j), 0, 0))],
        out_specs=o_spec,
        out_shape=jax.ShapeDtypeStruct(o_shape, F32),
        scratch_shapes=[pltpu.VMEM((tm, hv), BF16)],
        compiler_params=_cp("parallel", "parallel"),
        name="outproj",
    )(o_f, o_b, gate, gain2, w, x_ctx, x_lat, modsel)


def _gd_inproj_kernel(cp_ref, cc_ref, cn_ref, lp_ref, lc_ref, ln_ref, mod_ref, g_ref, wqkv_ref,
                      wg_ref, wab_ref, cw_ref, alog_ref, dtb_ref, q_ref, k_ref, v_ref, gate_ref,
                      ab_ref, xs_ref, ys_ref, *, nc, nblk, n_heads, tm, cw, rpc):
    t = pl.program_id(1)
    first = (t == 0) | (t == nc)
    last = (t == nc - 1) | (t == nblk - 1)
    m = mod_ref[0, 0]
    halo = SUBLANES
    nlt = xs_ref.shape[0]
    d = nlt * LANES
    tile = lambda c: slice(c * LANES, (c + 1) * LANES)

    @pl.when(t < nc)
    def _():
        for c in range(nlt):
            xs_ref[c, 0:halo] = cp_ref[0, :, tile(c)]
            xs_ref[c, halo:halo + tm] = cc_ref[0, :, tile(c)]
            xs_ref[c, halo + tm:] = cn_ref[0, :, tile(c)]

    @pl.when(t >= nc)
    def _():
        cpb = tm // rpc
        for c in range(nlt):
            xs_ref[c, 0:halo] = lp_ref[0, :, SUBLANES - 1, tile(c)]
            for j in range(cpb):
                xs_ref[c, halo + j * rpc:halo + (j + 1) * rpc] = lc_ref[0, :, j, tile(c)]
            xs_ref[c, halo + tm:] = ln_ref[0, :, 0, tile(c)]

    xc = jnp.concatenate([xs_ref[c, halo:halo + tm] for c in range(nlt)], axis=1)
    hc = _norm_mod(xc, g_ref[...], m[0:1], m[1:2]).astype(BF16)

    rows = tm + 2 * halo
    ni = rows // SUBLANES
    xp = jnp.concatenate(
        [jnp.concatenate([xs_ref[c, pl.ds(i, SUBLANES, stride=ni), :] for i in range(ni)], axis=0)
         for c in range(nlt)], axis=1)
    he = _norm_mod(xp, g_ref[...], m[0:1], m[1:2]).astype(BF16)
    row = _iota((rows, 1), 0)
    tok = ni * (row % SUBLANES) + row // SUBLANES
    valid = ((tok >= halo) | jnp.logical_not(first)) & ((tok < tm + halo) | jnp.logical_not(last))

    gate_ref[0] = _silu(_dot(hc, wg_ref[...])).astype(BF16)

    zab = _dot(hc, wab_ref[...])
    lane = _iota(zab.shape, 1)
    loga = -jnp.exp(alog_ref[...]) * jax.nn.softplus(zab + dtb_ref[...])
    ab_ref[0] = jnp.where(lane < 2 * n_heads, loga, jax.nn.sigmoid(zab))

    qk = n_heads * HEAD_DK
    nchan = wqkv_ref.shape[1]
    pad = GD_CONV_W // 2
    for cc in range(nchan // cw):
        c0 = cc * cw
        z = jnp.where(valid, _dot(he, wqkv_ref[:, c0:c0 + cw]), 0.0)
        z3 = z.reshape(ni, SUBLANES, cw)
        zlo = pltpu.roll(z3[ni - pad:], 1, 1)
        zhi = pltpu.roll(z3[:pad], SUBLANES - 1, 1)
        taps = [cw_ref[j:j + 1, c0:c0 + cw] for j in range(GD_CONV_W)]

        def group(i):
            return zlo[i + pad] if i < 0 else (zhi[i - ni] if i >= ni else z3[i])

        def finish(acc, i0):
            u = _silu(acc)
            if c0 < 2 * qk:
                scale = HEAD_DK ** -0.5 if c0 < qk else 1.0
                parts = []
                for j in range(cw // HEAD_DK):
                    tt = u[:, :, j * HEAD_DK:(j + 1) * HEAD_DK]
                    parts.append(tt * (lax.rsqrt(jnp.sum(tt * tt, axis=-1, keepdims=True) + EPS)
                                       * scale))
                u = jnp.concatenate(parts, axis=-1)
            for c in range(cw // LANES):
                for i in range(u.shape[0]):
                    ys_ref[c, pl.ds(i0 + i, SUBLANES, stride=ni), :] = u[i][:, tile(c)]

        inner = ni - 2 * pad
        acc = taps[0] * z3[0:inner]
        for j in range(1, GD_CONV_W):
            acc = acc + taps[j] * z3[j:j + inner]
        finish(acc, pad)
        for i0 in (0, ni - pad):
            edge = jnp.stack([sum(taps[j][0] * group(i + j - pad) for j in range(GD_CONV_W))
                              for i in range(i0, i0 + pad)], axis=0)
            finish(edge, i0)

        def centre(lane0, width):
            return jnp.concatenate([ys_ref[c, halo:halo + tm] for c in
                                    range(lane0 // LANES, (lane0 + width) // LANES)], axis=1)

        if c0 < 2 * qk:
            dst, base = (q_ref, c0) if c0 < qk else (k_ref, c0 - qk)
            for j in range(cw // HEAD_DK):
                dst[0, base // HEAD_DK + j] = centre(j * HEAD_DK, HEAD_DK).astype(BF16)
        else:
            base = c0 - 2 * qk
            for j in range(cw // GD_DV):
                v_ref[0, base // GD_DV + j] = centre(j * GD_DV, GD_DV).astype(BF16)


def _gd_inproj(x_ctx, x_lat, modsel, g, w_in, conv_w, a_log, dt_bias, *, tm):
    b, n_ctx, d = x_ctx.shape
    n_lat = x_lat.shape[1]
    rpc = n_lat // GRID_W
    assert rpc % SUBLANES == 0 and tm % rpc == 0 and n_ctx % tm == 0 and n_ctx > 0
    t = n_ctx + n_lat
    nc = n_ctx // tm
    cpb = tm // rpc
    assert cpb % SUBLANES == 0, "halo blocks take one 8-column group of the grid"
    lat_v = x_lat.reshape(b, rpc, GRID_W, d)
    n_heads = a_log.shape[-1]
    qk = n_heads * HEAD_DK
    vd = n_heads * GD_DV
    nchan = 2 * qk + vd
    nblk = t // tm
    cw = min(512, qk)
    wqkv = w_in[:, :nchan].astype(BF16)
    wg = w_in[:, nchan:nchan + vd].astype(BF16)
    wab = jnp.pad(w_in[:, nchan + vd:], ((0, 0), (0, LANES - 4 * n_heads))).astype(BF16)
    alog = jnp.pad(a_log.reshape(1, 2 * n_heads), ((0, 0), (0, LANES - 2 * n_heads)))
    dtb = jnp.pad(dt_bias.reshape(1, 2 * n_heads), ((0, 0), (0, LANES - 2 * n_heads)))
    g2 = g.reshape(1, d)
    full = lambda a: pl.BlockSpec(a.shape, lambda i, j: (0,) * a.ndim)
    spb = tm // SUBLANES
    last_slab = n_ctx // SUBLANES - 1
    gpb = cpb // SUBLANES
    last_cg = GRID_W // SUBLANES - 1
    return pl.pallas_call(
        functools.partial(_gd_inproj_kernel, nc=nc, nblk=nblk, n_heads=n_heads, tm=tm, cw=cw,
                          rpc=rpc),
        grid=(b, nblk),
        in_specs=[pl.BlockSpec((1, SUBLANES, d),
                               lambda i, j: (i, jnp.clip(j * spb - 1, 0, last_slab), 0)),
                  pl.BlockSpec((1, tm, d), lambda i, j: (i, jnp.minimum(j, nc - 1), 0)),
                  pl.BlockSpec((1, SUBLANES, d),
                               lambda i, j: (i, jnp.clip((j + 1) * spb, 0, last_slab), 0)),
                  pl.BlockSpec((1, SUBLANES, SUBLANES, d),
                               lambda i, j: (i, rpc // SUBLANES - 1,
                                             jnp.clip((j - nc) * gpb - 1, 0, last_cg), 0)),
                  pl.BlockSpec((1, rpc, cpb, d), lambda i, j: (i, 0, jnp.maximum(j - nc, 0), 0)),
                  pl.BlockSpec((1, SUBLANES, SUBLANES, d),
                               lambda i, j: (i, 0, jnp.clip((j - nc + 1) * gpb, 0, last_cg), 0)),
                  pl.BlockSpec((1, 1, 6, d), lambda i, j: (i, jnp.where(j >= nc, 1, 0), 0, 0)),
                  full(g2), full(wqkv), full(wg), full(wab), full(conv_w), full(alog), full(dtb)],
        out_specs=[pl.BlockSpec((1, n_heads, tm, HEAD_DK), lambda i, j: (i, 0, j, 0)),
                   pl.BlockSpec((1, n_heads, tm, HEAD_DK), lambda i, j: (i, 0, j, 0)),
                   pl.BlockSpec((1, n_heads, tm, GD_DV), lambda i, j: (i, 0, j, 0)),
                   pl.BlockSpec((1, tm, vd), lambda i, j: (i, j, 0)),
                   pl.BlockSpec((1, tm, LANES), lambda i, j: (i, j, 0))],
        out_shape=[jax.ShapeDtypeStruct((b, n_heads, t, HEAD_DK), BF16),
                   jax.ShapeDtypeStruct((b, n_heads, t, HEAD_DK), BF16),
                   jax.ShapeDtypeStruct((b, n_heads, t, GD_DV), BF16),
                   jax.ShapeDtypeStruct((b, t, vd), BF16),
                   jax.ShapeDtypeStruct((b, t, LANES), F32)],
        scratch_shapes=[pltpu.VMEM((d // LANES, tm + 2 * SUBLANES, LANES), F32),
                        pltpu.VMEM((cw // LANES, tm + 2 * SUBLANES, LANES), F32)],
        compiler_params=_cp("parallel", "parallel"),
        name="gd_inproj",
    )(x_ctx, x_ctx, x_ctx, lat_v, lat_v, lat_v, modsel, g2, wqkv, wg, wab, conv_w, alog, dtb)


def _neumann_inverses(mats, eye):
    n = eye.shape[0]
    ts = [eye - a for a in mats]
    ps = []
    for a in mats:
        ab = a.astype(BF16)
        ps.append(_dot(ab, ab))
    yield
    lvl = 2
    while lvl < CHUNK:
        for i in range(len(mats)):
            pb = ps[i].astype(BF16)
            tb = ts[i].astype(BF16)
            if 2 * lvl >= CHUNK:
                ts[i] = ts[i] + _dot(pb, tb)
            else:
                out = _dot(pb, jnp.concatenate([pb, tb], axis=1))
                ps[i] = out[:, :n]
                ts[i] = ts[i] + out[:, n:]
        yield
        lvl *= 2
    return ts


def _gdn_scan_block(q_ref, k_ref, v_ref, ab_ref, o_ref, s_ref, u_ref, l1_ref, l2_ref,
                    *, rev, n_heads, nck, tm):
    d = 1 if rev else 0

    def masks(n):
        r = _iota((n, n), 0)
        c = _iota((n, n), 1)
        same = (r // CHUNK) == (c // CHUNK)
        return (same & ((c >= r) if rev else (c <= r)), same & ((c > r) if rev else (c < r)),
                jnp.where(r == c, 1.0, 0.0))

    pb = min(tm, GD_SOLVE_ROWS)
    incl, strict, eye = masks(pb)
    dv = v_ref.shape[-1]

    ab = ab_ref[0]
    g3 = _dot(jnp.where(masks(tm)[0], 1.0, 0.0).astype(BF16),
              jnp.concatenate(_split3(ab), axis=1))
    gcol = g3[:, :LANES] + g3[:, LANES:2 * LANES] + g3[:, 2 * LANES:]
    eye_l = jnp.where(_iota((LANES, LANES), 0) == _iota((LANES, LANES), 1), 1.0, 0.0).astype(BF16)
    gr3 = _dot_nt(eye_l, jnp.concatenate(_split3(gcol), axis=0))
    grow = gr3[:, :tm] + gr3[:, tm:2 * tm] + gr3[:, 2 * tm:]

    incl_c, _ = _chunk_masks(rev)
    wide = lambda x, n: jnp.concatenate([x] * (n // LANES), axis=1) if n > LANES else x[:, :n]
    heads = range(n_heads)
    cas = [d * n_heads + h for h in heads]
    gcbs = [jnp.broadcast_to(gcol[:, ca:ca + 1], (tm, LANES)) for ca in cas]
    bcbs = [jnp.broadcast_to(ab[:, 2 * n_heads + ca:2 * n_heads + ca + 1], (tm, LANES))
            for ca in cas]

    mats = []
    for h in heads:
        k = k_ref[0, h]
        for p0 in range(0, tm, pb):
            gam = jnp.where(incl, jnp.exp(jnp.minimum(
                wide(gcbs[h][p0:p0 + pb], pb) - grow[cas[h]:cas[h] + 1, p0:p0 + pb], 0.0)), 0.0)
            kp = k[p0:p0 + pb]
            mats.append(jnp.where(strict, wide(bcbs[h][p0:p0 + pb], pb) * _dot_nt(kp, kp) * gam,
                                  0.0))
    yield
    tinvs = yield from _neumann_inverses(mats, eye)

    uws = []
    for h in heads:
        kf = k_ref[0, h].astype(F32)
        rhs = jnp.concatenate(
            [(wide(bcbs[h], dv) * v_ref[0, h].astype(F32)).astype(BF16),
             (bcbs[h] * jnp.exp(gcbs[h]) * kf).astype(BF16)], axis=1)
        uws.append(jnp.concatenate(
            [_dot(tinvs[h * (tm // pb) + i].astype(BF16), rhs[i * pb:(i + 1) * pb])
             for i in range(tm // pb)], axis=0))
    yield

    egl = {}
    for h in heads:
        ca, gcb, uw = cas[h], gcbs[h], uws[h]
        q = q_ref[0, h]
        k = k_ref[0, h]
        kf = k.astype(F32)
        egb = jnp.exp(gcb)
        u_ref[h] = uw[:, :dv]
        wb = uw[:, dv:].astype(BF16)
        qg = (q.astype(F32) * egb).astype(BF16)
        gls = [gcb[ci * CHUNK:ci * CHUNK + 1] if rev
               else gcb[(ci + 1) * CHUNK - 1:(ci + 1) * CHUNK] for ci in range(nck)]
        glb = jnp.concatenate([jnp.broadcast_to(g, (CHUNK, LANES)) for g in gls], axis=0)
        kd = kf * jnp.exp(glb - gcb)
        egl[h] = [jnp.exp(g[:, 0:1]) for g in gls]
        for ci in range(nck):
            c0 = ci * CHUNK
            l1_ref[h, ci, :CHUNK] = wb[c0:c0 + CHUNK]
            l1_ref[h, ci, CHUNK:] = qg[c0:c0 + CHUNK]
            gam_c = jnp.where(incl_c, jnp.exp(jnp.minimum(
                gcb[c0:c0 + CHUNK, :CHUNK] - grow[ca:ca + 1, c0:c0 + CHUNK], 0.0)), 0.0)
            qk_c = _dot_nt(q[c0:c0 + CHUNK], k[c0:c0 + CHUNK])
            l2_ref[h, ci, :CHUNK] = (qk_c * gam_c).astype(BF16)
            l2_ref[h, ci, CHUNK:] = kd[c0:c0 + CHUNK].T.astype(BF16)

    for step in range(nck):
        yield
        ci = nck - 1 - step if rev else step
        rows = pl.ds(ci * CHUNK, CHUNK)
        r1s = [_dot(l1_ref[h, ci], s_ref[h].astype(BF16)) for h in heads]
        weffs = [(u_ref[h, rows, :] - r1s[h][:CHUNK]).astype(BF16) for h in heads]
        yield
        r2s = [_dot(l2_ref[h, ci], weffs[h]) for h in heads]
        for h in heads:
            o_ref[0, h, rows, :] = (r2s[h][:CHUNK] + r1s[h][CHUNK:]).astype(o_ref.dtype)
            s_ref[h] = s_ref[h] * egl[h][ci] + r2s[h][CHUNK:]


def _gdn_scan_kernel(*refs, n_heads, nck, tm):
    fwd = refs[0:4] + refs[8:9] + refs[10:14]
    bwd = refs[4:8] + refs[9:10] + refs[14:18]

    @pl.when(pl.program_id(1) == 0)
    def _():
        for s_ref in (fwd[5], bwd[5]):
            s_ref[...] = jnp.zeros(s_ref.shape, F32)

    blocks = [_gdn_scan_block(*fwd, rev=False, n_heads=n_heads, nck=nck, tm=tm),
              _gdn_scan_block(*bwd, rev=True, n_heads=n_heads, nck=nck, tm=tm)]
    for _ in itertools.zip_longest(*blocks):
        pass


def _gdn_scan(q, k, v, ab, *, nc, tm, out_dtype):
    b, n_heads, t, dk = q.shape
    dv = v.shape[-1]
    nblk = t // tm
    nl = nblk - nc
    nck = tm // CHUNK

    def specs(rev):
        blk = lambda s: _scan_block_index(s, nc, nl, rev)
        return [pl.BlockSpec((1, n_heads, tm, dk), lambda i, s: (i, 0, blk(s), 0)),
                pl.BlockSpec((1, n_heads, tm, dk), lambda i, s: (i, 0, blk(s), 0)),
                pl.BlockSpec((1, n_heads, tm, dv), lambda i, s: (i, 0, blk(s), 0)),
                pl.BlockSpec((1, tm, LANES), lambda i, s: (i, blk(s), 0))]

    scratch = [pltpu.VMEM((n_heads, dk, dv), F32),
               pltpu.VMEM((n_heads, tm, dv), F32),
               pltpu.VMEM((n_heads, nck, 2 * CHUNK, dk), BF16),
               pltpu.VMEM((n_heads, nck, CHUNK + dk, CHUNK), BF16)]
    o_shape = jax.ShapeDtypeStruct((b, n_heads, t, dv), out_dtype)
    return pl.pallas_call(
        functools.partial(_gdn_scan_kernel, n_heads=n_heads, nck=nck, tm=tm),
        grid=(b, nblk),
        in_specs=specs(False) + specs(True),
        out_specs=[specs(False)[2], specs(True)[2]],
        out_shape=[o_shape, o_shape],
        scratch_shapes=scratch + scratch,
        compiler_params=_cp("parallel", "arbitrary"),
        name="gdn_scan",
    )(q, k, v, ab, q, k, v, ab)


def _route_kernel(x_ref, mod_ref, g_ref, wrt_ref, h_ref, slot_ref, aff_ref, lg_ref,
                  *, n_ctx, cap_c, cap_l, tr):
    n_exp, t = lg_ref.shape
    for rt in range(t // tr):
        m = mod_ref[0, 0 if rt * tr < n_ctx else 1]
        rows = slice(rt * tr, (rt + 1) * tr)
        h = _norm_mod(x_ref[0, rows, :], g_ref[...], m[3:4], m[4:5])
        h_ref[0, rows, :] = h.astype(BF16)
        lg_ref[:, rows] = _dot_nt_hi(wrt_ref[...], h)
    lg = lg_ref[...]
    e = jnp.exp(lg - jnp.max(lg, axis=0, keepdims=True))
    aff = e / jnp.sum(e, axis=0, keepdims=True)
    aff_ref[0] = aff
    bits = lax.bitcast_convert_type(aff, jnp.int32)
    lane = _iota((n_exp, t), 1)

    if n_ctx:
        is_ctx = lane < n_ctx
        regions = [(is_ctx, cap_c), (jnp.logical_not(is_ctx), cap_l)]
    else:
        regions = [(None, cap_l)]

    def count(pred, mask):
        p = pred if mask is None else (pred & mask)
        return jnp.sum(jnp.where(p, 1.0, 0.0), axis=1, keepdims=True)

    def thr_body(i, thrs):
        bit = lax.shift_left(jnp.int32(1), AFF_BITS - 1 - i)
        out = []
        for (mask, cap), thr in zip(regions, thrs):
            cand = thr | bit
            out.append(jnp.where(count(bits >= cand, mask) >= cap, cand, thr))
        return tuple(out)

    zero = jnp.zeros((n_exp, 1), jnp.int32)
    thrs = lax.fori_loop(0, AFF_BITS, thr_body, tuple(zero for _ in regions))

    idx_bits = t.bit_length()
    sel = None
    for (mask, cap), thr in zip(regions, thrs):
        gt = bits > thr
        tie = bits == thr
        need = cap - count(gt, mask)

        def j_body(i, j, tie=tie, mask=mask, need=need):
            cand = j | lax.shift_left(jnp.int32(1), idx_bits - 1 - i)
            return jnp.where(count(tie & (lane < cand), mask) <= need, cand, j)

        jmax = lax.fori_loop(0, idx_bits, j_body, zero)
        s = gt | (tie & (lane < jmax))
        if mask is not None:
            s = s & mask
        sel = s if sel is None else (sel | s)

    lt = LANES if t % LANES == 0 else CHUNK
    ut = jnp.where(_iota((lt, lt), 0) < _iota((lt, lt), 1), 1.0, 0.0).astype(BF16)
    self = jnp.where(sel, 1.0, 0.0)
    run = jnp.zeros((n_exp, 1), F32)
    pres = []
    for i in range(t // lt):
        tile = self[:, i * lt:(i + 1) * lt]
        pres.append(_dot(tile.astype(BF16), ut) + run)
        run = run + jnp.sum(tile, axis=1, keepdims=True)
    pre = jnp.concatenate(pres, axis=1)
    if n_ctx:
        slot = jnp.where(is_ctx, pre + cap_l, pre - cap_c)
    else:
        slot = pre
    slot_ref[0] = jnp.where(sel, slot, -1.0)


def _route(x, modsel, g, w_router, *, n_ctx, cap_c, cap_l, tr):
    b, t, d = x.shape
    n_exp = w_router.shape[-1]
    wrt = w_router.T
    g2 = g.reshape(1, d)
    return pl.pallas_call(
        functools.partial(_route_kernel, n_ctx=n_ctx, cap_c=cap_c, cap_l=cap_l, tr=tr),
        grid=(b,),
        in_specs=[pl.BlockSpec((1, t, d), lambda i: (i, 0, 0)),
                  pl.BlockSpec((1, 2, 6, d), lambda i: (i, 0, 0, 0)),
                  pl.BlockSpec((1, d), lambda i: (0, 0)),
                  pl.BlockSpec((n_exp, d), lambda i: (0, 0))],
        out_specs=[pl.BlockSpec((1, t, d), lambda i: (i, 0, 0)),
                   pl.BlockSpec((1, n_exp, t), lambda i: (i, 0, 0)),
                   pl.BlockSpec((1, n_exp, t), lambda i: (i, 0, 0))],
        out_shape=[jax.ShapeDtypeStruct((b, t, d), BF16),
                   jax.ShapeDtypeStruct((b, n_exp, t), F32),
                   jax.ShapeDtypeStruct((b, n_exp, t), F32)],
        scratch_shapes=[pltpu.VMEM((n_exp, t), F32)],
        compiler_params=_cp("parallel"),
        name="moe_route",
    )(x, modsel, g2, wrt)


def _gather_kernel(slot_ref, h_ref, x_ref, *, n_ctx, cap_c, cap_l):
    slot = slot_ref[0, 0]
    t = slot.shape[1]
    sl = slot[:, n_ctx:]
    p = jnp.where(sl == _iota((cap_l, t - n_ctx), 0).astype(F32), 1.0, 0.0).astype(BF16)
    x_ref[0, 0, 0:cap_l, :] = _dot(p, h_ref[0, n_ctx:, :]).astype(BF16)
    if n_ctx:
        sc = slot[:, :n_ctx] - float(cap_l)
        p = jnp.where(sc == _iota((cap_c, n_ctx), 0).astype(F32), 1.0, 0.0).astype(BF16)
        x_ref[0, 0, cap_l:, :] = _dot(p, h_ref[0, :n_ctx, :]).astype(BF16)


def _gather(h, slot_row, *, n_ctx, cap_c, cap_l):
    b, t, d = h.shape
    n_exp = slot_row.shape[1]
    m = cap_l + (cap_c if n_ctx else 0)
    return pl.pallas_call(
        functools.partial(_gather_kernel, n_ctx=n_ctx, cap_c=cap_c, cap_l=cap_l),
        grid=(b, n_exp),
        in_specs=[pl.BlockSpec((1, 1, 1, t), lambda i, e: (i, e, 0, 0)),
                  pl.BlockSpec((1, t, d), lambda i, e: (i, 0, 0))],
        out_specs=pl.BlockSpec((1, 1, m, d), lambda i, e: (e, i, 0, 0)),
        out_shape=jax.ShapeDtypeStruct((n_exp, b, m, d), BF16),
        compiler_params=_cp("parallel", "parallel"),
        name="moe_gather",
    )(slot_row.reshape(b, n_exp, 1, t), h)


def _ffn_kernel(x_ref, wg_ref, wu_ref, wd_ref, y_ref, acc_ref, wgb_ref, wub_ref, wdb_ref, *, tr):
    ft = pl.program_id(1)

    @pl.when(ft == 0)
    def _():
        acc_ref[...] = jnp.zeros(acc_ref.shape, F32)

    wgb_ref[...] = wg_ref[...].astype(BF16)
    wub_ref[...] = wu_ref[...].astype(BF16)
    wdb_ref[...] = wd_ref[...].astype(BF16)

    def body(i, carry):
        rows = pl.ds(pl.multiple_of(i * tr, tr), tr)
        x = x_ref[0, rows, :]
        hid = (_silu(_dot(x, wgb_ref[...])) * _dot(x, wub_ref[...])).astype(BF16)
        acc_ref[rows, :] += _dot(hid, wdb_ref[...])
        return carry

    lax.fori_loop(0, x_ref.shape[1] // tr, body, 0)

    @pl.when(ft == pl.num_programs(1) - 1)
    def _():
        y_ref[0] = acc_ref[...].astype(y_ref.dtype)


def _ffn(xs, w_gate, w_up, w_down, *, layer):
    n_exp, bm, d = xs.shape
    f = w_gate.shape[-1]
    tf = FFN_TILE if f % FFN_TILE == 0 else f
    tr = bm // (-(-bm // FFN_MAX_ROWS))
    return pl.pallas_call(
        functools.partial(_ffn_kernel, tr=tr),
        grid=(n_exp, f // tf),
        in_specs=[pl.BlockSpec((1, bm, d), lambda e, j: (e, 0, 0)),
                  pl.BlockSpec((None, None, d, tf), lambda e, j: (layer, e, 0, j)),
                  pl.BlockSpec((None, None, d, tf), lambda e, j: (layer, e, 0, j)),
                  pl.BlockSpec((None, None, tf, d), lambda e, j: (layer, e, j, 0))],
        out_specs=pl.BlockSpec((1, bm, d), lambda e, j: (e, 0, 0)),
        out_shape=jax.ShapeDtypeStruct((n_exp, bm, d), BF16),
        scratch_shapes=[pltpu.VMEM((bm, d), F32),
                        pltpu.VMEM((d, tf), BF16),
                        pltpu.VMEM((d, tf), BF16),
                        pltpu.VMEM((tf, d), BF16)],
        compiler_params=_cp("parallel", "arbitrary"),
        name="moe_ffn",
    )(xs, w_gate, w_up, w_down)


def _combine_kernel(*refs, nc, cap_c, cap_l, final):
    x_ref, ys_ref, sc_ref, ac_ref, mod_ref = refs[:5]
    fg_ref = refs[5] if final else None
    outs = refs[5 + bool(final):-1]
    acc_ref = refs[-1]
    n_exp = ys_ref.shape[0]
    tm = x_ref.shape[1]

    def run(k0, kk, o_ref):
        sc = sc_ref[0]
        ac = ac_ref[0]
        lane = (_iota((tm, kk), 1) + k0).astype(F32)
        acc_ref[...] = jnp.zeros(acc_ref.shape, F32)
        for e in range(n_exp):
            pt = jnp.where(sc[:, e:e + 1] == lane, 1.0, 0.0).astype(BF16)
            acc_ref[...] += ac[:, e:e + 1] * _dot(pt, ys_ref[e, 0, k0:k0 + kk, :])
        x2 = x_ref[0] + mod_ref[0, 0][5:6] * acc_ref[...]
        if final:
            x2 = x2 * lax.rsqrt(jnp.mean(x2 * x2, axis=-1, keepdims=True) + EPS) * fg_ref[...]
        o_ref[0] = x2

    if nc:
        t = pl.program_id(1)

        @pl.when(t < nc)
        def _():
            run(cap_l, cap_c, outs[0])

        @pl.when(t >= nc)
        def _():
            run(0, cap_l, outs[1])
    else:
        run(0, cap_l, outs[0])


def _combine(x, ys, slot_col, aff_col, modsel, final_g, *, nc, cap_c, cap_l, tm):
    b, t, d = x.shape
    n_exp, _, m, _ = ys.shape
    seg = (lambda j: jnp.where(j >= nc, 1, 0)) if nc else (lambda j: 1)
    if nc:
        out_specs = [pl.BlockSpec((1, tm, d), lambda i, j: (i, jnp.minimum(j, nc - 1), 0)),
                     pl.BlockSpec((1, tm, d), lambda i, j: (i, jnp.maximum(j - nc, 0), 0))]
        out_shape = [jax.ShapeDtypeStruct((b, nc * tm, d), F32),
                     jax.ShapeDtypeStruct((b, t - nc * tm, d), F32)]
    else:
        out_specs = pl.BlockSpec((1, tm, d), lambda i, j: (i, j, 0))
        out_shape = jax.ShapeDtypeStruct((b, t, d), F32)
    in_specs = [pl.BlockSpec((1, tm, d), lambda i, j: (i, j, 0)),
                pl.BlockSpec((n_exp, 1, m, d), lambda i, j: (0, i, 0, 0)),
                pl.BlockSpec((1, tm, n_exp), lambda i, j: (i, j, 0)),
                pl.BlockSpec((1, tm, n_exp), lambda i, j: (i, j, 0)),
                pl.BlockSpec((1, 1, 6, d), lambda i, j: (i, seg(j), 0, 0))]
    args = [x, ys, slot_col, aff_col, modsel]
    if final_g is not None:
        in_specs.append(pl.BlockSpec((1, d), lambda i, j: (0, 0)))
        args.append(final_g.reshape(1, d))
    return pl.pallas_call(
        functools.partial(_combine_kernel, nc=nc, cap_c=cap_c, cap_l=cap_l,
                          final=final_g is not None),
        grid=(b, t // tm),
        in_specs=in_specs,
        out_specs=out_specs,
        out_shape=out_shape,
        scratch_shapes=[pltpu.VMEM((tm, d), F32)],
        compiler_params=_cp("parallel", "arbitrary"),
        name="moe_combine",
    )(*args)


def _moe(x, modsel, g, w_router, w_gate, w_up, w_down, final_g, *, layer, n_ctx, n_lat, tm):
    b, t, d = x.shape
    n_exp = w_router.shape[-1]
    cap_l = EC_CAPACITY_FACTOR * n_lat // n_exp
    cap_c = EC_CAPACITY_FACTOR * n_ctx // n_exp
    h, slot_row, aff_row = _route(x, modsel, g, w_router, n_ctx=n_ctx, cap_c=cap_c, cap_l=cap_l,
                                  tr=tm)
    xs = _gather(h, slot_row, n_ctx=n_ctx, cap_c=cap_c, cap_l=cap_l)
    m = xs.shape[2]
    ys = _ffn(xs.reshape(n_exp, b * m, d), w_gate, w_up, w_down, layer=layer)
    return _combine(x, ys.reshape(n_exp, b, m, d), jnp.swapaxes(slot_row, 1, 2),
                    jnp.swapaxes(aff_row, 1, 2), modsel, final_g,
                    nc=n_ctx // tm, cap_c=cap_c, cap_l=cap_l, tm=tm)


def kernel(x, c, ctx, c_ctx, w_mod, b_mod, norm_g, hg_w_in, hg_lb, hg_onorm, hg_w_out, gd_w_in,
           gd_conv, gd_a_log, gd_dt_bias, gd_onorm, gd_w_out, moe_router, moe_w_gate, moe_w_up,
           moe_w_down, final_g):
    bsz, n_lat, d = x.shape
    n_ctx = ctx.shape[1]
    depth = w_mod.shape[0]
    assert depth == 2, "layer 0 = HGRN2, layer 1 = gated DeltaNet"
    tm = math.gcd(math.gcd(n_ctx, n_lat), MAX_BLOCK_ROWS)
    assert tm % CHUNK == 0
    nc = n_ctx // tm
    o_dt = BF16

    rows = -(-(bsz + 1) // SUBLANES) * SUBLANES
    cond = jnp.zeros((rows, d), F32).at[:bsz].set(c).at[bsz].set(c_ctx)
    mod = _adaln(cond, w_mod, b_mod).reshape(depth, rows, 6, d)

    def modsel(i):
        ctx_mod = jnp.broadcast_to(mod[i, bsz][None], (bsz, 6, d))
        return jnp.stack([ctx_mod, mod[i, :bsz]], axis=1)

    ms = modsel(0)
    q, lf, v, gate = _hg_inproj(ctx, x, ms, norm_g[0, 0], hg_lb, hg_w_in[0], layer=0, nc=nc, tm=tm)
    o_f, o_b = _gla_scan(q, lf, v, nc=nc, tm=tm, out_dtype=o_dt)
    x_all = _outproj(o_f, o_b, gate, hg_onorm[0], hg_w_out[0], ctx, x, ms, nc=nc, tm=tm)
    x_ctx, x_lat = _moe(x_all, ms, norm_g[0, 1], moe_router[0], moe_w_gate, moe_w_up, moe_w_down,
                        None, layer=0, n_ctx=n_ctx, n_lat=n_lat, tm=tm)

    ms = modsel(1)
    rpc = n_lat // GRID_W
    q, k, v, gate, ab = _gd_inproj(x_ctx, x_lat, ms, norm_g[1, 0], gd_w_in[0], gd_conv[0],
                                   gd_a_log[0], gd_dt_bias[0], tm=tm)
    o_f, o_b = _gdn_scan(q, k, v, ab, nc=nc, tm=tm, out_dtype=o_dt)
    x_lat = _outproj(o_f, o_b, gate, gd_onorm[0], gd_w_out[0], x_ctx,
                     x_lat.reshape(bsz, rpc, GRID_W, d), ms, nc=nc, tm=tm, rpc=rpc)
    x_lat = x_lat.reshape(bsz, n_lat, d)
    return _moe(x_lat, ms, norm_g[1, 1], moe_router[1], moe_w_gate, moe_w_up, moe_w_down, final_g,
                layer=1, n_ctx=0, n_lat=n_lat, tm=tm)
```

```python
import functools
import itertools
import math

import jax
import jax.numpy as jnp
from jax import lax
from jax.experimental import pallas as pl
from jax.experimental.pallas import tpu as pltpu

F32 = jnp.float32
BF16 = jnp.bfloat16
HIGHEST = lax.Precision.HIGHEST

EPS = 1e-6
LANES = 128
SUBLANES = 8
V7X_VMEM_BYTES = 64 * 1024 * 1024
VMEM_LIMIT = V7X_VMEM_BYTES * 7 // 8

HEAD_DK = 128
GD_DV = 256
GD_CONV_W = 5
GRID_W = 64
EC_CAPACITY_FACTOR = 2
CHUNK = 64
HALF = CHUNK // 2
GLA_ATT_ROWS = 2 * CHUNK
GD_SOLVE_ROWS = 2 * CHUNK
GD_COL_CHUNK = 512
ADALN_COL_TILES = 12
EXP_CLAMP = 38.0
MAX_BLOCK_ROWS = 256
FFN_TILE = 256
FFN_MAX_ROWS = 2304
AFF_BITS = 31


def _cp(*sem):
    return pltpu.CompilerParams(dimension_semantics=sem, vmem_limit_bytes=VMEM_LIMIT)


def _dot(a, b):
    return jnp.dot(a, b, preferred_element_type=F32)


def _dot_nt(a, b):
    return lax.dot_general(a, b, (((1,), (1,)), ((), ())), preferred_element_type=F32)


def _dot_tn(a, b):
    return lax.dot_general(a, b, (((0,), (0,)), ((), ())), preferred_element_type=F32)


def _dot_hi(a, b):
    return jnp.dot(a, b, preferred_element_type=F32, precision=HIGHEST)


def _dot_nt_hi(a, b):
    return lax.dot_general(a, b, (((1,), (1,)), ((), ())), preferred_element_type=F32,
                           precision=HIGHEST)


def _silu(x):
    return x * jax.nn.sigmoid(x)


def _norm_mod(x, g, shift, scale):
    y = x * lax.rsqrt(jnp.mean(x * x, axis=-1, keepdims=True) + EPS)
    return y * (g * (1.0 + scale)) + shift


def _iota(shape, dim):
    return lax.broadcasted_iota(jnp.int32, shape, dim)


def _split3(x):
    hi = x.astype(BF16)
    r1 = x - hi.astype(F32)
    mid = r1.astype(BF16)
    lo = (r1 - mid.astype(F32)).astype(BF16)
    return [hi, mid, lo]


def _adaln_kernel(c_ref, w_ref, b_ref, o_ref):
    o_ref[0] = _dot_hi(_silu(c_ref[...]), w_ref[0]) + b_ref[0]


def _adaln(cond, w_mod, b_mod):
    depth, d, n = w_mod.shape
    r = cond.shape[0]
    tn = n // ADALN_COL_TILES
    return pl.pallas_call(
        _adaln_kernel,
        grid=(depth, n // tn),
        in_specs=[pl.BlockSpec((r, d), lambda i, j: (0, 0)),
                  pl.BlockSpec((1, d, tn), lambda i, j: (i, 0, j)),
                  pl.BlockSpec((1, 1, tn), lambda i, j: (i, 0, j))],
        out_specs=pl.BlockSpec((1, r, tn), lambda i, j: (i, 0, j)),
        out_shape=jax.ShapeDtypeStruct((depth, r, n), F32),
        compiler_params=_cp("parallel", "parallel"),
        name="adaln",
    )(cond, w_mod, b_mod.reshape(depth, 1, n))


def _hg_inproj_kernel(c_ref, x_ref, mod_ref, g_ref, lb_ref, wq_ref, wf_ref, wi_ref, wg_ref,
                      q_ref, lf_ref, v_ref, gate_ref, *, layer, n_heads, nc):
    m = mod_ref[0, 0]
    xb = jnp.where(pl.program_id(1) < nc, c_ref[0], x_ref[0])
    hb = _norm_mod(xb, g_ref[...], m[0:1], m[1:2]).astype(BF16)
    fdim = wq_ref.shape[1]
    dk = fdim // n_heads
    dv = wi_ref.shape[1] // n_heads

    q = _silu(_dot(hb, wq_ref[...]))
    for h in range(n_heads):
        q_ref[0, h] = q[:, h * dk:(h + 1) * dk].astype(BF16)

    lbp = lb_ref[...]
    e = jnp.exp(lbp - jnp.max(lbp, axis=0))
    lb = jnp.sum(e[:layer + 1], axis=0) / jnp.sum(e, axis=0)

    zf = _dot(hb, wf_ref[...])
    for d in range(2):
        lbd = lb[d:d + 1]
        f = lbd + (1.0 - lbd) * jax.nn.sigmoid(zf[:, d * fdim:(d + 1) * fdim])
        lf = jnp.log(f)
        for h in range(n_heads):
            lf_ref[d, 0, h] = lf[:, h * dk:(h + 1) * dk]

    v = _dot(hb, wi_ref[...])
    for h in range(n_heads):
        v_ref[0, h] = v[:, h * dv:(h + 1) * dv].astype(BF16)

    gate_ref[0] = _silu(_dot(hb, wg_ref[...])).astype(BF16)


def _hg_inproj(ctx, x, modsel, g, hg_lb, w_in, *, layer, nc, tm):
    b, n_lat, d = x.shape
    t = ctx.shape[1] + n_lat
    fdim = hg_lb.shape[-1]
    n_heads = fdim // HEAD_DK
    dv = d // n_heads
    wq = w_in[:, :fdim].astype(BF16)
    wf = w_in[:, fdim:3 * fdim].astype(BF16)
    wi = w_in[:, 3 * fdim:3 * fdim + d].astype(BF16)
    wg = w_in[:, 3 * fdim + d:].astype(BF16)
    full = lambda a: pl.BlockSpec(a.shape, lambda i, j: (0,) * a.ndim)
    g2 = g.reshape(1, d)
    return pl.pallas_call(
        functools.partial(_hg_inproj_kernel, layer=layer, n_heads=n_heads, nc=nc),
        grid=(b, t // tm),
        in_specs=[pl.BlockSpec((1, tm, d), lambda i, j: (i, jnp.minimum(j, nc - 1), 0)),
                  pl.BlockSpec((1, tm, d), lambda i, j: (i, jnp.maximum(j - nc, 0), 0)),
                  pl.BlockSpec((1, 1, 6, d), lambda i, j: (i, jnp.where(j >= nc, 1, 0), 0, 0)),
                  full(g2), full(hg_lb), full(wq), full(wf), full(wi), full(wg)],
        out_specs=[pl.BlockSpec((1, n_heads, tm, HEAD_DK), lambda i, j: (i, 0, j, 0)),
                   pl.BlockSpec((2, 1, n_heads, tm, HEAD_DK), lambda i, j: (0, i, 0, j, 0)),
                   pl.BlockSpec((1, n_heads, tm, dv), lambda i, j: (i, 0, j, 0)),
                   pl.BlockSpec((1, tm, d), lambda i, j: (i, j, 0))],
        out_shape=[jax.ShapeDtypeStruct((b, n_heads, t, HEAD_DK), BF16),
                   jax.ShapeDtypeStruct((2, b, n_heads, t, HEAD_DK), F32),
                   jax.ShapeDtypeStruct((b, n_heads, t, dv), BF16),
                   jax.ShapeDtypeStruct((b, t, d), BF16)],
        compiler_params=_cp("parallel", "parallel"),
        name="hg_inproj",
    )(ctx, x, modsel, g2, hg_lb, wq, wf, wi, wg)


def _scan_block_index(step, nc, nl, rev):
    if not rev:
        return step
    return jnp.where(step < nc, nc - 1 - step, nc + nl - 1 - (step - nc))


def _chunk_masks(rev):
    r = _iota((CHUNK, CHUNK), 0)
    c = _iota((CHUNK, CHUNK), 1)
    incl = (c >= r) if rev else (c <= r)
    strict = (c > r) if rev else (c < r)
    return incl, strict


def _gla_scan_kernel(*refs, n_heads, nck, tm):
    dirs = [(False,) + refs[0:3] + refs[6:7] + refs[8:12],
            (True,) + refs[3:6] + refs[7:8] + refs[12:16]]

    @pl.when(pl.program_id(1) == 0)
    def _():
        for d in dirs:
            d[5][...] = jnp.zeros(d[5].shape, F32)

    dk = refs[0].shape[-1]
    r = _iota((tm, tm), 0)
    c = _iota((tm, tm), 1)
    same_chunk = (r // CHUNK) == (c // CHUNK)
    same_half = (r // HALF) == (c // HALF)

    def per_rows(bh, row_of, span):
        return jnp.concatenate([jnp.broadcast_to(bh[row_of(x0):row_of(x0) + 1], (span, dk))
                                for x0 in range(0, tm, span)], axis=0)

    heads = range(n_heads)
    jobs = [(d, h) for h in heads for d in dirs]
    ab = min(tm, GLA_ATT_ROWS)

    bhs = []
    for (rev, q_ref, lf_ref, *_), h in jobs:
        causal = (c >= r) if rev else (c <= r)
        lmb = jnp.where(same_chunk & causal, 1.0, 0.0).astype(BF16)
        g3 = _dot(lmb, jnp.concatenate(_split3(lf_ref[0, 0, h]), axis=1))
        bhs.append(g3[:, :dk] + g3[:, dk:2 * dk] + g3[:, 2 * dk:])

    decs, atts = [], []
    for ((rev, q_ref, lf_ref, v_ref, o_ref, st_ref, oin_ref, qc_ref, kh_ref), h), bh in zip(jobs,
                                                                                            bhs):
        q = q_ref[0, h].astype(F32)
        k = 1.0 - jnp.exp(lf_ref[0, 0, h])
        bls = [bh[ci * CHUNK:ci * CHUNK + 1] if rev
               else bh[(ci + 1) * CHUNK - 1:(ci + 1) * CHUNK] for ci in range(nck)]
        blc = jnp.concatenate([jnp.broadcast_to(x, (CHUNK, dk)) for x in bls], axis=0)
        qc_ref[h] = (q * jnp.exp(bh)).astype(BF16)
        kh_ref[h] = (k * jnp.exp(blc - bh)).astype(BF16)
        decs.append([jnp.exp(x) for x in bls])
        bm = per_rows(bh, lambda x0: x0 + HALF // 2, HALF)
        qd = (q * jnp.exp(jnp.minimum(bh - bm, EXP_CLAMP))).astype(BF16)
        kd = (k * jnp.exp(jnp.minimum(bm - bh, EXP_CLAMP))).astype(BF16)
        be = per_rows(bh, lambda x0: x0 + (HALF if rev else HALF - 1), CHUNK)
        qx = (q * jnp.exp(jnp.minimum(bh - be, EXP_CLAMP))).astype(BF16)
        kx = (k * jnp.exp(jnp.minimum(be - bh, EXP_CLAMP))).astype(BF16)
        atts.append([(_dot_nt(qd[p0:p0 + ab], kd[p0:p0 + ab]), _dot_nt(qx[p0:p0 + ab], kx[p0:p0 + ab]))
                     for p0 in range(0, tm, ab)])

    for ((rev, _, _, v_ref, _, _, oin_ref, _, _), h), att_blocks in zip(jobs, atts):
        causal = (c >= r) if rev else (c <= r)
        second = 0 if rev else 1
        mask_d = (same_half & causal)[:ab, :ab]
        mask_x = (same_chunk & ((r // HALF) % 2 == second) & ((c // HALF) % 2 == 1 - second))[:ab, :ab]
        for i, (att_d, att_x) in enumerate(att_blocks):
            att = jnp.where(mask_d, att_d, 0.0) + jnp.where(mask_x, att_x, 0.0)
            oin_ref[h, i * ab:(i + 1) * ab, :] = _dot(att.astype(BF16),
                                                     v_ref[0, h, i * ab:(i + 1) * ab, :])

    for step in range(nck):
        for ((rev, _, _, v_ref, o_ref, st_ref, oin_ref, qc_ref, kh_ref), h), dec in zip(jobs, decs):
            ci = nck - 1 - step if rev else step
            rows = pl.ds(ci * CHUNK, CHUNK)
            st = st_ref[h]
            o = oin_ref[h, rows, :] + _dot_nt(qc_ref[h, rows, :], st.astype(BF16))
            o_ref[0, h, rows, :] = o.astype(o_ref.dtype)
            st_ref[h] = st * dec[ci] + _dot_tn(v_ref[0, h, rows, :], kh_ref[h, rows, :])


def _gla_scan(q, lf, v, *, nc, tm, out_dtype):
    b, n_heads, t, dk = q.shape
    dv = v.shape[-1]
    nblk = t // tm
    nl = nblk - nc
    nck = tm // CHUNK

    def specs(rev):
        d = 1 if rev else 0
        blk = lambda s: _scan_block_index(s, nc, nl, rev)
        return [pl.BlockSpec((1, n_heads, tm, dk), lambda i, s: (i, 0, blk(s), 0)),
                pl.BlockSpec((1, 1, n_heads, tm, dk), lambda i, s: (d, i, 0, blk(s), 0)),
                pl.BlockSpec((1, n_heads, tm, dv), lambda i, s: (i, 0, blk(s), 0))]

    scratch = [pltpu.VMEM((n_heads, dv, dk), F32),
               pltpu.VMEM((n_heads, tm, dv), F32),
               pltpu.VMEM((n_heads, tm, dk), BF16),
               pltpu.VMEM((n_heads, tm, dk), BF16)]
    o_shape = jax.ShapeDtypeStruct((b, n_heads, t, dv), out_dtype)
    return pl.pallas_call(
        functools.partial(_gla_scan_kernel, n_heads=n_heads, nck=nck, tm=tm),
        grid=(b, nblk),
        in_specs=specs(False) + specs(True),
        out_specs=[specs(False)[2], specs(True)[2]],
        out_shape=[o_shape, o_shape],
        scratch_shapes=scratch + scratch,
        compiler_params=_cp("parallel", "arbitrary"),
        name="gla_scan",
    )(q, lf, v, q, lf, v)


def _outproj_kernel(of_ref, ob_ref, gate_ref, gain_ref, w_ref, c_ref, x_ref, mod_ref, o_ref, y_ref,
                    *, n_heads, nc, rpc):
    dv = of_ref.shape[-1]
    for h in range(n_heads):
        o = of_ref[0, h].astype(F32) + ob_ref[0, h].astype(F32)
        cs = slice(h * dv, (h + 1) * dv)
        o = o * lax.rsqrt(jnp.mean(o * o, axis=-1, keepdims=True) + EPS) * gain_ref[:, cs]
        y_ref[:, cs] = (o * gate_ref[0, :, cs].astype(F32)).astype(BF16)
    y = mod_ref[0, 0][2:3] * _dot(y_ref[...], w_ref[...])
    if not rpc:
        o_ref[0] = jnp.where(pl.program_id(1) < nc, c_ref[0], x_ref[0]) + y
    else:
        for j in range(y.shape[0] // rpc):
            o_ref[0, :, j, :] = x_ref[0, :, j, :] + y[j * rpc:(j + 1) * rpc]


def _outproj(o_f, o_b, gate, gain, w_out, x_ctx, x_lat, modsel, *, nc, tm, rpc=0):
    b, n_heads, t, dv = o_f.shape
    d = w_out.shape[-1]
    hv = n_heads * dv
    off = nc if rpc else 0
    nblk = t // tm - off
    seg = (lambda j: 1) if rpc else (lambda j: jnp.where(j >= nc, 1, 0))
    w = w_out.astype(BF16)
    gain2 = gain.reshape(1, hv)
    if rpc:
        x_spec = o_spec = pl.BlockSpec((1, rpc, tm // rpc, d), lambda i, j: (i, 0, j, 0))
        o_shape = x_lat.shape
    else:
        x_spec = pl.BlockSpec((1, tm, d), lambda i, j: (i, jnp.maximum(j - nc, 0), 0))
        o_spec = pl.BlockSpec((1, tm, d), lambda i, j: (i, j, 0))
        o_shape = (b, t, d)
    return pl.pallas_call(
        functools.partial(_outproj_kernel, n_heads=n_heads, nc=nc, rpc=rpc),
        grid=(b, nblk),
        in_specs=[pl.BlockSpec((1, n_heads, tm, dv), lambda i, j: (i, 0, j + off, 0)),
                  pl.BlockSpec((1, n_heads, tm, dv), lambda i, j: (i, 0, j + off, 0)),
                  pl.BlockSpec((1, tm, hv), lambda i, j: (i, j + off, 0)),
                  pl.BlockSpec((1, hv), lambda i, j: (0, 0)),
                  pl.BlockSpec((hv, d), lambda i, j: (0, 0)),
                  pl.BlockSpec((1, tm, d), lambda i, j: (i, jnp.minimum(j, nc - 1), 0)),
                  x_spec,
                  pl.BlockSpec((1, 1, 6, d), lambda i, j: (i, seg(j), 0, 0))],
        out_specs=o_spec,
        out_shape=jax.ShapeDtypeStruct(o_shape, F32),
        scratch_shapes=[pltpu.VMEM((tm, hv), BF16)],
        compiler_params=_cp("parallel", "parallel"),
        name="outproj",
    )(o_f, o_b, gate, gain2, w, x_ctx, x_lat, modsel)


def _gd_inproj_kernel(cp_ref, cc_ref, cn_ref, lp_ref, lc_ref, ln_ref, mod_ref, g_ref, wqkv_ref,
                      wg_ref, wab_ref, cw_ref, alog_ref, dtb_ref, q_ref, k_ref, v_ref, gate_ref,
                      ab_ref, xs_ref, ys_ref, *, nc, nblk, n_heads, tm, cw, rpc):
    t = pl.program_id(1)
    first = (t == 0) | (t == nc)
    last = (t == nc - 1) | (t == nblk - 1)
    m = mod_ref[0, 0]
    halo = SUBLANES
    nlt = xs_ref.shape[0]
    d = nlt * LANES
    tile = lambda c: slice(c * LANES, (c + 1) * LANES)

    @pl.when(t < nc)
    def _():
        for c in range(nlt):
            xs_ref[c, 0:halo] = cp_ref[0, :, tile(c)]
            xs_ref[c, halo:halo + tm] = cc_ref[0, :, tile(c)]
            xs_ref[c, halo + tm:] = cn_ref[0, :, tile(c)]

    @pl.when(t >= nc)
    def _():
        cpb = tm // rpc
        for c in range(nlt):
            xs_ref[c, 0:halo] = lp_ref[0, :, SUBLANES - 1, tile(c)]
            for j in range(cpb):
                xs_ref[c, halo + j * rpc:halo + (j + 1) * rpc] = lc_ref[0, :, j, tile(c)]
            xs_ref[c, halo + tm:] = ln_ref[0, :, 0, tile(c)]

    xc = jnp.concatenate([xs_ref[c, halo:halo + tm] for c in range(nlt)], axis=1)
    hc = _norm_mod(xc, g_ref[...], m[0:1], m[1:2]).astype(BF16)

    rows = tm + 2 * halo
    ni = rows // SUBLANES
    xp = jnp.concatenate(
        [jnp.concatenate([xs_ref[c, pl.ds(i, SUBLANES, stride=ni), :] for i in range(ni)], axis=0)
         for c in range(nlt)], axis=1)
    he = _norm_mod(xp, g_ref[...], m[0:1], m[1:2]).astype(BF16)
    row = _iota((rows, 1), 0)
    tok = ni * (row % SUBLANES) + row // SUBLANES
    valid = ((tok >= halo) | jnp.logical_not(first)) & ((tok < tm + halo) | jnp.logical_not(last))

    gate_ref[0] = _silu(_dot(hc, wg_ref[...])).astype(BF16)

    zab = _dot(hc, wab_ref[...])
    lane = _iota(zab.shape, 1)
    loga = -jnp.exp(alog_ref[...]) * jax.nn.softplus(zab + dtb_ref[...])
    ab_ref[0] = jnp.where(lane < 2 * n_heads, loga, jax.nn.sigmoid(zab))

    qk = n_heads * HEAD_DK
    nchan = wqkv_ref.shape[1]
    pad = GD_CONV_W // 2
    for cc in range(nchan // cw):
        c0 = cc * cw
        z = jnp.where(valid, _dot(he, wqkv_ref[:, c0:c0 + cw]), 0.0)
        z3 = z.reshape(ni, SUBLANES, cw)
        zlo = pltpu.roll(z3[ni - pad:], 1, 1)
        zhi = pltpu.roll(z3[:pad], SUBLANES - 1, 1)
        taps = [cw_ref[j:j + 1, c0:c0 + cw] for j in range(GD_CONV_W)]

        def group(i):
            return zlo[i + pad] if i < 0 else (zhi[i - ni] if i >= ni else z3[i])

        def finish(acc, i0):
            u = _silu(acc)
            if c0 < 2 * qk:
                scale = HEAD_DK ** -0.5 if c0 < qk else 1.0
                parts = []
                for j in range(cw // HEAD_DK):
                    tt = u[:, :, j * HEAD_DK:(j + 1) * HEAD_DK]
                    parts.append(tt * (lax.rsqrt(jnp.sum(tt * tt, axis=-1, keepdims=True) + EPS)
                                       * scale))
                u = jnp.concatenate(parts, axis=-1)
            for c in range(cw // LANES):
                for i in range(u.shape[0]):
                    ys_ref[c, pl.ds(i0 + i, SUBLANES, stride=ni), :] = u[i][:, tile(c)]

        inner = ni - 2 * pad
        acc = taps[0] * z3[0:inner]
        for j in range(1, GD_CONV_W):
            acc = acc + taps[j] * z3[j:j + inner]
        finish(acc, pad)
        for i0 in (0, ni - pad):
            edge = jnp.stack([sum(taps[j][0] * group(i + j - pad) for j in range(GD_CONV_W))
                              for i in range(i0, i0 + pad)], axis=0)
            finish(edge, i0)

        def centre(lane0, width):
            return jnp.concatenate([ys_ref[c, halo:halo + tm] for c in
                                    range(lane0 // LANES, (lane0 + width) // LANES)], axis=1)

        if c0 < 2 * qk:
            dst, base = (q_ref, c0) if c0 < qk else (k_ref, c0 - qk)
            for j in range(cw // HEAD_DK):
                dst[0, base // HEAD_DK + j] = centre(j * HEAD_DK, HEAD_DK).astype(BF16)
        else:
            base = c0 - 2 * qk
            for j in range(cw // GD_DV):
                v_ref[0, base // GD_DV + j] = centre(j * GD_DV, GD_DV).astype(BF16)


def _gd_inproj(x_ctx, x_lat, modsel, g, w_in, conv_w, a_log, dt_bias, *, tm):
    b, n_ctx, d = x_ctx.shape
    n_lat = x_lat.shape[1]
    rpc = n_lat // GRID_W
    assert rpc % SUBLANES == 0 and tm % rpc == 0 and n_ctx % tm == 0 and n_ctx > 0
    t = n_ctx + n_lat
    nc = n_ctx // tm
    cpb = tm // rpc
    assert cpb % SUBLANES == 0, "halo blocks take one 8-column group of the grid"
    lat_v = x_lat.reshape(b, rpc, GRID_W, d)
    n_heads = a_log.shape[-1]
    qk = n_heads * HEAD_DK
    vd = n_heads * GD_DV
    nchan = 2 * qk + vd
    nblk = t // tm
    cw = min(GD_COL_CHUNK, qk)
    wqkv = w_in[:, :nchan].astype(BF16)
    wg = w_in[:, nchan:nchan + vd].astype(BF16)
    wab = jnp.pad(w_in[:, nchan + vd:], ((0, 0), (0, LANES - 4 * n_heads))).astype(BF16)
    alog = jnp.pad(a_log.reshape(1, 2 * n_heads), ((0, 0), (0, LANES - 2 * n_heads)))
    dtb = jnp.pad(dt_bias.reshape(1, 2 * n_heads), ((0, 0), (0, LANES - 2 * n_heads)))
    g2 = g.reshape(1, d)
    full = lambda a: pl.BlockSpec(a.shape, lambda i, j: (0,) * a.ndim)
    spb = tm // SUBLANES
    last_slab = n_ctx // SUBLANES - 1
    gpb = cpb // SUBLANES
    last_cg = GRID_W // SUBLANES - 1
    return pl.pallas_call(
        functools.partial(_gd_inproj_kernel, nc=nc, nblk=nblk, n_heads=n_heads, tm=tm, cw=cw,
                          rpc=rpc),
        grid=(b, nblk),
        in_specs=[pl.BlockSpec((1, SUBLANES, d),
                               lambda i, j: (i, jnp.clip(j * spb - 1, 0, last_slab), 0)),
                  pl.BlockSpec((1, tm, d), lambda i, j: (i, jnp.minimum(j, nc - 1), 0)),
                  pl.BlockSpec((1, SUBLANES, d),
                               lambda i, j: (i, jnp.clip((j + 1) * spb, 0, last_slab), 0)),
                  pl.BlockSpec((1, SUBLANES, SUBLANES, d),
                               lambda i, j: (i, rpc // SUBLANES - 1,
                                             jnp.clip((j - nc) * gpb - 1, 0, last_cg), 0)),
                  pl.BlockSpec((1, rpc, cpb, d), lambda i, j: (i, 0, jnp.maximum(j - nc, 0), 0)),
                  pl.BlockSpec((1, SUBLANES, SUBLANES, d),
                               lambda i, j: (i, 0, jnp.clip((j - nc + 1) * gpb, 0, last_cg), 0)),
                  pl.BlockSpec((1, 1, 6, d), lambda i, j: (i, jnp.where(j >= nc, 1, 0), 0, 0)),
                  full(g2), full(wqkv), full(wg), full(wab), full(conv_w), full(alog), full(dtb)],
        out_specs=[pl.BlockSpec((1, n_heads, tm, HEAD_DK), lambda i, j: (i, 0, j, 0)),
                   pl.BlockSpec((1, n_heads, tm, HEAD_DK), lambda i, j: (i, 0, j, 0)),
                   pl.BlockSpec((1, n_heads, tm, GD_DV), lambda i, j: (i, 0, j, 0)),
                   pl.BlockSpec((1, tm, vd), lambda i, j: (i, j, 0)),
                   pl.BlockSpec((1, tm, LANES), lambda i, j: (i, j, 0))],
        out_shape=[jax.ShapeDtypeStruct((b, n_heads, t, HEAD_DK), BF16),
                   jax.ShapeDtypeStruct((b, n_heads, t, HEAD_DK), BF16),
                   jax.ShapeDtypeStruct((b, n_heads, t, GD_DV), BF16),
                   jax.ShapeDtypeStruct((b, t, vd), BF16),
                   jax.ShapeDtypeStruct((b, t, LANES), F32)],
        scratch_shapes=[pltpu.VMEM((d // LANES, tm + 2 * SUBLANES, LANES), F32),
                        pltpu.VMEM((cw // LANES, tm + 2 * SUBLANES, LANES), F32)],
        compiler_params=_cp("parallel", "parallel"),
        name="gd_inproj",
    )(x_ctx, x_ctx, x_ctx, lat_v, lat_v, lat_v, modsel, g2, wqkv, wg, wab, conv_w, alog, dtb)


def _neumann_inverses(mats, eye):
    n = eye.shape[0]
    ts = [eye - a for a in mats]
    ps = []
    for a in mats:
        ab = a.astype(BF16)
        ps.append(_dot(ab, ab))
    yield
    lvl = 2
    while lvl < CHUNK:
        for i in range(len(mats)):
            pb = ps[i].astype(BF16)
            tb = ts[i].astype(BF16)
            if 2 * lvl >= CHUNK:
                ts[i] = ts[i] + _dot(pb, tb)
            else:
                out = _dot(pb, jnp.concatenate([pb, tb], axis=1))
                ps[i] = out[:, :n]
                ts[i] = ts[i] + out[:, n:]
        yield
        lvl *= 2
    return ts


def _gdn_scan_block(q_ref, k_ref, v_ref, ab_ref, o_ref, s_ref, u_ref, l1_ref, l2_ref,
                    *, rev, n_heads, nck, tm):
    d = 1 if rev else 0

    def masks(n):
        r = _iota((n, n), 0)
        c = _iota((n, n), 1)
        same = (r // CHUNK) == (c // CHUNK)
        return (same & ((c >= r) if rev else (c <= r)), same & ((c > r) if rev else (c < r)),
                jnp.where(r == c, 1.0, 0.0))

    pb = min(tm, GD_SOLVE_ROWS)
    incl, strict, eye = masks(pb)
    dv = v_ref.shape[-1]

    ab = ab_ref[0]
    g3 = _dot(jnp.where(masks(tm)[0], 1.0, 0.0).astype(BF16),
              jnp.concatenate(_split3(ab), axis=1))
    gcol = g3[:, :LANES] + g3[:, LANES:2 * LANES] + g3[:, 2 * LANES:]
    eye_l = jnp.where(_iota((LANES, LANES), 0) == _iota((LANES, LANES), 1), 1.0, 0.0).astype(BF16)
    gr3 = _dot_nt(eye_l, jnp.concatenate(_split3(gcol), axis=0))
    grow = gr3[:, :tm] + gr3[:, tm:2 * tm] + gr3[:, 2 * tm:]

    incl_c, _ = _chunk_masks(rev)
    wide = lambda x, n: jnp.concatenate([x] * (n // LANES), axis=1) if n > LANES else x[:, :n]
    heads = range(n_heads)
    cas = [d * n_heads + h for h in heads]
    gcbs = [jnp.broadcast_to(gcol[:, ca:ca + 1], (tm, LANES)) for ca in cas]
    bcbs = [jnp.broadcast_to(ab[:, 2 * n_heads + ca:2 * n_heads + ca + 1], (tm, LANES))
            for ca in cas]

    mats = []
    for h in heads:
        k = k_ref[0, h]
        for p0 in range(0, tm, pb):
            gam = jnp.where(incl, jnp.exp(jnp.minimum(
                wide(gcbs[h][p0:p0 + pb], pb) - grow[cas[h]:cas[h] + 1, p0:p0 + pb], 0.0)), 0.0)
            kp = k[p0:p0 + pb]
            mats.append(jnp.where(strict, wide(bcbs[h][p0:p0 + pb], pb) * _dot_nt(kp, kp) * gam,
                                  0.0))
    yield
    tinvs = yield from _neumann_inverses(mats, eye)

    uws = []
    for h in heads:
        kf = k_ref[0, h].astype(F32)
        rhs = jnp.concatenate(
            [(wide(bcbs[h], dv) * v_ref[0, h].astype(F32)).astype(BF16),
             (bcbs[h] * jnp.exp(gcbs[h]) * kf).astype(BF16)], axis=1)
        uws.append(jnp.concatenate(
            [_dot(tinvs[h * (tm // pb) + i].astype(BF16), rhs[i * pb:(i + 1) * pb])
             for i in range(tm // pb)], axis=0))
    yield

    egl = {}
    for h in heads:
        ca, gcb, uw = cas[h], gcbs[h], uws[h]
        q = q_ref[0, h]
        k = k_ref[0, h]
        kf = k.astype(F32)
        egb = jnp.exp(gcb)
        u_ref[h] = uw[:, :dv]
        wb = uw[:, dv:].astype(BF16)
        qg = (q.astype(F32) * egb).astype(BF16)
        gls = [gcb[ci * CHUNK:ci * CHUNK + 1] if rev
               else gcb[(ci + 1) * CHUNK - 1:(ci + 1) * CHUNK] for ci in range(nck)]
        glb = jnp.concatenate([jnp.broadcast_to(g, (CHUNK, LANES)) for g in gls], axis=0)
        kd = kf * jnp.exp(glb - gcb)
        egl[h] = [jnp.exp(g[:, 0:1]) for g in gls]
        for ci in range(nck):
            c0 = ci * CHUNK
            l1_ref[h, ci, :CHUNK] = wb[c0:c0 + CHUNK]
            l1_ref[h, ci, CHUNK:] = qg[c0:c0 + CHUNK]
            gam_c = jnp.where(incl_c, jnp.exp(jnp.minimum(
                gcb[c0:c0 + CHUNK, :CHUNK] - grow[ca:ca + 1, c0:c0 + CHUNK], 0.0)), 0.0)
            qk_c = _dot_nt(q[c0:c0 + CHUNK], k[c0:c0 + CHUNK])
            l2_ref[h, ci, :CHUNK] = (qk_c * gam_c).astype(BF16)
            l2_ref[h, ci, CHUNK:] = kd[c0:c0 + CHUNK].T.astype(BF16)

    for step in range(nck):
        yield
        ci = nck - 1 - step if rev else step
        rows = pl.ds(ci * CHUNK, CHUNK)
        r1s = [_dot(l1_ref[h, ci], s_ref[h].astype(BF16)) for h in heads]
        weffs = [(u_ref[h, rows, :] - r1s[h][:CHUNK]).astype(BF16) for h in heads]
        yield
        r2s = [_dot(l2_ref[h, ci], weffs[h]) for h in heads]
        for h in heads:
            o_ref[0, h, rows, :] = (r2s[h][:CHUNK] + r1s[h][CHUNK:]).astype(o_ref.dtype)
            s_ref[h] = s_ref[h] * egl[h][ci] + r2s[h][CHUNK:]


def _gdn_scan_kernel(*refs, n_heads, nck, tm):
    fwd = refs[0:4] + refs[8:9] + refs[10:14]
    bwd = refs[4:8] + refs[9:10] + refs[14:18]

    @pl.when(pl.program_id(1) == 0)
    def _():
        for s_ref in (fwd[5], bwd[5]):
            s_ref[...] = jnp.zeros(s_ref.shape, F32)

    blocks = [_gdn_scan_block(*fwd, rev=False, n_heads=n_heads, nck=nck, tm=tm),
              _gdn_scan_block(*bwd, rev=True, n_heads=n_heads, nck=nck, tm=tm)]
    for _ in itertools.zip_longest(*blocks):
        pass


def _gdn_scan(q, k, v, ab, *, nc, tm, out_dtype):
    b, n_heads, t, dk = q.shape
    dv = v.shape[-1]
    nblk = t // tm
    nl = nblk - nc
    nck = tm // CHUNK

    def specs(rev):
        blk = lambda s: _scan_block_index(s, nc, nl, rev)
        return [pl.BlockSpec((1, n_heads, tm, dk), lambda i, s: (i, 0, blk(s), 0)),
                pl.BlockSpec((1, n_heads, tm, dk), lambda i, s: (i, 0, blk(s), 0)),
                pl.BlockSpec((1, n_heads, tm, dv), lambda i, s: (i, 0, blk(s), 0)),
                pl.BlockSpec((1, tm, LANES), lambda i, s: (i, blk(s), 0))]

    scratch = [pltpu.VMEM((n_heads, dk, dv), F32),
               pltpu.VMEM((n_heads, tm, dv), F32),
               pltpu.VMEM((n_heads, nck, 2 * CHUNK, dk), BF16),
               pltpu.VMEM((n_heads, nck, CHUNK + dk, CHUNK), BF16)]
    o_shape = jax.ShapeDtypeStruct((b, n_heads, t, dv), out_dtype)
    return pl.pallas_call(
        functools.partial(_gdn_scan_kernel, n_heads=n_heads, nck=nck, tm=tm),
        grid=(b, nblk),
        in_specs=specs(False) + specs(True),
        out_specs=[specs(False)[2], specs(True)[2]],
        out_shape=[o_shape, o_shape],
        scratch_shapes=scratch + scratch,
        compiler_params=_cp("parallel", "arbitrary"),
        name="gdn_scan",
    )(q, k, v, ab, q, k, v, ab)


def _route_kernel(x_ref, mod_ref, g_ref, wrt_ref, h_ref, slot_ref, aff_ref, lg_ref,
                  *, n_ctx, cap_c, cap_l, tr):
    n_exp, t = lg_ref.shape
    for rt in range(t // tr):
        m = mod_ref[0, 0 if rt * tr < n_ctx else 1]
        rows = slice(rt * tr, (rt + 1) * tr)
        h = _norm_mod(x_ref[0, rows, :], g_ref[...], m[3:4], m[4:5])
        h_ref[0, rows, :] = h.astype(BF16)
        lg_ref[:, rows] = _dot_nt_hi(wrt_ref[...], h)
    lg = lg_ref[...]
    e = jnp.exp(lg - jnp.max(lg, axis=0, keepdims=True))
    aff = e / jnp.sum(e, axis=0, keepdims=True)
    aff_ref[0] = aff
    bits = lax.bitcast_convert_type(aff, jnp.int32)
    lane = _iota((n_exp, t), 1)

    if n_ctx:
        is_ctx = lane < n_ctx
        regions = [(is_ctx, cap_c), (jnp.logical_not(is_ctx), cap_l)]
    else:
        regions = [(None, cap_l)]

    def count(pred, mask):
        p = pred if mask is None else (pred & mask)
        return jnp.sum(jnp.where(p, 1.0, 0.0), axis=1, keepdims=True)

    def thr_body(i, thrs):
        bit = lax.shift_left(jnp.int32(1), AFF_BITS - 1 - i)
        out = []
        for (mask, cap), thr in zip(regions, thrs):
            cand = thr | bit
            out.append(jnp.where(count(bits >= cand, mask) >= cap, cand, thr))
        return tuple(out)

    zero = jnp.zeros((n_exp, 1), jnp.int32)
    thrs = lax.fori_loop(0, AFF_BITS, thr_body, tuple(zero for _ in regions))

    idx_bits = t.bit_length()
    sel = None
    for (mask, cap), thr in zip(regions, thrs):
        gt = bits > thr
        tie = bits == thr
        need = cap - count(gt, mask)

        def j_body(i, j, tie=tie, mask=mask, need=need):
            cand = j | lax.shift_left(jnp.int32(1), idx_bits - 1 - i)
            return jnp.where(count(tie & (lane < cand), mask) <= need, cand, j)

        jmax = lax.fori_loop(0, idx_bits, j_body, zero)
        s = gt | (tie & (lane < jmax))
        if mask is not None:
            s = s & mask
        sel = s if sel is None else (sel | s)

    lt = LANES if t % LANES == 0 else CHUNK
    ut = jnp.where(_iota((lt, lt), 0) < _iota((lt, lt), 1), 1.0, 0.0).astype(BF16)
    self = jnp.where(sel, 1.0, 0.0)
    run = jnp.zeros((n_exp, 1), F32)
    pres = []
    for i in range(t // lt):
        tile = self[:, i * lt:(i + 1) * lt]
        pres.append(_dot(tile.astype(BF16), ut) + run)
        run = run + jnp.sum(tile, axis=1, keepdims=True)
    pre = jnp.concatenate(pres, axis=1)
    if n_ctx:
        slot = jnp.where(is_ctx, pre + cap_l, pre - cap_c)
    else:
        slot = pre
    slot_ref[0] = jnp.where(sel, slot, -1.0)


def _route(x, modsel, g, w_router, *, n_ctx, cap_c, cap_l, tr):
    b, t, d = x.shape
    n_exp = w_router.shape[-1]
    wrt = w_router.T
    g2 = g.reshape(1, d)
    return pl.pallas_call(
        functools.partial(_route_kernel, n_ctx=n_ctx, cap_c=cap_c, cap_l=cap_l, tr=tr),
        grid=(b,),
        in_specs=[pl.BlockSpec((1, t, d), lambda i: (i, 0, 0)),
                  pl.BlockSpec((1, 2, 6, d), lambda i: (i, 0, 0, 0)),
                  pl.BlockSpec((1, d), lambda i: (0, 0)),
                  pl.BlockSpec((n_exp, d), lambda i: (0, 0))],
        out_specs=[pl.BlockSpec((1, t, d), lambda i: (i, 0, 0)),
                   pl.BlockSpec((1, n_exp, t), lambda i: (i, 0, 0)),
                   pl.BlockSpec((1, n_exp, t), lambda i: (i, 0, 0))],
        out_shape=[jax.ShapeDtypeStruct((b, t, d), BF16),
                   jax.ShapeDtypeStruct((b, n_exp, t), F32),
                   jax.ShapeDtypeStruct((b, n_exp, t), F32)],
        scratch_shapes=[pltpu.VMEM((n_exp, t), F32)],
        compiler_params=_cp("parallel"),
        name="moe_route",
    )(x, modsel, g2, wrt)


def _gather_kernel(slot_ref, h_ref, x_ref, *, n_ctx, cap_c, cap_l):
    slot = slot_ref[0, 0]
    t = slot.shape[1]
    sl = slot[:, n_ctx:]
    p = jnp.where(sl == _iota((cap_l, t - n_ctx), 0).astype(F32), 1.0, 0.0).astype(BF16)
    x_ref[0, 0, 0:cap_l, :] = _dot(p, h_ref[0, n_ctx:, :]).astype(BF16)
    if n_ctx:
        sc = slot[:, :n_ctx] - float(cap_l)
        p = jnp.where(sc == _iota((cap_c, n_ctx), 0).astype(F32), 1.0, 0.0).astype(BF16)
        x_ref[0, 0, cap_l:, :] = _dot(p, h_ref[0, :n_ctx, :]).astype(BF16)


def _gather(h, slot_row, *, n_ctx, cap_c, cap_l):
    b, t, d = h.shape
    n_exp = slot_row.shape[1]
    m = cap_l + (cap_c if n_ctx else 0)
    return pl.pallas_call(
        functools.partial(_gather_kernel, n_ctx=n_ctx, cap_c=cap_c, cap_l=cap_l),
        grid=(b, n_exp),
        in_specs=[pl.BlockSpec((1, 1, 1, t), lambda i, e: (i, e, 0, 0)),
                  pl.BlockSpec((1, t, d), lambda i, e: (i, 0, 0))],
        out_specs=pl.BlockSpec((1, 1, m, d), lambda i, e: (e, i, 0, 0)),
        out_shape=jax.ShapeDtypeStruct((n_exp, b, m, d), BF16),
        compiler_params=_cp("parallel", "parallel"),
        name="moe_gather",
    )(slot_row.reshape(b, n_exp, 1, t), h)


def _ffn_kernel(x_ref, wg_ref, wu_ref, wd_ref, y_ref, acc_ref, wgb_ref, wub_ref, wdb_ref, *, tr):
    ft = pl.program_id(1)

    @pl.when(ft == 0)
    def _():
        acc_ref[...] = jnp.zeros(acc_ref.shape, F32)

    wgb_ref[...] = wg_ref[...].astype(BF16)
    wub_ref[...] = wu_ref[...].astype(BF16)
    wdb_ref[...] = wd_ref[...].astype(BF16)

    def body(i, carry):
        rows = pl.ds(pl.multiple_of(i * tr, tr), tr)
        x = x_ref[0, rows, :]
        hid = (_silu(_dot(x, wgb_ref[...])) * _dot(x, wub_ref[...])).astype(BF16)
        acc_ref[rows, :] += _dot(hid, wdb_ref[...])
        return carry

    lax.fori_loop(0, x_ref.shape[1] // tr, body, 0)

    @pl.when(ft == pl.num_programs(1) - 1)
    def _():
        y_ref[0] = acc_ref[...].astype(y_ref.dtype)


def _ffn(xs, w_gate, w_up, w_down, *, layer):
    n_exp, bm, d = xs.shape
    f = w_gate.shape[-1]
    tf = FFN_TILE if f % FFN_TILE == 0 else f
    tr = bm // (-(-bm // FFN_MAX_ROWS))
    return pl.pallas_call(
        functools.partial(_ffn_kernel, tr=tr),
        grid=(n_exp, f // tf),
        in_specs=[pl.BlockSpec((1, bm, d), lambda e, j: (e, 0, 0)),
                  pl.BlockSpec((None, None, d, tf), lambda e, j: (layer, e, 0, j)),
                  pl.BlockSpec((None, None, d, tf), lambda e, j: (layer, e, 0, j)),
                  pl.BlockSpec((None, None, tf, d), lambda e, j: (layer, e, j, 0))],
        out_specs=pl.BlockSpec((1, bm, d), lambda e, j: (e, 0, 0)),
        out_shape=jax.ShapeDtypeStruct((n_exp, bm, d), BF16),
        scratch_shapes=[pltpu.VMEM((bm, d), F32),
                        pltpu.VMEM((d, tf), BF16),
                        pltpu.VMEM((d, tf), BF16),
                        pltpu.VMEM((tf, d), BF16)],
        compiler_params=_cp("parallel", "arbitrary"),
        name="moe_ffn",
    )(xs, w_gate, w_up, w_down)


def _combine_kernel(*refs, nc, cap_c, cap_l, final):
    x_ref, ys_ref, sc_ref, ac_ref, mod_ref = refs[:5]
    fg_ref = refs[5] if final else None
    outs = refs[5 + bool(final):-1]
    acc_ref = refs[-1]
    n_exp = ys_ref.shape[0]
    tm = x_ref.shape[1]

    def run(k0, kk, o_ref):
        sc = sc_ref[0]
        ac = ac_ref[0]
        lane = (_iota((tm, kk), 1) + k0).astype(F32)
        acc_ref[...] = jnp.zeros(acc_ref.shape, F32)
        for e in range(n_exp):
            pt = jnp.where(sc[:, e:e + 1] == lane, 1.0, 0.0).astype(BF16)
            acc_ref[...] += ac[:, e:e + 1] * _dot(pt, ys_ref[e, 0, k0:k0 + kk, :])
        x2 = x_ref[0] + mod_ref[0, 0][5:6] * acc_ref[...]
        if final:
            x2 = x2 * lax.rsqrt(jnp.mean(x2 * x2, axis=-1, keepdims=True) + EPS) * fg_ref[...]
        o_ref[0] = x2

    if nc:
        t = pl.program_id(1)

        @pl.when(t < nc)
        def _():
            run(cap_l, cap_c, outs[0])

        @pl.when(t >= nc)
        def _():
            run(0, cap_l, outs[1])
    else:
        run(0, cap_l, outs[0])


def _combine(x, ys, slot_col, aff_col, modsel, final_g, *, nc, cap_c, cap_l, tm):
    b, t, d = x.shape
    n_exp, _, m, _ = ys.shape
    seg = (lambda j: jnp.where(j >= nc, 1, 0)) if nc else (lambda j: 1)
    if nc:
        out_specs = [pl.BlockSpec((1, tm, d), lambda i, j: (i, jnp.minimum(j, nc - 1), 0)),
                     pl.BlockSpec((1, tm, d), lambda i, j: (i, jnp.maximum(j - nc, 0), 0))]
        out_shape = [jax.ShapeDtypeStruct((b, nc * tm, d), F32),
                     jax.ShapeDtypeStruct((b, t - nc * tm, d), F32)]
    else:
        out_specs = pl.BlockSpec((1, tm, d), lambda i, j: (i, j, 0))
        out_shape = jax.ShapeDtypeStruct((b, t, d), F32)
    in_specs = [pl.BlockSpec((1, tm, d), lambda i, j: (i, j, 0)),
                pl.BlockSpec((n_exp, 1, m, d), lambda i, j: (0, i, 0, 0)),
                pl.BlockSpec((1, tm, n_exp), lambda i, j: (i, j, 0)),
                pl.BlockSpec((1, tm, n_exp), lambda i, j: (i, j, 0)),
                pl.BlockSpec((1, 1, 6, d), lambda i, j: (i, seg(j), 0, 0))]
    args = [x, ys, slot_col, aff_col, modsel]
    if final_g is not None:
        in_specs.append(pl.BlockSpec((1, d), lambda i, j: (0, 0)))
        args.append(final_g.reshape(1, d))
    return pl.pallas_call(
        functools.partial(_combine_kernel, nc=nc, cap_c=cap_c, cap_l=cap_l,
                          final=final_g is not None),
        grid=(b, t // tm),
        in_specs=in_specs,
        out_specs=out_specs,
        out_shape=out_shape,
        scratch_shapes=[pltpu.VMEM((tm, d), F32)],
        compiler_params=_cp("parallel", "arbitrary"),
        name="moe_combine",
    )(*args)


def _moe(x, modsel, g, w_router, w_gate, w_up, w_down, final_g, *, layer, n_ctx, n_lat, tm):
    b, t, d = x.shape
    n_exp = w_router.shape[-1]
    cap_l = EC_CAPACITY_FACTOR * n_lat // n_exp
    cap_c = EC_CAPACITY_FACTOR * n_ctx // n_exp
    h, slot_row, aff_row = _route(x, modsel, g, w_router, n_ctx=n_ctx, cap_c=cap_c, cap_l=cap_l,
                                  tr=tm)
    xs = _gather(h, slot_row, n_ctx=n_ctx, cap_c=cap_c, cap_l=cap_l)
    m = xs.shape[2]
    ys = _ffn(xs.reshape(n_exp, b * m, d), w_gate, w_up, w_down, layer=layer)
    return _combine(x, ys.reshape(n_exp, b, m, d), jnp.swapaxes(slot_row, 1, 2),
                    jnp.swapaxes(aff_row, 1, 2), modsel, final_g,
                    nc=n_ctx // tm, cap_c=cap_c, cap_l=cap_l, tm=tm)


def kernel(x, c, ctx, c_ctx, w_mod, b_mod, norm_g, hg_w_in, hg_lb, hg_onorm, hg_w_out, gd_w_in,
           gd_conv, gd_a_log, gd_dt_bias, gd_onorm, gd_w_out, moe_router, moe_w_gate, moe_w_up,
           moe_w_down, final_g):
    bsz, n_lat, d = x.shape
    n_ctx = ctx.shape[1]
    depth = w_mod.shape[0]
    assert depth == 2, "layer 0 = HGRN2, layer 1 = gated DeltaNet"
    tm = math.gcd(math.gcd(n_ctx, n_lat), MAX_BLOCK_ROWS)
    assert tm % CHUNK == 0
    nc = n_ctx // tm
    o_dt = BF16

    rows = -(-(bsz + 1) // SUBLANES) * SUBLANES
    cond = jnp.zeros((rows, d), F32).at[:bsz].set(c).at[bsz].set(c_ctx)
    mod = _adaln(cond, w_mod, b_mod).reshape(depth, rows, 6, d)

    def modsel(i):
        ctx_mod = jnp.broadcast_to(mod[i, bsz][None], (bsz, 6, d))
        return jnp.stack([ctx_mod, mod[i, :bsz]], axis=1)

    ms = modsel(0)
    q, lf, v, gate = _hg_inproj(ctx, x, ms, norm_g[0, 0], hg_lb, hg_w_in[0], layer=0, nc=nc, tm=tm)
    o_f, o_b = _gla_scan(q, lf, v, nc=nc, tm=tm, out_dtype=o_dt)
    x_all = _outproj(o_f, o_b, gate, hg_onorm[0], hg_w_out[0], ctx, x, ms, nc=nc, tm=tm)
    x_ctx, x_lat = _moe(x_all, ms, norm_g[0, 1], moe_router[0], moe_w_gate, moe_w_up, moe_w_down,
                        None, layer=0, n_ctx=n_ctx, n_lat=n_lat, tm=tm)

    ms = modsel(1)
    rpc = n_lat // GRID_W
    q, k, v, gate, ab = _gd_inproj(x_ctx, x_lat, ms, norm_g[1, 0], gd_w_in[0], gd_conv[0],
                                   gd_a_log[0], gd_dt_bias[0], tm=tm)
    o_f, o_b = _gdn_scan(q, k, v, ab, nc=nc, tm=tm, out_dtype=o_dt)
    x_lat = _outproj(o_f, o_b, gate, gd_onorm[0], gd_w_out[0], x_ctx,
                     x_lat.reshape(bsz, rpc, GRID_W, d), ms, nc=nc, tm=tm, rpc=rpc)
    x_lat = x_lat.reshape(bsz, n_lat, d)
    return _moe(x_lat, ms, norm_g[1, 1], moe_router[1], moe_w_gate, moe_w_up, moe_w_down, final_g,
                layer=1, n_ctx=0, n_lat=n_lat, tm=tm)
```

```python
import functools
import itertools
import math

import jax
import jax.numpy as jnp
from jax import lax
from jax.experimental import pallas as pl
from jax.experimental.pallas import tpu as pltpu

F32 = jnp.float32
BF16 = jnp.bfloat16
HIGHEST = lax.Precision.HIGHEST

EPS = 1e-6
LANES = 128
SUBLANES = 8
V7X_VMEM_BYTES = 64 * 1024 * 1024
VMEM_LIMIT = V7X_VMEM_BYTES * 7 // 8

HEAD_DK = 128
GD_DV = 256
GD_CONV_W = 5
GRID_W = 64
EC_CAPACITY_FACTOR = 2
CHUNK = 64
HALF = CHUNK // 2
GLA_ATT_ROWS = 2 * CHUNK
GD_SOLVE_ROWS = 2 * CHUNK
GD_COL_CHUNK = 512
ADALN_COL_TILES = 12
EXP_CLAMP = 38.0
MAX_BLOCK_ROWS = 256
FFN_TILE = 256
FFN_MAX_ROWS = 2304
AFF_BITS = 31


def _cp(*sem):
    return pltpu.CompilerParams(dimension_semantics=sem, vmem_limit_bytes=VMEM_LIMIT)


def _dot(a, b):
    return jnp.dot(a, b, preferred_element_type=F32)


def _dot_nt(a, b):
    return lax.dot_general(a, b, (((1,), (1,)), ((), ())), preferred_element_type=F32)


def _dot_tn(a, b):
    return lax.dot_general(a, b, (((0,), (0,)), ((), ())), preferred_element_type=F32)


def _dot_hi(a, b):
    return jnp.dot(a, b, preferred_element_type=F32, precision=HIGHEST)


def _dot_nt_hi(a, b):
    return lax.dot_general(a, b, (((1,), (1,)), ((), ())), preferred_element_type=F32,
                           precision=HIGHEST)


def _silu(x):
    return x * jax.nn.sigmoid(x)


def _norm_mod(x, g, shift, scale):
    y = x * lax.rsqrt(jnp.mean(x * x, axis=-1, keepdims=True) + EPS)
    return y * (g * (1.0 + scale)) + shift


def _iota(shape, dim):
    return lax.broadcasted_iota(jnp.int32, shape, dim)


def _split3(x):
    hi = x.astype(BF16)
    r1 = x - hi.astype(F32)
    mid = r1.astype(BF16)
    lo = (r1 - mid.astype(F32)).astype(BF16)
    return [hi, mid, lo]


def _adaln_kernel(c_ref, w_ref, b_ref, o_ref):
    o_ref[0] = _dot_hi(_silu(c_ref[...]), w_ref[0]) + b_ref[0]


def _adaln(cond, w_mod, b_mod):
    depth, d, n = w_mod.shape
    r = cond.shape[0]
    tn = n // ADALN_COL_TILES
    return pl.pallas_call(
        _adaln_kernel,
        grid=(depth, n // tn),
        in_specs=[pl.BlockSpec((r, d), lambda i, j: (0, 0)),
                  pl.BlockSpec((1, d, tn), lambda i, j: (i, 0, j)),
                  pl.BlockSpec((1, 1, tn), lambda i, j: (i, 0, j))],
        out_specs=pl.BlockSpec((1, r, tn), lambda i, j: (i, 0, j)),
        out_shape=jax.ShapeDtypeStruct((depth, r, n), F32),
        compiler_params=_cp("parallel", "parallel"),
        name="adaln",
    )(cond, w_mod, b_mod.reshape(depth, 1, n))


def _hg_inproj_kernel(c_ref, x_ref, mod_ref, g_ref, lb_ref, wq_ref, wf_ref, wi_ref, wg_ref,
                      q_ref, lf_ref, v_ref, gate_ref, *, layer, n_heads, nc):
    m = mod_ref[0, 0]
    xb = jnp.where(pl.program_id(1) < nc, c_ref[0], x_ref[0])
    hb = _norm_mod(xb, g_ref[...], m[0:1], m[1:2]).astype(BF16)
    fdim = wq_ref.shape[1]
    dk = fdim // n_heads
    dv = wi_ref.shape[1] // n_heads

    q = _silu(_dot(hb, wq_ref[...]))
    for h in range(n_heads):
        q_ref[0, h] = q[:, h * dk:(h + 1) * dk].astype(BF16)

    lbp = lb_ref[...]
    e = jnp.exp(lbp - jnp.max(lbp, axis=0))
    lb = jnp.sum(e[:layer + 1], axis=0) / jnp.sum(e, axis=0)

    zf = _dot(hb, wf_ref[...])
    for d in range(2):
        lbd = lb[d:d + 1]
        f = lbd + (1.0 - lbd) * jax.nn.sigmoid(zf[:, d * fdim:(d + 1) * fdim])
        lf = jnp.log(f)
        for h in range(n_heads):
            lf_ref[d, 0, h] = lf[:, h * dk:(h + 1) * dk]

    v = _dot(hb, wi_ref[...])
    for h in range(n_heads):
        v_ref[0, h] = v[:, h * dv:(h + 1) * dv].astype(BF16)

    gate_ref[0] = _silu(_dot(hb, wg_ref[...])).astype(BF16)


def _hg_inproj(ctx, x, modsel, g, hg_lb, w_in, *, layer, nc, tm):
    b, n_lat, d = x.shape
    t = ctx.shape[1] + n_lat
    fdim = hg_lb.shape[-1]
    n_heads = fdim // HEAD_DK
    dv = d // n_heads
    wq = w_in[:, :fdim].astype(BF16)
    wf = w_in[:, fdim:3 * fdim].astype(BF16)
    wi = w_in[:, 3 * fdim:3 * fdim + d].astype(BF16)
    wg = w_in[:, 3 * fdim + d:].astype(BF16)
    full = lambda a: pl.BlockSpec(a.shape, lambda i, j: (0,) * a.ndim)
    g2 = g.reshape(1, d)
    return pl.pallas_call(
        functools.partial(_hg_inproj_kernel, layer=layer, n_heads=n_heads, nc=nc),
        grid=(b, t // tm),
        in_specs=[pl.BlockSpec((1, tm, d), lambda i, j: (i, jnp.minimum(j, nc - 1), 0)),
                  pl.BlockSpec((1, tm, d), lambda i, j: (i, jnp.maximum(j - nc, 0), 0)),
                  pl.BlockSpec((1, 1, 6, d), lambda i, j: (i, jnp.where(j >= nc, 1, 0), 0, 0)),
                  full(g2), full(hg_lb), full(wq), full(wf), full(wi), full(wg)],
        out_specs=[pl.BlockSpec((1, n_heads, tm, HEAD_DK), lambda i, j: (i, 0, j, 0)),
                   pl.BlockSpec((2, 1, n_heads, tm, HEAD_DK), lambda i, j: (0, i, 0, j, 0)),
                   pl.BlockSpec((1, n_heads, tm, dv), lambda i, j: (i, 0, j, 0)),
                   pl.BlockSpec((1, tm, d), lambda i, j: (i, j, 0))],
        out_shape=[jax.ShapeDtypeStruct((b, n_heads, t, HEAD_DK), BF16),
                   jax.ShapeDtypeStruct((2, b, n_heads, t, HEAD_DK), F32),
                   jax.ShapeDtypeStruct((b, n_heads, t, dv), BF16),
                   jax.ShapeDtypeStruct((b, t, d), BF16)],
        compiler_params=_cp("parallel", "parallel"),
        name="hg_inproj",
    )(ctx, x, modsel, g2, hg_lb, wq, wf, wi, wg)


def _scan_block_index(step, nc, nl, rev):
    if not rev:
        return step
    return jnp.where(step < nc, nc - 1 - step, nc + nl - 1 - (step - nc))


def _chunk_masks(rev):
    r = _iota((CHUNK, CHUNK), 0)
    c = _iota((CHUNK, CHUNK), 1)
    incl = (c >= r) if rev else (c <= r)
    strict = (c > r) if rev else (c < r)
    return incl, strict


def _gla_scan_kernel(*refs, n_heads, nck, tm):
    dirs = [(False,) + refs[0:3] + refs[6:7] + refs[8:12],
            (True,) + refs[3:6] + refs[7:8] + refs[12:16]]

    @pl.when(pl.program_id(1) == 0)
    def _():
        for d in dirs:
            d[5][...] = jnp.zeros(d[5].shape, F32)

    dk = refs[0].shape[-1]
    r = _iota((tm, tm), 0)
    c = _iota((tm, tm), 1)
    same_chunk = (r // CHUNK) == (c // CHUNK)
    same_half = (r // HALF) == (c // HALF)

    def per_rows(bh, row_of, span):
        return jnp.concatenate([jnp.broadcast_to(bh[row_of(x0):row_of(x0) + 1], (span, dk))
                                for x0 in range(0, tm, span)], axis=0)

    heads = range(n_heads)
    jobs = [(d, h) for h in heads for d in dirs]
    ab = min(tm, GLA_ATT_ROWS)

    bhs = []
    for (rev, q_ref, lf_ref, *_), h in jobs:
        causal = (c >= r) if rev else (c <= r)
        lmb = jnp.where(same_chunk & causal, 1.0, 0.0).astype(BF16)
        g3 = _dot(lmb, jnp.concatenate(_split3(lf_ref[0, 0, h]), axis=1))
        bhs.append(g3[:, :dk] + g3[:, dk:2 * dk] + g3[:, 2 * dk:])

    decs, atts = [], []
    for ((rev, q_ref, lf_ref, v_ref, o_ref, st_ref, oin_ref, qc_ref, kh_ref), h), bh in zip(jobs,
                                                                                            bhs):
        q = q_ref[0, h].astype(F32)
        k = 1.0 - jnp.exp(lf_ref[0, 0, h])
        bls = [bh[ci * CHUNK:ci * CHUNK + 1] if rev
               else bh[(ci + 1) * CHUNK - 1:(ci + 1) * CHUNK] for ci in range(nck)]
        blc = jnp.concatenate([jnp.broadcast_to(x, (CHUNK, dk)) for x in bls], axis=0)
        qc_ref[h] = (q * jnp.exp(bh)).astype(BF16)
        kh_ref[h] = (k * jnp.exp(blc - bh)).astype(BF16)
        decs.append([jnp.exp(x) for x in bls])
        bm = per_rows(bh, lambda x0: x0 + HALF // 2, HALF)
        qd = (q * jnp.exp(jnp.minimum(bh - bm, EXP_CLAMP))).astype(BF16)
        kd = (k * jnp.exp(jnp.minimum(bm - bh, EXP_CLAMP))).astype(BF16)
        be = per_rows(bh, lambda x0: x0 + (HALF if rev else HALF - 1), CHUNK)
        ex = jnp.exp(-jnp.abs(bh - be))
        qx = (q * ex).astype(BF16)
        kx = (k * ex).astype(BF16)
        atts.append([(_dot_nt(qd[p0:p0 + ab], kd[p0:p0 + ab]), _dot_nt(qx[p0:p0 + ab], kx[p0:p0 + ab]))
                     for p0 in range(0, tm, ab)])

    for ((rev, _, _, v_ref, _, _, oin_ref, _, _), h), att_blocks in zip(jobs, atts):
        causal = (c >= r) if rev else (c <= r)
        second = 0 if rev else 1
        mask_d = (same_half & causal)[:ab, :ab]
        mask_x = (same_chunk & ((r // HALF) % 2 == second) & ((c // HALF) % 2 == 1 - second))[:ab, :ab]
        for i, (att_d, att_x) in enumerate(att_blocks):
            att = jnp.where(mask_d, att_d, 0.0) + jnp.where(mask_x, att_x, 0.0)
            oin_ref[h, i * ab:(i + 1) * ab, :] = _dot(att.astype(BF16),
                                                     v_ref[0, h, i * ab:(i + 1) * ab, :])

    for step in range(nck):
        for ((rev, _, _, v_ref, o_ref, st_ref, oin_ref, qc_ref, kh_ref), h), dec in zip(jobs, decs):
            ci = nck - 1 - step if rev else step
            rows = pl.ds(ci * CHUNK, CHUNK)
            st = st_ref[h]
            o = oin_ref[h, rows, :] + _dot_nt(qc_ref[h, rows, :], st.astype(BF16))
            o_ref[0, h, rows, :] = o.astype(o_ref.dtype)
            st_ref[h] = st * dec[ci] + _dot_tn(v_ref[0, h, rows, :], kh_ref[h, rows, :])


def _gla_scan(q, lf, v, *, nc, tm, out_dtype):
    b, n_heads, t, dk = q.shape
    dv = v.shape[-1]
    nblk = t // tm
    nl = nblk - nc
    nck = tm // CHUNK

    def specs(rev):
        d = 1 if rev else 0
        blk = lambda s: _scan_block_index(s, nc, nl, rev)
        return [pl.BlockSpec((1, n_heads, tm, dk), lambda i, s: (i, 0, blk(s), 0)),
                pl.BlockSpec((1, 1, n_heads, tm, dk), lambda i, s: (d, i, 0, blk(s), 0)),
                pl.BlockSpec((1, n_heads, tm, dv), lambda i, s: (i, 0, blk(s), 0))]

    scratch = [pltpu.VMEM((n_heads, dv, dk), F32),
               pltpu.VMEM((n_heads, tm, dv), F32),
               pltpu.VMEM((n_heads, tm, dk), BF16),
               pltpu.VMEM((n_heads, tm, dk), BF16)]
    o_shape = jax.ShapeDtypeStruct((b, n_heads, t, dv), out_dtype)
    return pl.pallas_call(
        functools.partial(_gla_scan_kernel, n_heads=n_heads, nck=nck, tm=tm),
        grid=(b, nblk),
        in_specs=specs(False) + specs(True),
        out_specs=[specs(False)[2], specs(True)[2]],
        out_shape=[o_shape, o_shape],
        scratch_shapes=scratch + scratch,
        compiler_params=_cp("parallel", "arbitrary"),
        name="gla_scan",
    )(q, lf, v, q, lf, v)


def _outproj_kernel(of_ref, ob_ref, gate_ref, gain_ref, w_ref, c_ref, x_ref, mod_ref, o_ref, y_ref,
                    *, n_heads, nc, rpc):
    dv = of_ref.shape[-1]
    for h in range(n_heads):
        o = of_ref[0, h].astype(F32) + ob_ref[0, h].astype(F32)
        cs = slice(h * dv, (h + 1) * dv)
        o = o * lax.rsqrt(jnp.mean(o * o, axis=-1, keepdims=True) + EPS) * gain_ref[:, cs]
        y_ref[:, cs] = (o * gate_ref[0, :, cs].astype(F32)).astype(BF16)
    y = mod_ref[0, 0][2:3] * _dot(y_ref[...], w_ref[...])
    if not rpc:
        o_ref[0] = jnp.where(pl.program_id(1) < nc, c_ref[0], x_ref[0]) + y
    else:
        for j in range(y.shape[0] // rpc):
            o_ref[0, :, j, :] = x_ref[0, :, j, :] + y[j * rpc:(j + 1) * rpc]


def _outproj(o_f, o_b, gate, gain, w_out, x_ctx, x_lat, modsel, *, nc, tm, rpc=0):
    b, n_heads, t, dv = o_f.shape
    d = w_out.shape[-1]
    hv = n_heads * dv
    off = nc if rpc else 0
    nblk = t // tm - off
    seg = (lambda j: 1) if rpc else (lambda j: jnp.where(j >= nc, 1, 0))
    w = w_out.astype(BF16)
    gain2 = gain.reshape(1, hv)
    if rpc:
        x_spec = o_spec = pl.BlockSpec((1, rpc, tm // rpc, d), lambda i, j: (i, 0, j, 0))
        o_shape = x_lat.shape
    else:
        x_spec = pl.BlockSpec((1, tm, d), lambda i, j: (i, jnp.maximum(j - nc, 0), 0))
        o_spec = pl.BlockSpec((1, tm, d), lambda i, j: (i, j, 0))
        o_shape = (b, t, d)
    return pl.pallas_call(
        functools.partial(_outproj_kernel, n_heads=n_heads, nc=nc, rpc=rpc),
        grid=(b, nblk),
        in_specs=[pl.BlockSpec((1, n_heads, tm, dv), lambda i, j: (i, 0, j + off, 0)),
                  pl.BlockSpec((1, n_heads, tm, dv), lambda i, j: (i, 0, j + off, 0)),
                  pl.BlockSpec((1, tm, hv), lambda i, j: (i, j + off, 0)),
                  pl.BlockSpec((1, hv), lambda i, j: (0, 0)),
                  pl.BlockSpec((hv, d), lambda i, j: (0, 0)),
                  pl.BlockSpec((1, tm, d), lambda i, j: (i, jnp.minimum(j, nc - 1), 0)),
                  x_spec,
                  pl.BlockSpec((1, 1, 6, d), lambda i, j: (i, seg(j), 0, 0))],
        out_specs=o_spec,
        out_shape=jax.ShapeDtypeStruct(o_shape, F32),
        scratch_shapes=[pltpu.VMEM((tm, hv), BF16)],
        compiler_params=_cp("parallel", "parallel"),
        name="outproj",
    )(o_f, o_b, gate, gain2, w, x_ctx, x_lat, modsel)


def _gd_inproj_kernel(cp_ref, cc_ref, cn_ref, lp_ref, lc_ref, ln_ref, mod_ref, g_ref, wqkv_ref,
                      wg_ref, wab_ref, cw_ref, alog_ref, dtb_ref, q_ref, k_ref, v_ref, gate_ref,
                      ab_ref, xs_ref, ys_ref, *, nc, nblk, n_heads, tm, cw, rpc):
    t = pl.program_id(1)
    first = (t == 0) | (t == nc)
    last = (t == nc - 1) | (t == nblk - 1)
    m = mod_ref[0, 0]
    halo = SUBLANES
    nlt = xs_ref.shape[0]
    d = nlt * LANES
    tile = lambda c: slice(c * LANES, (c + 1) * LANES)

    @pl.when(t < nc)
    def _():
        for c in range(nlt):
            xs_ref[c, 0:halo] = cp_ref[0, :, tile(c)]
            xs_ref[c, halo:halo + tm] = cc_ref[0, :, tile(c)]
            xs_ref[c, halo + tm:] = cn_ref[0, :, tile(c)]

    @pl.when(t >= nc)
    def _():
        cpb = tm // rpc
        for c in range(nlt):
            xs_ref[c, 0:halo] = lp_ref[0, :, SUBLANES - 1, tile(c)]
            for j in range(cpb):
                xs_ref[c, halo + j * rpc:halo + (j + 1) * rpc] = lc_ref[0, :, j, tile(c)]
            xs_ref[c, halo + tm:] = ln_ref[0, :, 0, tile(c)]

    xc = jnp.concatenate([xs_ref[c, halo:halo + tm] for c in range(nlt)], axis=1)
    hc = _norm_mod(xc, g_ref[...], m[0:1], m[1:2]).astype(BF16)

    rows = tm + 2 * halo
    ni = rows // SUBLANES
    xp = jnp.concatenate(
        [jnp.concatenate([xs_ref[c, pl.ds(i, SUBLANES, stride=ni), :] for i in range(ni)], axis=0)
         for c in range(nlt)], axis=1)
    he = _norm_mod(xp, g_ref[...], m[0:1], m[1:2]).astype(BF16)
    row = _iota((rows, 1), 0)
    tok = ni * (row % SUBLANES) + row // SUBLANES
    valid = ((tok >= halo) | jnp.logical_not(first)) & ((tok < tm + halo) | jnp.logical_not(last))

    gate_ref[0] = _silu(_dot(hc, wg_ref[...])).astype(BF16)

    zab = _dot(hc, wab_ref[...])
    lane = _iota(zab.shape, 1)
    loga = -jnp.exp(alog_ref[...]) * jax.nn.softplus(zab + dtb_ref[...])
    ab_ref[0] = jnp.where(lane < 2 * n_heads, loga, jax.nn.sigmoid(zab))

    qk = n_heads * HEAD_DK
    nchan = wqkv_ref.shape[1]
    pad = GD_CONV_W // 2
    for cc in range(nchan // cw):
        c0 = cc * cw
        z = jnp.where(valid, _dot(he, wqkv_ref[:, c0:c0 + cw]), 0.0)
        z3 = z.reshape(ni, SUBLANES, cw)
        zlo = pltpu.roll(z3[ni - pad:], 1, 1)
        zhi = pltpu.roll(z3[:pad], SUBLANES - 1, 1)
        taps = [cw_ref[j:j + 1, c0:c0 + cw] for j in range(GD_CONV_W)]

        def group(i):
            return zlo[i + pad] if i < 0 else (zhi[i - ni] if i >= ni else z3[i])

        def finish(acc, i0):
            u = _silu(acc)
            if c0 < 2 * qk:
                scale = HEAD_DK ** -0.5 if c0 < qk else 1.0
                parts = []
                for j in range(cw // HEAD_DK):
                    tt = u[:, :, j * HEAD_DK:(j + 1) * HEAD_DK]
                    parts.append(tt * (lax.rsqrt(jnp.sum(tt * tt, axis=-1, keepdims=True) + EPS)
                                       * scale))
                u = jnp.concatenate(parts, axis=-1)
            for c in range(cw // LANES):
                for i in range(u.shape[0]):
                    ys_ref[c, pl.ds(i0 + i, SUBLANES, stride=ni), :] = u[i][:, tile(c)]

        inner = ni - 2 * pad
        acc = taps[0] * z3[0:inner]
        for j in range(1, GD_CONV_W):
            acc = acc + taps[j] * z3[j:j + inner]
        finish(acc, pad)
        for i0 in (0, ni - pad):
            edge = jnp.stack([sum(taps[j][0] * group(i + j - pad) for j in range(GD_CONV_W))
                              for i in range(i0, i0 + pad)], axis=0)
            finish(edge, i0)

        def centre(lane0, width):
            return jnp.concatenate([ys_ref[c, halo:halo + tm] for c in
                                    range(lane0 // LANES, (lane0 + width) // LANES)], axis=1)

        if c0 < 2 * qk:
            dst, base = (q_ref, c0) if c0 < qk else (k_ref, c0 - qk)
            for j in range(cw // HEAD_DK):
                dst[0, base // HEAD_DK + j] = centre(j * HEAD_DK, HEAD_DK).astype(BF16)
        else:
            base = c0 - 2 * qk
            for j in range(cw // GD_DV):
                v_ref[0, base // GD_DV + j] = centre(j * GD_DV, GD_DV).astype(BF16)


def _gd_inproj(x_ctx, x_lat, modsel, g, w_in, conv_w, a_log, dt_bias, *, tm):
    b, n_ctx, d = x_ctx.shape
    n_lat = x_lat.shape[1]
    rpc = n_lat // GRID_W
    assert rpc % SUBLANES == 0 and tm % rpc == 0 and n_ctx % tm == 0 and n_ctx > 0
    t = n_ctx + n_lat
    nc = n_ctx // tm
    cpb = tm // rpc
    assert cpb % SUBLANES == 0, "halo blocks take one 8-column group of the grid"
    lat_v = x_lat.reshape(b, rpc, GRID_W, d)
    n_heads = a_log.shape[-1]
    qk = n_heads * HEAD_DK
    vd = n_heads * GD_DV
    nchan = 2 * qk + vd
    nblk = t // tm
    cw = min(GD_COL_CHUNK, qk)
    wqkv = w_in[:, :nchan].astype(BF16)
    wg = w_in[:, nchan:nchan + vd].astype(BF16)
    wab = jnp.pad(w_in[:, nchan + vd:], ((0, 0), (0, LANES - 4 * n_heads))).astype(BF16)
    alog = jnp.pad(a_log.reshape(1, 2 * n_heads), ((0, 0), (0, LANES - 2 * n_heads)))
    dtb = jnp.pad(dt_bias.reshape(1, 2 * n_heads), ((0, 0), (0, LANES - 2 * n_heads)))
    g2 = g.reshape(1, d)
    full = lambda a: pl.BlockSpec(a.shape, lambda i, j: (0,) * a.ndim)
    spb = tm // SUBLANES
    last_slab = n_ctx // SUBLANES - 1
    gpb = cpb // SUBLANES
    last_cg = GRID_W // SUBLANES - 1
    return pl.pallas_call(
        functools.partial(_gd_inproj_kernel, nc=nc, nblk=nblk, n_heads=n_heads, tm=tm, cw=cw,
                          rpc=rpc),
        grid=(b, nblk),
        in_specs=[pl.BlockSpec((1, SUBLANES, d),
                               lambda i, j: (i, jnp.clip(j * spb - 1, 0, last_slab), 0)),
                  pl.BlockSpec((1, tm, d), lambda i, j: (i, jnp.minimum(j, nc - 1), 0)),
                  pl.BlockSpec((1, SUBLANES, d),
                               lambda i, j: (i, jnp.clip((j + 1) * spb, 0, last_slab), 0)),
                  pl.BlockSpec((1, SUBLANES, SUBLANES, d),
                               lambda i, j: (i, rpc // SUBLANES - 1,
                                             jnp.clip((j - nc) * gpb - 1, 0, last_cg), 0)),
                  pl.BlockSpec((1, rpc, cpb, d), lambda i, j: (i, 0, jnp.maximum(j - nc, 0), 0)),
                  pl.BlockSpec((1, SUBLANES, SUBLANES, d),
                               lambda i, j: (i, 0, jnp.clip((j - nc + 1) * gpb, 0, last_cg), 0)),
                  pl.BlockSpec((1, 1, 6, d), lambda i, j: (i, jnp.where(j >= nc, 1, 0), 0, 0)),
                  full(g2), full(wqkv), full(wg), full(wab), full(conv_w), full(alog), full(dtb)],
        out_specs=[pl.BlockSpec((1, n_heads, tm, HEAD_DK), lambda i, j: (i, 0, j, 0)),
                   pl.BlockSpec((1, n_heads, tm, HEAD_DK), lambda i, j: (i, 0, j, 0)),
                   pl.BlockSpec((1, n_heads, tm, GD_DV), lambda i, j: (i, 0, j, 0)),
                   pl.BlockSpec((1, tm, vd), lambda i, j: (i, j, 0)),
                   pl.BlockSpec((1, tm, LANES), lambda i, j: (i, j, 0))],
        out_shape=[jax.ShapeDtypeStruct((b, n_heads, t, HEAD_DK), BF16),
                   jax.ShapeDtypeStruct((b, n_heads, t, HEAD_DK), BF16),
                   jax.ShapeDtypeStruct((b, n_heads, t, GD_DV), BF16),
                   jax.ShapeDtypeStruct((b, t, vd), BF16),
                   jax.ShapeDtypeStruct((b, t, LANES), F32)],
        scratch_shapes=[pltpu.VMEM((d // LANES, tm + 2 * SUBLANES, LANES), F32),
                        pltpu.VMEM((cw // LANES, tm + 2 * SUBLANES, LANES), F32)],
        compiler_params=_cp("parallel", "parallel"),
        name="gd_inproj",
    )(x_ctx, x_ctx, x_ctx, lat_v, lat_v, lat_v, modsel, g2, wqkv, wg, wab, conv_w, alog, dtb)


def _neumann_inverses(mats, eye):
    n = eye.shape[0]
    ts = [eye - a for a in mats]
    ps = []
    for a in mats:
        ab = a.astype(BF16)
        ps.append(_dot(ab, ab))
    yield
    lvl = 2
    while lvl < CHUNK:
        for i in range(len(mats)):
            pb = ps[i].astype(BF16)
            tb = ts[i].astype(BF16)
            if 2 * lvl >= CHUNK:
                ts[i] = ts[i] + _dot(pb, tb)
            else:
                out = _dot(pb, jnp.concatenate([pb, tb], axis=1))
                ps[i] = out[:, :n]
                ts[i] = ts[i] + out[:, n:]
        yield
        lvl *= 2
    return ts


def _gdn_scan_block(q_ref, k_ref, v_ref, ab_ref, o_ref, s_ref, u_ref, l1_ref, l2_ref,
                    *, rev, n_heads, nck, tm):
    d = 1 if rev else 0

    def masks(n):
        r = _iota((n, n), 0)
        c = _iota((n, n), 1)
        same = (r // CHUNK) == (c // CHUNK)
        return (same & ((c >= r) if rev else (c <= r)), same & ((c > r) if rev else (c < r)),
                jnp.where(r == c, 1.0, 0.0))

    pb = min(tm, GD_SOLVE_ROWS)
    incl, strict, eye = masks(pb)
    dv = v_ref.shape[-1]

    ab = ab_ref[0]
    g3 = _dot(jnp.where(masks(tm)[0], 1.0, 0.0).astype(BF16),
              jnp.concatenate(_split3(ab), axis=1))
    gcol = g3[:, :LANES] + g3[:, LANES:2 * LANES] + g3[:, 2 * LANES:]
    eye_l = jnp.where(_iota((LANES, LANES), 0) == _iota((LANES, LANES), 1), 1.0, 0.0).astype(BF16)
    gr3 = _dot_nt(eye_l, jnp.concatenate(_split3(gcol), axis=0))
    grow = gr3[:, :tm] + gr3[:, tm:2 * tm] + gr3[:, 2 * tm:]

    incl_c, _ = _chunk_masks(rev)
    wide = lambda x, n: jnp.concatenate([x] * (n // LANES), axis=1) if n > LANES else x[:, :n]
    heads = range(n_heads)
    cas = [d * n_heads + h for h in heads]
    gcbs = [jnp.broadcast_to(gcol[:, ca:ca + 1], (tm, LANES)) for ca in cas]
    bcbs = [jnp.broadcast_to(ab[:, 2 * n_heads + ca:2 * n_heads + ca + 1], (tm, LANES))
            for ca in cas]

    mats = []
    for h in heads:
        k = k_ref[0, h]
        for p0 in range(0, tm, pb):
            gam = jnp.where(incl, jnp.exp(jnp.minimum(
                wide(gcbs[h][p0:p0 + pb], pb) - grow[cas[h]:cas[h] + 1, p0:p0 + pb], 0.0)), 0.0)
            kp = k[p0:p0 + pb]
            mats.append(jnp.where(strict, wide(bcbs[h][p0:p0 + pb], pb) * _dot_nt(kp, kp) * gam,
                                  0.0))
    yield
    tinvs = yield from _neumann_inverses(mats, eye)

    uws = []
    for h in heads:
        kf = k_ref[0, h].astype(F32)
        rhs = jnp.concatenate(
            [(wide(bcbs[h], dv) * v_ref[0, h].astype(F32)).astype(BF16),
             (bcbs[h] * jnp.exp(gcbs[h]) * kf).astype(BF16)], axis=1)
        uws.append(jnp.concatenate(
            [_dot(tinvs[h * (tm // pb) + i].astype(BF16), rhs[i * pb:(i + 1) * pb])
             for i in range(tm // pb)], axis=0))
    yield

    egl = {}
    for h in heads:
        ca, gcb, uw = cas[h], gcbs[h], uws[h]
        q = q_ref[0, h]
        k = k_ref[0, h]
        kf = k.astype(F32)
        egb = jnp.exp(gcb)
        u_ref[h] = uw[:, :dv]
        wb = uw[:, dv:].astype(BF16)
        qg = (q.astype(F32) * egb).astype(BF16)
        gls = [gcb[ci * CHUNK:ci * CHUNK + 1] if rev
               else gcb[(ci + 1) * CHUNK - 1:(ci + 1) * CHUNK] for ci in range(nck)]
        glb = jnp.concatenate([jnp.broadcast_to(g, (CHUNK, LANES)) for g in gls], axis=0)
        kd = kf * jnp.exp(glb - gcb)
        egl[h] = [jnp.exp(g[:, 0:1]) for g in gls]
        for ci in range(nck):
            c0 = ci * CHUNK
            l1_ref[h, ci, :CHUNK] = wb[c0:c0 + CHUNK]
            l1_ref[h, ci, CHUNK:] = qg[c0:c0 + CHUNK]
            gam_c = jnp.where(incl_c, jnp.exp(jnp.minimum(
                gcb[c0:c0 + CHUNK, :CHUNK] - grow[ca:ca + 1, c0:c0 + CHUNK], 0.0)), 0.0)
            qk_c = _dot_nt(q[c0:c0 + CHUNK], k[c0:c0 + CHUNK])
            l2_ref[h, ci, :CHUNK] = (qk_c * gam_c).astype(BF16)
            l2_ref[h, ci, CHUNK:] = kd[c0:c0 + CHUNK].T.astype(BF16)

    for step in range(nck):
        yield
        ci = nck - 1 - step if rev else step
        rows = pl.ds(ci * CHUNK, CHUNK)
        r1s = [_dot(l1_ref[h, ci], s_ref[h].astype(BF16)) for h in heads]
        weffs = [(u_ref[h, rows, :] - r1s[h][:CHUNK]).astype(BF16) for h in heads]
        yield
        r2s = [_dot(l2_ref[h, ci], weffs[h]) for h in heads]
        for h in heads:
            o_ref[0, h, rows, :] = (r2s[h][:CHUNK] + r1s[h][CHUNK:]).astype(o_ref.dtype)
            s_ref[h] = s_ref[h] * egl[h][ci] + r2s[h][CHUNK:]


def _gdn_scan_kernel(*refs, n_heads, nck, tm):
    fwd = refs[0:4] + refs[8:9] + refs[10:14]
    bwd = refs[4:8] + refs[9:10] + refs[14:18]

    @pl.when(pl.program_id(1) == 0)
    def _():
        for s_ref in (fwd[5], bwd[5]):
            s_ref[...] = jnp.zeros(s_ref.shape, F32)

    blocks = [_gdn_scan_block(*fwd, rev=False, n_heads=n_heads, nck=nck, tm=tm),
              _gdn_scan_block(*bwd, rev=True, n_heads=n_heads, nck=nck, tm=tm)]
    for _ in itertools.zip_longest(*blocks):
        pass


def _gdn_scan(q, k, v, ab, *, nc, tm, out_dtype):
    b, n_heads, t, dk = q.shape
    dv = v.shape[-1]
    nblk = t // tm
    nl = nblk - nc
    nck = tm // CHUNK

    def specs(rev):
        blk = lambda s: _scan_block_index(s, nc, nl, rev)
        return [pl.BlockSpec((1, n_heads, tm, dk), lambda i, s: (i, 0, blk(s), 0)),
                pl.BlockSpec((1, n_heads, tm, dk), lambda i, s: (i, 0, blk(s), 0)),
                pl.BlockSpec((1, n_heads, tm, dv), lambda i, s: (i, 0, blk(s), 0)),
                pl.BlockSpec((1, tm, LANES), lambda i, s: (i, blk(s), 0))]

    scratch = [pltpu.VMEM((n_heads, dk, dv), F32),
               pltpu.VMEM((n_heads, tm, dv), F32),
               pltpu.VMEM((n_heads, nck, 2 * CHUNK, dk), BF16),
               pltpu.VMEM((n_heads, nck, CHUNK + dk, CHUNK), BF16)]
    o_shape = jax.ShapeDtypeStruct((b, n_heads, t, dv), out_dtype)
    return pl.pallas_call(
        functools.partial(_gdn_scan_kernel, n_heads=n_heads, nck=nck, tm=tm),
        grid=(b, nblk),
        in_specs=specs(False) + specs(True),
        out_specs=[specs(False)[2], specs(True)[2]],
        out_shape=[o_shape, o_shape],
        scratch_shapes=scratch + scratch,
        compiler_params=_cp("parallel", "arbitrary"),
        name="gdn_scan",
    )(q, k, v, ab, q, k, v, ab)


def _route_kernel(x_ref, mod_ref, g_ref, wrt_ref, h_ref, slot_ref, aff_ref, lg_ref,
                  *, n_ctx, cap_c, cap_l, tr):
    n_exp, t = lg_ref.shape
    for rt in range(t // tr):
        m = mod_ref[0, 0 if rt * tr < n_ctx else 1]
        rows = slice(rt * tr, (rt + 1) * tr)
        h = _norm_mod(x_ref[0, rows, :], g_ref[...], m[3:4], m[4:5])
        h_ref[0, rows, :] = h.astype(BF16)
        lg_ref[:, rows] = _dot_nt_hi(wrt_ref[...], h)
    lg = lg_ref[...]
    e = jnp.exp(lg - jnp.max(lg, axis=0, keepdims=True))
    aff = e / jnp.sum(e, axis=0, keepdims=True)
    aff_ref[0] = aff
    bits = lax.bitcast_convert_type(aff, jnp.int32)
    lane = _iota((n_exp, t), 1)

    if n_ctx:
        is_ctx = lane < n_ctx
        regions = [(is_ctx, cap_c), (jnp.logical_not(is_ctx), cap_l)]
    else:
        regions = [(None, cap_l)]

    def count(pred, mask):
        p = pred if mask is None else (pred & mask)
        return jnp.sum(jnp.where(p, 1.0, 0.0), axis=1, keepdims=True)

    def thr_body(i, thrs):
        bit = lax.shift_left(jnp.int32(1), AFF_BITS - 1 - i)
        out = []
        for (mask, cap), thr in zip(regions, thrs):
            cand = thr | bit
            out.append(jnp.where(count(bits >= cand, mask) >= cap, cand, thr))
        return tuple(out)

    zero = jnp.zeros((n_exp, 1), jnp.int32)
    thrs = lax.fori_loop(0, AFF_BITS, thr_body, tuple(zero for _ in regions))

    idx_bits = t.bit_length()
    sel = None
    for (mask, cap), thr in zip(regions, thrs):
        gt = bits > thr
        tie = bits == thr
        need = cap - count(gt, mask)

        def j_body(i, j, tie=tie, mask=mask, need=need):
            cand = j | lax.shift_left(jnp.int32(1), idx_bits - 1 - i)
            return jnp.where(count(tie & (lane < cand), mask) <= need, cand, j)

        jmax = lax.fori_loop(0, idx_bits, j_body, zero)
        s = gt | (tie & (lane < jmax))
        if mask is not None:
            s = s & mask
        sel = s if sel is None else (sel | s)

    lt = LANES if t % LANES == 0 else CHUNK
    ut = jnp.where(_iota((lt, lt), 0) < _iota((lt, lt), 1), 1.0, 0.0).astype(BF16)
    self = jnp.where(sel, 1.0, 0.0)
    run = jnp.zeros((n_exp, 1), F32)
    pres = []
    for i in range(t // lt):
        tile = self[:, i * lt:(i + 1) * lt]
        pres.append(_dot(tile.astype(BF16), ut) + run)
        run = run + jnp.sum(tile, axis=1, keepdims=True)
    pre = jnp.concatenate(pres, axis=1)
    if n_ctx:
        slot = jnp.where(is_ctx, pre + cap_l, pre - cap_c)
    else:
        slot = pre
    slot_ref[0] = jnp.where(sel, slot, -1.0)


def _route(x, modsel, g, w_router, *, n_ctx, cap_c, cap_l, tr):
    b, t, d = x.shape
    n_exp = w_router.shape[-1]
    wrt = w_router.T
    g2 = g.reshape(1, d)
    return pl.pallas_call(
        functools.partial(_route_kernel, n_ctx=n_ctx, cap_c=cap_c, cap_l=cap_l, tr=tr),
        grid=(b,),
        in_specs=[pl.BlockSpec((1, t, d), lambda i: (i, 0, 0)),
                  pl.BlockSpec((1, 2, 6, d), lambda i: (i, 0, 0, 0)),
                  pl.BlockSpec((1, d), lambda i: (0, 0)),
                  pl.BlockSpec((n_exp, d), lambda i: (0, 0))],
        out_specs=[pl.BlockSpec((1, t, d), lambda i: (i, 0, 0)),
                   pl.BlockSpec((1, n_exp, t), lambda i: (i, 0, 0)),
                   pl.BlockSpec((1, n_exp, t), lambda i: (i, 0, 0))],
        out_shape=[jax.ShapeDtypeStruct((b, t, d), BF16),
                   jax.ShapeDtypeStruct((b, n_exp, t), F32),
                   jax.ShapeDtypeStruct((b, n_exp, t), F32)],
        scratch_shapes=[pltpu.VMEM((n_exp, t), F32)],
        compiler_params=_cp("parallel"),
        name="moe_route",
    )(x, modsel, g2, wrt)


def _gather_kernel(slot_ref, h_ref, x_ref, *, n_ctx, cap_c, cap_l):
    slot = slot_ref[0, 0]
    t = slot.shape[1]
    sl = slot[:, n_ctx:]
    p = jnp.where(sl == _iota((cap_l, t - n_ctx), 0).astype(F32), 1.0, 0.0).astype(BF16)
    x_ref[0, 0, 0:cap_l, :] = _dot(p, h_ref[0, n_ctx:, :]).astype(BF16)
    if n_ctx:
        sc = slot[:, :n_ctx] - float(cap_l)
        p = jnp.where(sc == _iota((cap_c, n_ctx), 0).astype(F32), 1.0, 0.0).astype(BF16)
        x_ref[0, 0, cap_l:, :] = _dot(p, h_ref[0, :n_ctx, :]).astype(BF16)


def _gather(h, slot_row, *, n_ctx, cap_c, cap_l):
    b, t, d = h.shape
    n_exp = slot_row.shape[1]
    m = cap_l + (cap_c if n_ctx else 0)
    return pl.pallas_call(
        functools.partial(_gather_kernel, n_ctx=n_ctx, cap_c=cap_c, cap_l=cap_l),
        grid=(b, n_exp),
        in_specs=[pl.BlockSpec((1, 1, 1, t), lambda i, e: (i, e, 0, 0)),
                  pl.BlockSpec((1, t, d), lambda i, e: (i, 0, 0))],
        out_specs=pl.BlockSpec((1, 1, m, d), lambda i, e: (e, i, 0, 0)),
        out_shape=jax.ShapeDtypeStruct((n_exp, b, m, d), BF16),
        compiler_params=_cp("parallel", "parallel"),
        name="moe_gather",
    )(slot_row.reshape(b, n_exp, 1, t), h)


def _ffn_kernel(x_ref, wg_ref, wu_ref, wd_ref, y_ref, acc_ref, wgb_ref, wub_ref, wdb_ref, *, tr):
    ft = pl.program_id(1)

    @pl.when(ft == 0)
    def _():
        acc_ref[...] = jnp.zeros(acc_ref.shape, F32)

    wgb_ref[...] = wg_ref[...].astype(BF16)
    wub_ref[...] = wu_ref[...].astype(BF16)
    wdb_ref[...] = wd_ref[...].astype(BF16)

    def body(i, carry):
        rows = pl.ds(pl.multiple_of(i * tr, tr), tr)
        x = x_ref[0, rows, :]
        hid = (_silu(_dot(x, wgb_ref[...])) * _dot(x, wub_ref[...])).astype(BF16)
        acc_ref[rows, :] += _dot(hid, wdb_ref[...])
        return carry

    lax.fori_loop(0, x_ref.shape[1] // tr, body, 0)

    @pl.when(ft == pl.num_programs(1) - 1)
    def _():
        y_ref[0] = acc_ref[...].astype(y_ref.dtype)


def _ffn(xs, w_gate, w_up, w_down, *, layer):
    n_exp, bm, d = xs.shape
    f = w_gate.shape[-1]
    tf = FFN_TILE if f % FFN_TILE == 0 else f
    tr = bm // (-(-bm // FFN_MAX_ROWS))
    return pl.pallas_call(
        functools.partial(_ffn_kernel, tr=tr),
        grid=(n_exp, f // tf),
        in_specs=[pl.BlockSpec((1, bm, d), lambda e, j: (e, 0, 0)),
                  pl.BlockSpec((None, None, d, tf), lambda e, j: (layer, e, 0, j)),
                  pl.BlockSpec((None, None, d, tf), lambda e, j: (layer, e, 0, j)),
                  pl.BlockSpec((None, None, tf, d), lambda e, j: (layer, e, j, 0))],
        out_specs=pl.BlockSpec((1, bm, d), lambda e, j: (e, 0, 0)),
        out_shape=jax.ShapeDtypeStruct((n_exp, bm, d), BF16),
        scratch_shapes=[pltpu.VMEM((bm, d), F32),
                        pltpu.VMEM((d, tf), BF16),
                        pltpu.VMEM((d, tf), BF16),
                        pltpu.VMEM((tf, d), BF16)],
        compiler_params=_cp("parallel", "arbitrary"),
        name="moe_ffn",
    )(xs, w_gate, w_up, w_down)


def _combine_kernel(*refs, nc, cap_c, cap_l, final):
    x_ref, ys_ref, sc_ref, ac_ref, mod_ref = refs[:5]
    fg_ref = refs[5] if final else None
    outs = refs[5 + bool(final):-1]
    acc_ref = refs[-1]
    n_exp = ys_ref.shape[0]
    tm = x_ref.shape[1]

    def run(k0, kk, o_ref):
        sc = sc_ref[0]
        ac = ac_ref[0]
        lane = (_iota((tm, kk), 1) + k0).astype(F32)
        acc_ref[...] = jnp.zeros(acc_ref.shape, F32)
        for e in range(n_exp):
            pt = jnp.where(sc[:, e:e + 1] == lane, 1.0, 0.0).astype(BF16)
            acc_ref[...] += ac[:, e:e + 1] * _dot(pt, ys_ref[e, 0, k0:k0 + kk, :])
        x2 = x_ref[0] + mod_ref[0, 0][5:6] * acc_ref[...]
        if final:
            x2 = x2 * lax.rsqrt(jnp.mean(x2 * x2, axis=-1, keepdims=True) + EPS) * fg_ref[...]
        o_ref[0] = x2

    if nc:
        t = pl.program_id(1)

        @pl.when(t < nc)
        def _():
            run(cap_l, cap_c, outs[0])

        @pl.when(t >= nc)
        def _():
            run(0, cap_l, outs[1])
    else:
        run(0, cap_l, outs[0])


def _combine(x, ys, slot_col, aff_col, modsel, final_g, *, nc, cap_c, cap_l, tm):
    b, t, d = x.shape
    n_exp, _, m, _ = ys.shape
    seg = (lambda j: jnp.where(j >= nc, 1, 0)) if nc else (lambda j: 1)
    if nc:
        out_specs = [pl.BlockSpec((1, tm, d), lambda i, j: (i, jnp.minimum(j, nc - 1), 0)),
                     pl.BlockSpec((1, tm, d), lambda i, j: (i, jnp.maximum(j - nc, 0), 0))]
        out_shape = [jax.ShapeDtypeStruct((b, nc * tm, d), F32),
                     jax.ShapeDtypeStruct((b, t - nc * tm, d), F32)]
    else:
        out_specs = pl.BlockSpec((1, tm, d), lambda i, j: (i, j, 0))
        out_shape = jax.ShapeDtypeStruct((b, t, d), F32)
    in_specs = [pl.BlockSpec((1, tm, d), lambda i, j: (i, j, 0)),
                pl.BlockSpec((n_exp, 1, m, d), lambda i, j: (0, i, 0, 0)),
                pl.BlockSpec((1, tm, n_exp), lambda i, j: (i, j, 0)),
                pl.BlockSpec((1, tm, n_exp), lambda i, j: (i, j, 0)),
                pl.BlockSpec((1, 1, 6, d), lambda i, j: (i, seg(j), 0, 0))]
    args = [x, ys, slot_col, aff_col, modsel]
    if final_g is not None:
        in_specs.append(pl.BlockSpec((1, d), lambda i, j: (0, 0)))
        args.append(final_g.reshape(1, d))
    return pl.pallas_call(
        functools.partial(_combine_kernel, nc=nc, cap_c=cap_c, cap_l=cap_l,
                          final=final_g is not None),
        grid=(b, t // tm),
        in_specs=in_specs,
        out_specs=out_specs,
        out_shape=out_shape,
        scratch_shapes=[pltpu.VMEM((tm, d), F32)],
        compiler_params=_cp("parallel", "arbitrary"),
        name="moe_combine",
    )(*args)


def _moe(x, modsel, g, w_router, w_gate, w_up, w_down, final_g, *, layer, n_ctx, n_lat, tm):
    b, t, d = x.shape
    n_exp = w_router.shape[-1]
    cap_l = EC_CAPACITY_FACTOR * n_lat // n_exp
    cap_c = EC_CAPACITY_FACTOR * n_ctx // n_exp
    h, slot_row, aff_row = _route(x, modsel, g, w_router, n_ctx=n_ctx, cap_c=cap_c, cap_l=cap_l,
                                  tr=tm)
    xs = _gather(h, slot_row, n_ctx=n_ctx, cap_c=cap_c, cap_l=cap_l)
    m = xs.shape[2]
    ys = _ffn(xs.reshape(n_exp, b * m, d), w_gate, w_up, w_down, layer=layer)
    return _combine(x, ys.reshape(n_exp, b, m, d), jnp.swapaxes(slot_row, 1, 2),
                    jnp.swapaxes(aff_row, 1, 2), modsel, final_g,
                    nc=n_ctx // tm, cap_c=cap_c, cap_l=cap_l, tm=tm)


def kernel(x, c, ctx, c_ctx, w_mod, b_mod, norm_g, hg_w_in, hg_lb, hg_onorm, hg_w_out, gd_w_in,
           gd_conv, gd_a_log, gd_dt_bias, gd_onorm, gd_w_out, moe_router, moe_w_gate, moe_w_up,
           moe_w_down, final_g):
    bsz, n_lat, d = x.shape
    n_ctx = ctx.shape[1]
    depth = w_mod.shape[0]
    assert depth == 2, "layer 0 = HGRN2, layer 1 = gated DeltaNet"
    tm = math.gcd(math.gcd(n_ctx, n_lat), MAX_BLOCK_ROWS)
    assert tm % CHUNK == 0
    nc = n_ctx // tm
    o_dt = BF16

    rows = -(-(bsz + 1) // SUBLANES) * SUBLANES
    cond = jnp.zeros((rows, d), F32).at[:bsz].set(c).at[bsz].set(c_ctx)
    mod = _adaln(cond, w_mod, b_mod).reshape(depth, rows, 6, d)

    def modsel(i):
        ctx_mod = jnp.broadcast_to(mod[i, bsz][None], (bsz, 6, d))
        return jnp.stack([ctx_mod, mod[i, :bsz]], axis=1)

    ms = modsel(0)
    q, lf, v, gate = _hg_inproj(ctx, x, ms, norm_g[0, 0], hg_lb, hg_w_in[0], layer=0, nc=nc, tm=tm)
    o_f, o_b = _gla_scan(q, lf, v, nc=nc, tm=tm, out_dtype=o_dt)
    x_all = _outproj(o_f, o_b, gate, hg_onorm[0], hg_w_out[0], ctx, x, ms, nc=nc, tm=tm)
    x_ctx, x_lat = _moe(x_all, ms, norm_g[0, 1], moe_router[0], moe_w_gate, moe_w_up, moe_w_down,
                        None, layer=0, n_ctx=n_ctx, n_lat=n_lat, tm=tm)

    ms = modsel(1)
    rpc = n_lat // GRID_W
    q, k, v, gate, ab = _gd_inproj(x_ctx, x_lat, ms, norm_g[1, 0], gd_w_in[0], gd_conv[0],
                                   gd_a_log[0], gd_dt_bias[0], tm=tm)
    o_f, o_b = _gdn_scan(q, k, v, ab, nc=nc, tm=tm, out_dtype=o_dt)
    x_lat = _outproj(o_f, o_b, gate, gd_onorm[0], gd_w_out[0], x_ctx,
                     x_lat.reshape(bsz, rpc, GRID_W, d), ms, nc=nc, tm=tm, rpc=rpc)
    x_lat = x_lat.reshape(bsz, n_lat, d)
    return _moe(x_lat, ms, norm_g[1, 1], moe_router[1], moe_w_gate, moe_w_up, moe_w_down, final_g,
                layer=1, n_ctx=0, n_lat=n_lat, tm=tm)
```

```python
import functools
import itertools
import math

import jax
import jax.numpy as jnp
from jax import lax
from jax.experimental import pallas as pl
from jax.experimental.pallas import tpu as pltpu

F32 = jnp.float32
BF16 = jnp.bfloat16
HIGHEST = lax.Precision.HIGHEST

EPS = 1e-6
LANES = 128
SUBLANES = 8
V7X_VMEM_BYTES = 64 * 1024 * 1024
VMEM_LIMIT = V7X_VMEM_BYTES * 7 // 8

HEAD_DK = 128
GD_DV = 256
GD_CONV_W = 5
GRID_W = 64
EC_CAPACITY_FACTOR = 2
CHUNK = 64
HALF = CHUNK // 2
GLA_ATT_ROWS = 2 * CHUNK
GD_SOLVE_ROWS = 2 * CHUNK
GD_COL_CHUNK = 512
ADALN_COL_TILES = 12
EXP_CLAMP = 38.0
MAX_BLOCK_ROWS = 256
FFN_TILE = 256
FFN_MAX_ROWS = 2304
AFF_BITS = 31


def _cp(*sem):
    return pltpu.CompilerParams(dimension_semantics=sem, vmem_limit_bytes=VMEM_LIMIT)


def _dot(a, b):
    return jnp.dot(a, b, preferred_element_type=F32)


def _dot_nt(a, b):
    return lax.dot_general(a, b, (((1,), (1,)), ((), ())), preferred_element_type=F32)


def _dot_tn(a, b):
    return lax.dot_general(a, b, (((0,), (0,)), ((), ())), preferred_element_type=F32)


def _dot_hi(a, b):
    return jnp.dot(a, b, preferred_element_type=F32, precision=HIGHEST)


def _dot_nt_hi(a, b):
    return lax.dot_general(a, b, (((1,), (1,)), ((), ())), preferred_element_type=F32,
                           precision=HIGHEST)


def _silu(x):
    return x * jax.nn.sigmoid(x)


def _norm_mod(x, g, shift, scale):
    y = x * lax.rsqrt(jnp.mean(x * x, axis=-1, keepdims=True) + EPS)
    return y * (g * (1.0 + scale)) + shift


def _iota(shape, dim):
    return lax.broadcasted_iota(jnp.int32, shape, dim)


def _split3(x):
    hi = x.astype(BF16)
    r1 = x - hi.astype(F32)
    mid = r1.astype(BF16)
    lo = (r1 - mid.astype(F32)).astype(BF16)
    return [hi, mid, lo]


def _adaln_kernel(c_ref, w_ref, b_ref, o_ref):
    o_ref[0] = _dot_hi(_silu(c_ref[...]), w_ref[0]) + b_ref[0]


def _adaln(cond, w_mod, b_mod):
    depth, d, n = w_mod.shape
    r = cond.shape[0]
    tn = n // ADALN_COL_TILES
    return pl.pallas_call(
        _adaln_kernel,
        grid=(depth, n // tn),
        in_specs=[pl.BlockSpec((r, d), lambda i, j: (0, 0)),
                  pl.BlockSpec((1, d, tn), lambda i, j: (i, 0, j)),
                  pl.BlockSpec((1, 1, tn), lambda i, j: (i, 0, j))],
        out_specs=pl.BlockSpec((1, r, tn), lambda i, j: (i, 0, j)),
        out_shape=jax.ShapeDtypeStruct((depth, r, n), F32),
        compiler_params=_cp("parallel", "parallel"),
        name="adaln",
    )(cond, w_mod, b_mod.reshape(depth, 1, n))


def _hg_inproj_kernel(c_ref, x_ref, mod_ref, g_ref, lb_ref, wq_ref, wf_ref, wi_ref, wg_ref,
                      q_ref, lf_ref, v_ref, gate_ref, *, layer, n_heads, nc):
    m = mod_ref[0, 0]
    xb = jnp.where(pl.program_id(1) < nc, c_ref[0], x_ref[0])
    hb = _norm_mod(xb, g_ref[...], m[0:1], m[1:2]).astype(BF16)
    fdim = wq_ref.shape[1]
    dk = fdim // n_heads
    dv = wi_ref.shape[1] // n_heads

    q = _silu(_dot(hb, wq_ref[...]))
    for h in range(n_heads):
        q_ref[0, h] = q[:, h * dk:(h + 1) * dk].astype(BF16)

    lbp = lb_ref[...]
    e = jnp.exp(lbp - jnp.max(lbp, axis=0))
    lb = jnp.sum(e[:layer + 1], axis=0) / jnp.sum(e, axis=0)

    zf = _dot(hb, wf_ref[...])
    for d in range(2):
        lbd = lb[d:d + 1]
        f = lbd + (1.0 - lbd) * jax.nn.sigmoid(zf[:, d * fdim:(d + 1) * fdim])
        lf = jnp.log(f)
        for h in range(n_heads):
            lf_ref[d, 0, h] = lf[:, h * dk:(h + 1) * dk]

    v = _dot(hb, wi_ref[...])
    for h in range(n_heads):
        v_ref[0, h] = v[:, h * dv:(h + 1) * dv].astype(BF16)

    gate_ref[0] = _silu(_dot(hb, wg_ref[...])).astype(BF16)


def _hg_inproj(ctx, x, modsel, g, hg_lb, w_in, *, layer, nc, tm):
    b, n_lat, d = x.shape
    t = ctx.shape[1] + n_lat
    fdim = hg_lb.shape[-1]
    n_heads = fdim // HEAD_DK
    dv = d // n_heads
    wq = w_in[:, :fdim].astype(BF16)
    wf = w_in[:, fdim:3 * fdim].astype(BF16)
    wi = w_in[:, 3 * fdim:3 * fdim + d].astype(BF16)
    wg = w_in[:, 3 * fdim + d:].astype(BF16)
    full = lambda a: pl.BlockSpec(a.shape, lambda i, j: (0,) * a.ndim)
    g2 = g.reshape(1, d)
    return pl.pallas_call(
        functools.partial(_hg_inproj_kernel, layer=layer, n_heads=n_heads, nc=nc),
        grid=(b, t // tm),
        in_specs=[pl.BlockSpec((1, tm, d), lambda i, j: (i, jnp.minimum(j, nc - 1), 0)),
                  pl.BlockSpec((1, tm, d), lambda i, j: (i, jnp.maximum(j - nc, 0), 0)),
                  pl.BlockSpec((1, 1, 6, d), lambda i, j: (i, jnp.where(j >= nc, 1, 0), 0, 0)),
                  full(g2), full(hg_lb), full(wq), full(wf), full(wi), full(wg)],
        out_specs=[pl.BlockSpec((1, n_heads, tm, HEAD_DK), lambda i, j: (i, 0, j, 0)),
                   pl.BlockSpec((2, 1, n_heads, tm, HEAD_DK), lambda i, j: (0, i, 0, j, 0)),
                   pl.BlockSpec((1, n_heads, tm, dv), lambda i, j: (i, 0, j, 0)),
                   pl.BlockSpec((1, tm, d), lambda i, j: (i, j, 0))],
        out_shape=[jax.ShapeDtypeStruct((b, n_heads, t, HEAD_DK), BF16),
                   jax.ShapeDtypeStruct((2, b, n_heads, t, HEAD_DK), F32),
                   jax.ShapeDtypeStruct((b, n_heads, t, dv), BF16),
                   jax.ShapeDtypeStruct((b, t, d), BF16)],
        compiler_params=_cp("parallel", "parallel"),
        name="hg_inproj",
    )(ctx, x, modsel, g2, hg_lb, wq, wf, wi, wg)


def _scan_block_index(step, nc, nl, rev):
    if not rev:
        return step
    return jnp.where(step < nc, nc - 1 - step, nc + nl - 1 - (step - nc))


def _chunk_masks(rev):
    r = _iota((CHUNK, CHUNK), 0)
    c = _iota((CHUNK, CHUNK), 1)
    incl = (c >= r) if rev else (c <= r)
    strict = (c > r) if rev else (c < r)
    return incl, strict


def _gla_scan_kernel(*refs, n_heads, nck, tm):
    dirs = [(False,) + refs[0:3] + refs[6:7] + refs[8:12],
            (True,) + refs[3:6] + refs[7:8] + refs[12:16]]

    @pl.when(pl.program_id(1) == 0)
    def _():
        for d in dirs:
            d[5][...] = jnp.zeros(d[5].shape, F32)

    dk = refs[0].shape[-1]
    r = _iota((tm, tm), 0)
    c = _iota((tm, tm), 1)
    same_chunk = (r // CHUNK) == (c // CHUNK)
    same_half = (r // HALF) == (c // HALF)

    def per_rows(bh, row_of, span):
        return jnp.concatenate([jnp.broadcast_to(bh[row_of(x0):row_of(x0) + 1], (span, dk))
                                for x0 in range(0, tm, span)], axis=0)

    heads = range(n_heads)
    jobs = [(d, h) for h in heads for d in dirs]
    ab = min(tm, GLA_ATT_ROWS)

    bhs = []
    for (rev, q_ref, lf_ref, *_), h in jobs:
        causal = (c >= r) if rev else (c <= r)
        lmb = jnp.where(same_chunk & causal, 1.0, 0.0).astype(BF16)
        g3 = _dot(lmb, jnp.concatenate(_split3(lf_ref[0, 0, h]), axis=1))
        bhs.append(g3[:, :dk] + g3[:, dk:2 * dk] + g3[:, 2 * dk:])

    decs, atts = [], []
    for ((rev, q_ref, lf_ref, v_ref, o_ref, st_ref, oin_ref, qc_ref, kh_ref), h), bh in zip(jobs,
                                                                                            bhs):
        q = q_ref[0, h].astype(F32)
        k = 1.0 - jnp.exp(lf_ref[0, 0, h])
        bls = [bh[ci * CHUNK:ci * CHUNK + 1] if rev
               else bh[(ci + 1) * CHUNK - 1:(ci + 1) * CHUNK] for ci in range(nck)]
        blc = jnp.concatenate([jnp.broadcast_to(x, (CHUNK, dk)) for x in bls], axis=0)
        qc_ref[h] = (q * jnp.exp(bh)).astype(BF16)
        kh_ref[h] = (k * jnp.exp(blc - bh)).astype(BF16)
        decs.append([jnp.exp(x) for x in bls])
        bm = per_rows(bh, lambda x0: x0 + HALF // 2, HALF)
        qd = (q * jnp.exp(jnp.minimum(bh - bm, EXP_CLAMP))).astype(BF16)
        kd = (k * jnp.exp(jnp.minimum(bm - bh, EXP_CLAMP))).astype(BF16)
        be = per_rows(bh, lambda x0: x0 + (HALF if rev else HALF - 1), CHUNK)
        ex = jnp.exp(-jnp.abs(bh - be))
        qx = (q * ex).astype(BF16)
        kx = (k * ex).astype(BF16)
        atts.append([(_dot_nt(qd[p0:p0 + ab], kd[p0:p0 + ab]), _dot_nt(qx[p0:p0 + ab], kx[p0:p0 + ab]))
                     for p0 in range(0, tm, ab)])

    for ((rev, _, _, v_ref, _, _, oin_ref, _, _), h), att_blocks in zip(jobs, atts):
        causal = (c >= r) if rev else (c <= r)
        second = 0 if rev else 1
        mask_d = (same_half & causal)[:ab, :ab]
        mask_x = (same_chunk & ((r // HALF) % 2 == second) & ((c // HALF) % 2 == 1 - second))[:ab, :ab]
        for i, (att_d, att_x) in enumerate(att_blocks):
            att = jnp.where(mask_d, att_d, 0.0) + jnp.where(mask_x, att_x, 0.0)
            oin_ref[h, i * ab:(i + 1) * ab, :] = _dot(att.astype(BF16),
                                                     v_ref[0, h, i * ab:(i + 1) * ab, :])

    for step in range(nck):
        for ((rev, _, _, v_ref, o_ref, st_ref, oin_ref, qc_ref, kh_ref), h), dec in zip(jobs, decs):
            ci = nck - 1 - step if rev else step
            rows = pl.ds(ci * CHUNK, CHUNK)
            st = st_ref[h]
            o = oin_ref[h, rows, :] + _dot_nt(qc_ref[h, rows, :], st.astype(BF16))
            o_ref[0, h, rows, :] = o.astype(o_ref.dtype)
            st_ref[h] = st * dec[ci] + _dot_tn(v_ref[0, h, rows, :], kh_ref[h, rows, :])


def _gla_scan(q, lf, v, *, nc, tm, out_dtype):
    b, n_heads, t, dk = q.shape
    dv = v.shape[-1]
    nblk = t // tm
    nl = nblk - nc
    nck = tm // CHUNK

    def specs(rev):
        d = 1 if rev else 0
        blk = lambda s: _scan_block_index(s, nc, nl, rev)
        return [pl.BlockSpec((1, n_heads, tm, dk), lambda i, s: (i, 0, blk(s), 0)),
                pl.BlockSpec((1, 1, n_heads, tm, dk), lambda i, s: (d, i, 0, blk(s), 0)),
                pl.BlockSpec((1, n_heads, tm, dv), lambda i, s: (i, 0, blk(s), 0))]

    scratch = [pltpu.VMEM((n_heads, dv, dk), F32),
               pltpu.VMEM((n_heads, tm, dv), F32),
               pltpu.VMEM((n_heads, tm, dk), BF16),
               pltpu.VMEM((n_heads, tm, dk), BF16)]
    o_shape = jax.ShapeDtypeStruct((b, n_heads, t, dv), out_dtype)
    return pl.pallas_call(
        functools.partial(_gla_scan_kernel, n_heads=n_heads, nck=nck, tm=tm),
        grid=(b, nblk),
        in_specs=specs(False) + specs(True),
        out_specs=[specs(False)[2], specs(True)[2]],
        out_shape=[o_shape, o_shape],
        scratch_shapes=scratch + scratch,
        compiler_params=_cp("parallel", "arbitrary"),
        name="gla_scan",
    )(q, lf, v, q, lf, v)


def _outproj_kernel(of_ref, ob_ref, gate_ref, gain_ref, w_ref, c_ref, x_ref, mod_ref, o_ref, y_ref,
                    *, n_heads, nc, rpc):
    dv = of_ref.shape[-1]
    for h in range(n_heads):
        o = of_ref[0, h].astype(F32) + ob_ref[0, h].astype(F32)
        cs = slice(h * dv, (h + 1) * dv)
        o = o * lax.rsqrt(jnp.mean(o * o, axis=-1, keepdims=True) + EPS) * gain_ref[:, cs]
        y_ref[:, cs] = (o * gate_ref[0, :, cs].astype(F32)).astype(BF16)
    y = mod_ref[0, 0][2:3] * _dot(y_ref[...], w_ref[...])
    if not rpc:
        o_ref[0] = jnp.where(pl.program_id(1) < nc, c_ref[0], x_ref[0]) + y
    else:
        for j in range(y.shape[0] // rpc):
            o_ref[0, :, j, :] = x_ref[0, :, j, :] + y[j * rpc:(j + 1) * rpc]


def _outproj(o_f, o_b, gate, gain, w_out, x_ctx, x_lat, modsel, *, nc, tm, rpc=0):
    b, n_heads, t, dv = o_f.shape
    d = w_out.shape[-1]
    hv = n_heads * dv
    off = nc if rpc else 0
    nblk = t // tm - off
    seg = (lambda j: 1) if rpc else (lambda j: jnp.where(j >= nc, 1, 0))
    w = w_out.astype(BF16)
    gain2 = gain.reshape(1, hv)
    if rpc:
        x_spec = o_spec = pl.BlockSpec((1, rpc, tm // rpc, d), lambda i, j: (i, 0, j, 0))
        o_shape = x_lat.shape
    else:
        x_spec = pl.BlockSpec((1, tm, d), lambda i, j: (i, jnp.maximum(j - nc, 0), 0))
        o_spec = pl.BlockSpec((1, tm, d), lambda i, j: (i, j, 0))
        o_shape = (b, t, d)
    return pl.pallas_call(
        functools.partial(_outproj_kernel, n_heads=n_heads, nc=nc, rpc=rpc),
        grid=(b, nblk),
        in_specs=[pl.BlockSpec((1, n_heads, tm, dv), lambda i, j: (i, 0, j + off, 0)),
                  pl.BlockSpec((1, n_heads, tm, dv), lambda i, j: (i, 0, j + off, 0)),
                  pl.BlockSpec((1, tm, hv), lambda i, j: (i, j + off, 0)),
                  pl.BlockSpec((1, hv), lambda i, j: (0, 0)),
                  pl.BlockSpec((hv, d), lambda i, j: (0, 0)),
                  pl.BlockSpec((1, tm, d), lambda i, j: (i, jnp.minimum(j, nc - 1), 0)),
                  x_spec,
                  pl.BlockSpec((1, 1, 6, d), lambda i, j: (i, seg(j), 0, 0))],
        out_specs=o_spec,
        out_shape=jax.ShapeDtypeStruct(o_shape, F32),
        scratch_shapes=[pltpu.VMEM((tm, hv), BF16)],
        compiler_params=_cp("parallel", "parallel"),
        name="outproj",
    )(o_f, o_b, gate, gain2, w, x_ctx, x_lat, modsel)


def _gd_inproj_kernel(cp_ref, cc_ref, cn_ref, lp_ref, lc_ref, ln_ref, mod_ref, g_ref, wqkv_ref,
                      wg_ref, wab_ref, cw_ref, alog_ref, dtb_ref, q_ref, k_ref, v_ref, gate_ref,
                      ab_ref, xs_ref, ys_ref, *, nc, nblk, n_heads, tm, cw, rpc):
    t = pl.program_id(1)
    first = (t == 0) | (t == nc)
    last = (t == nc - 1) | (t == nblk - 1)
    m = mod_ref[0, 0]
    halo = SUBLANES
    nlt = xs_ref.shape[0]
    d = nlt * LANES
    tile = lambda c: slice(c * LANES, (c + 1) * LANES)

    @pl.when(t < nc)
    def _():
        for c in range(nlt):
            xs_ref[c, 0:halo] = cp_ref[0, :, tile(c)]
            xs_ref[c, halo:halo + tm] = cc_ref[0, :, tile(c)]
            xs_ref[c, halo + tm:] = cn_ref[0, :, tile(c)]

    @pl.when(t >= nc)
    def _():
        cpb = tm // rpc
        for c in range(nlt):
            xs_ref[c, 0:halo] = lp_ref[0, :, SUBLANES - 1, tile(c)]
            for j in range(cpb):
                xs_ref[c, halo + j * rpc:halo + (j + 1) * rpc] = lc_ref[0, :, j, tile(c)]
            xs_ref[c, halo + tm:] = ln_ref[0, :, 0, tile(c)]

    xc = jnp.concatenate([xs_ref[c, halo:halo + tm] for c in range(nlt)], axis=1)
    hc = _norm_mod(xc, g_ref[...], m[0:1], m[1:2]).astype(BF16)

    rows = tm + 2 * halo
    ni = rows // SUBLANES
    xp = jnp.concatenate(
        [jnp.concatenate([xs_ref[c, pl.ds(i, SUBLANES, stride=ni), :] for i in range(ni)], axis=0)
         for c in range(nlt)], axis=1)
    he = _norm_mod(xp, g_ref[...], m[0:1], m[1:2]).astype(BF16)
    row = _iota((rows, 1), 0)
    tok = ni * (row % SUBLANES) + row // SUBLANES
    valid = ((tok >= halo) | jnp.logical_not(first)) & ((tok < tm + halo) | jnp.logical_not(last))

    gate_ref[0] = _silu(_dot(hc, wg_ref[...])).astype(BF16)

    zab = _dot(hc, wab_ref[...])
    lane = _iota(zab.shape, 1)
    loga = -jnp.exp(alog_ref[...]) * jax.nn.softplus(zab + dtb_ref[...])
    ab_ref[0] = jnp.where(lane < 2 * n_heads, loga, jax.nn.sigmoid(zab))

    qk = n_heads * HEAD_DK
    nchan = wqkv_ref.shape[1]
    pad = GD_CONV_W // 2
    for cc in range(nchan // cw):
        c0 = cc * cw
        z = jnp.where(valid, _dot(he, wqkv_ref[:, c0:c0 + cw]), 0.0)
        z3 = z.reshape(ni, SUBLANES, cw)
        zlo = pltpu.roll(z3[ni - pad:], 1, 1)
        zhi = pltpu.roll(z3[:pad], SUBLANES - 1, 1)
        taps = [cw_ref[j:j + 1, c0:c0 + cw] for j in range(GD_CONV_W)]

        def group(i):
            return zlo[i + pad] if i < 0 else (zhi[i - ni] if i >= ni else z3[i])

        def finish(acc, i0):
            u = _silu(acc)
            if c0 < 2 * qk:
                scale = HEAD_DK ** -0.5 if c0 < qk else 1.0
                parts = []
                for j in range(cw // HEAD_DK):
                    tt = u[:, :, j * HEAD_DK:(j + 1) * HEAD_DK]
                    parts.append(tt * (lax.rsqrt(jnp.sum(tt * tt, axis=-1, keepdims=True) + EPS)
                                       * scale))
                u = jnp.concatenate(parts, axis=-1)
            for c in range(cw // LANES):
                for i in range(u.shape[0]):
                    ys_ref[c, pl.ds(i0 + i, SUBLANES, stride=ni), :] = u[i][:, tile(c)]

        inner = ni - 2 * pad
        acc = taps[0] * z3[0:inner]
        for j in range(1, GD_CONV_W):
            acc = acc + taps[j] * z3[j:j + inner]
        finish(acc, pad)
        for i0 in (0, ni - pad):
            edge = jnp.stack([sum(taps[j][0] * group(i + j - pad) for j in range(GD_CONV_W))
                              for i in range(i0, i0 + pad)], axis=0)
            finish(edge, i0)

        def centre(lane0, width):
            return jnp.concatenate([ys_ref[c, halo:halo + tm] for c in
                                    range(lane0 // LANES, (lane0 + width) // LANES)], axis=1)

        if c0 < 2 * qk:
            dst, base = (q_ref, c0) if c0 < qk else (k_ref, c0 - qk)
            for j in range(cw // HEAD_DK):
                dst[0, base // HEAD_DK + j] = centre(j * HEAD_DK, HEAD_DK).astype(BF16)
        else:
            base = c0 - 2 * qk
            for j in range(cw // GD_DV):
                v_ref[0, base // GD_DV + j] = centre(j * GD_DV, GD_DV).astype(BF16)


def _gd_inproj(x_ctx, x_lat, modsel, g, w_in, conv_w, a_log, dt_bias, *, tm):
    b, n_ctx, d = x_ctx.shape
    n_lat = x_lat.shape[1]
    rpc = n_lat // GRID_W
    assert rpc % SUBLANES == 0 and tm % rpc == 0 and n_ctx % tm == 0 and n_ctx > 0
    t = n_ctx + n_lat
    nc = n_ctx // tm
    cpb = tm // rpc
    assert cpb % SUBLANES == 0, "halo blocks take one 8-column group of the grid"
    lat_v = x_lat.reshape(b, rpc, GRID_W, d)
    n_heads = a_log.shape[-1]
    qk = n_heads * HEAD_DK
    vd = n_heads * GD_DV
    nchan = 2 * qk + vd
    nblk = t // tm
    cw = min(GD_COL_CHUNK, qk)
    wqkv = w_in[:, :nchan].astype(BF16)
    wg = w_in[:, nchan:nchan + vd].astype(BF16)
    wab = jnp.pad(w_in[:, nchan + vd:], ((0, 0), (0, LANES - 4 * n_heads))).astype(BF16)
    alog = jnp.pad(a_log.reshape(1, 2 * n_heads), ((0, 0), (0, LANES - 2 * n_heads)))
    dtb = jnp.pad(dt_bias.reshape(1, 2 * n_heads), ((0, 0), (0, LANES - 2 * n_heads)))
    g2 = g.reshape(1, d)
    full = lambda a: pl.BlockSpec(a.shape, lambda i, j: (0,) * a.ndim)
    spb = tm // SUBLANES
    last_slab = n_ctx // SUBLANES - 1
    gpb = cpb // SUBLANES
    last_cg = GRID_W // SUBLANES - 1
    return pl.pallas_call(
        functools.partial(_gd_inproj_kernel, nc=nc, nblk=nblk, n_heads=n_heads, tm=tm, cw=cw,
                          rpc=rpc),
        grid=(b, nblk),
        in_specs=[pl.BlockSpec((1, SUBLANES, d),
                               lambda i, j: (i, jnp.clip(j * spb - 1, 0, last_slab), 0)),
                  pl.BlockSpec((1, tm, d), lambda i, j: (i, jnp.minimum(j, nc - 1), 0)),
                  pl.BlockSpec((1, SUBLANES, d),
                               lambda i, j: (i, jnp.clip((j + 1) * spb, 0, last_slab), 0)),
                  pl.BlockSpec((1, SUBLANES, SUBLANES, d),
                               lambda i, j: (i, rpc // SUBLANES - 1,
                                             jnp.clip((j - nc) * gpb - 1, 0, last_cg), 0)),
                  pl.BlockSpec((1, rpc, cpb, d), lambda i, j: (i, 0, jnp.maximum(j - nc, 0), 0)),
                  pl.BlockSpec((1, SUBLANES, SUBLANES, d),
                               lambda i, j: (i, 0, jnp.clip((j - nc + 1) * gpb, 0, last_cg), 0)),
                  pl.BlockSpec((1, 1, 6, d), lambda i, j: (i, jnp.where(j >= nc, 1, 0), 0, 0)),
                  full(g2), full(wqkv), full(wg), full(wab), full(conv_w), full(alog), full(dtb)],
        out_specs=[pl.BlockSpec((1, n_heads, tm, HEAD_DK), lambda i, j: (i, 0, j, 0)),
                   pl.BlockSpec((1, n_heads, tm, HEAD_DK), lambda i, j: (i, 0, j, 0)),
                   pl.BlockSpec((1, n_heads, tm, GD_DV), lambda i, j: (i, 0, j, 0)),
                   pl.BlockSpec((1, tm, vd), lambda i, j: (i, j, 0)),
                   pl.BlockSpec((1, tm, LANES), lambda i, j: (i, j, 0))],
        out_shape=[jax.ShapeDtypeStruct((b, n_heads, t, HEAD_DK), BF16),
                   jax.ShapeDtypeStruct((b, n_heads, t, HEAD_DK), BF16),
                   jax.ShapeDtypeStruct((b, n_heads, t, GD_DV), BF16),
                   jax.ShapeDtypeStruct((b, t, vd), BF16),
                   jax.ShapeDtypeStruct((b, t, LANES), F32)],
        scratch_shapes=[pltpu.VMEM((d // LANES, tm + 2 * SUBLANES, LANES), F32),
                        pltpu.VMEM((cw // LANES, tm + 2 * SUBLANES, LANES), F32)],
        compiler_params=_cp("parallel", "parallel"),
        name="gd_inproj",
    )(x_ctx, x_ctx, x_ctx, lat_v, lat_v, lat_v, modsel, g2, wqkv, wg, wab, conv_w, alog, dtb)


def _neumann_inverses(mats, eye):
    n = eye.shape[0]
    ts = [eye - a for a in mats]
    ps = []
    for a in mats:
        ab = a.astype(BF16)
        ps.append(_dot(ab, ab))
    yield
    lvl = 2
    while lvl < CHUNK:
        for i in range(len(mats)):
            pb = ps[i].astype(BF16)
            tb = ts[i].astype(BF16)
            if 2 * lvl >= CHUNK:
                ts[i] = ts[i] + _dot(pb, tb)
            else:
                out = _dot(pb, jnp.concatenate([pb, tb], axis=1))
                ps[i] = out[:, :n]
                ts[i] = ts[i] + out[:, n:]
        yield
        lvl *= 2
    return ts


def _gdn_scan_block(q_ref, k_ref, v_ref, ab_ref, o_ref, s_ref, u_ref, l1_ref, l2_ref,
                    *, rev, n_heads, nck, tm):
    d = 1 if rev else 0

    def masks(n):
        r = _iota((n, n), 0)
        c = _iota((n, n), 1)
        same = (r // CHUNK) == (c // CHUNK)
        return (same & ((c >= r) if rev else (c <= r)), same & ((c > r) if rev else (c < r)),
                jnp.where(r == c, 1.0, 0.0))

    pb = min(tm, GD_SOLVE_ROWS)
    incl, strict, eye = masks(pb)
    dv = v_ref.shape[-1]

    ab = ab_ref[0]
    g3 = _dot(jnp.where(masks(tm)[0], 1.0, 0.0).astype(BF16),
              jnp.concatenate(_split3(ab), axis=1))
    gcol = g3[:, :LANES] + g3[:, LANES:2 * LANES] + g3[:, 2 * LANES:]
    eye_l = jnp.where(_iota((LANES, LANES), 0) == _iota((LANES, LANES), 1), 1.0, 0.0).astype(BF16)
    gr3 = _dot_nt(eye_l, jnp.concatenate(_split3(gcol), axis=0))
    grow = gr3[:, :tm] + gr3[:, tm:2 * tm] + gr3[:, 2 * tm:]

    incl_c, _ = _chunk_masks(rev)
    wide = lambda x, n: jnp.concatenate([x] * (n // LANES), axis=1) if n > LANES else x[:, :n]
    heads = range(n_heads)
    cas = [d * n_heads + h for h in heads]
    gcbs = [jnp.broadcast_to(gcol[:, ca:ca + 1], (tm, LANES)) for ca in cas]
    bcbs = [jnp.broadcast_to(ab[:, 2 * n_heads + ca:2 * n_heads + ca + 1], (tm, LANES))
            for ca in cas]

    mats = []
    for h in heads:
        k = k_ref[0, h]
        for p0 in range(0, tm, pb):
            gam = jnp.where(incl, jnp.exp(jnp.minimum(
                wide(gcbs[h][p0:p0 + pb], pb) - grow[cas[h]:cas[h] + 1, p0:p0 + pb], 0.0)), 0.0)
            kp = k[p0:p0 + pb]
            mats.append(jnp.where(strict, wide(bcbs[h][p0:p0 + pb], pb) * _dot_nt(kp, kp) * gam,
                                  0.0))
    yield
    tinvs = yield from _neumann_inverses(mats, eye)

    uws = []
    for h in heads:
        kf = k_ref[0, h].astype(F32)
        rhs = jnp.concatenate(
            [(wide(bcbs[h], dv) * v_ref[0, h].astype(F32)).astype(BF16),
             (bcbs[h] * jnp.exp(gcbs[h]) * kf).astype(BF16)], axis=1)
        uws.append(jnp.concatenate(
            [_dot(tinvs[h * (tm // pb) + i].astype(BF16), rhs[i * pb:(i + 1) * pb])
             for i in range(tm // pb)], axis=0))
    yield

    egl = {}
    for h in heads:
        ca, gcb, uw = cas[h], gcbs[h], uws[h]
        q = q_ref[0, h]
        k = k_ref[0, h]
        kf = k.astype(F32)
        egb = jnp.exp(gcb)
        u_ref[h] = uw[:, :dv]
        wb = uw[:, dv:].astype(BF16)
        qg = (q.astype(F32) * egb).astype(BF16)
        gls = [gcb[ci * CHUNK:ci * CHUNK + 1] if rev
               else gcb[(ci + 1) * CHUNK - 1:(ci + 1) * CHUNK] for ci in range(nck)]
        glb = jnp.concatenate([jnp.broadcast_to(g, (CHUNK, LANES)) for g in gls], axis=0)
        kd = kf * jnp.exp(glb - gcb)
        egl[h] = [jnp.exp(g[:, 0:1]) for g in gls]
        for ci in range(nck):
            c0 = ci * CHUNK
            l1_ref[h, ci, :CHUNK] = wb[c0:c0 + CHUNK]
            l1_ref[h, ci, CHUNK:] = qg[c0:c0 + CHUNK]
            gam_c = jnp.where(incl_c, jnp.exp(jnp.minimum(
                gcb[c0:c0 + CHUNK, :CHUNK] - grow[ca:ca + 1, c0:c0 + CHUNK], 0.0)), 0.0)
            qk_c = _dot_nt(q[c0:c0 + CHUNK], k[c0:c0 + CHUNK])
            l2_ref[h, ci, :CHUNK] = (qk_c * gam_c).astype(BF16)
            l2_ref[h, ci, CHUNK:] = kd[c0:c0 + CHUNK].T.astype(BF16)

    for step in range(nck):
        yield
        ci = nck - 1 - step if rev else step
        rows = pl.ds(ci * CHUNK, CHUNK)
        r1s = [_dot(l1_ref[h, ci], s_ref[h].astype(BF16)) for h in heads]
        weffs = [(u_ref[h, rows, :] - r1s[h][:CHUNK]).astype(BF16) for h in heads]
        yield
        r2s = [_dot(l2_ref[h, ci], weffs[h]) for h in heads]
        for h in heads:
            o_ref[0, h, rows, :] = (r2s[h][:CHUNK] + r1s[h][CHUNK:]).astype(o_ref.dtype)
            s_ref[h] = s_ref[h] * egl[h][ci] + r2s[h][CHUNK:]


def _gdn_scan_kernel(*refs, n_heads, nck, tm):
    fwd = refs[0:4] + refs[8:9] + refs[10:14]
    bwd = refs[4:8] + refs[9:10] + refs[14:18]

    @pl.when(pl.program_id(1) == 0)
    def _():
        for s_ref in (fwd[5], bwd[5]):
            s_ref[...] = jnp.zeros(s_ref.shape, F32)

    blocks = [_gdn_scan_block(*fwd, rev=False, n_heads=n_heads, nck=nck, tm=tm),
              _gdn_scan_block(*bwd, rev=True, n_heads=n_heads, nck=nck, tm=tm)]
    for _ in itertools.zip_longest(*blocks):
        pass


def _gdn_scan(q, k, v, ab, *, nc, tm, out_dtype):
    b, n_heads, t, dk = q.shape
    dv = v.shape[-1]
    nblk = t // tm
    nl = nblk - nc
    nck = tm // CHUNK

    def specs(rev):
        blk = lambda s: _scan_block_index(s, nc, nl, rev)
        return [pl.BlockSpec((1, n_heads, tm, dk), lambda i, s: (i, 0, blk(s), 0)),
                pl.BlockSpec((1, n_heads, tm, dk), lambda i, s: (i, 0, blk(s), 0)),
                pl.BlockSpec((1, n_heads, tm, dv), lambda i, s: (i, 0, blk(s), 0)),
                pl.BlockSpec((1, tm, LANES), lambda i, s: (i, blk(s), 0))]

    scratch = [pltpu.VMEM((n_heads, dk, dv), F32),
               pltpu.VMEM((n_heads, tm, dv), F32),
               pltpu.VMEM((n_heads, nck, 2 * CHUNK, dk), BF16),
               pltpu.VMEM((n_heads, nck, CHUNK + dk, CHUNK), BF16)]
    o_shape = jax.ShapeDtypeStruct((b, n_heads, t, dv), out_dtype)
    return pl.pallas_call(
        functools.partial(_gdn_scan_kernel, n_heads=n_heads, nck=nck, tm=tm),
        grid=(b, nblk),
        in_specs=specs(False) + specs(True),
        out_specs=[specs(False)[2], specs(True)[2]],
        out_shape=[o_shape, o_shape],
        scratch_shapes=scratch + scratch,
        compiler_params=_cp("parallel", "arbitrary"),
        name="gdn_scan",
    )(q, k, v, ab, q, k, v, ab)


def _route_kernel(x_ref, mod_ref, g_ref, wrt_ref, h_ref, slot_ref, aff_ref, lg_ref,
                  *, n_ctx, cap_c, cap_l, tr):
    n_exp, t = lg_ref.shape
    for rt in range(t // tr):
        m = mod_ref[0, 0 if rt * tr < n_ctx else 1]
        rows = slice(rt * tr, (rt + 1) * tr)
        h = _norm_mod(x_ref[0, rows, :], g_ref[...], m[3:4], m[4:5])
        h_hi, h_mid, _ = _split3(h)
        h_ref[0, rows, :] = h_hi
        w_hi, w_mid, _ = _split3(wrt_ref[...])
        lg2 = _dot_nt(jnp.concatenate([w_hi, w_mid], axis=0), h_hi)
        lg_ref[:, rows] = lg2[:n_exp] + lg2[n_exp:] + _dot_nt(w_hi, h_mid)
    lg = lg_ref[...]
    e = jnp.exp(lg - jnp.max(lg, axis=0, keepdims=True))
    aff = e / jnp.sum(e, axis=0, keepdims=True)
    aff_ref[0] = aff
    bits = lax.bitcast_convert_type(aff, jnp.int32)
    lane = _iota((n_exp, t), 1)

    if n_ctx:
        is_ctx = lane < n_ctx
        regions = [(is_ctx, cap_c), (jnp.logical_not(is_ctx), cap_l)]
    else:
        regions = [(None, cap_l)]

    def count(pred, mask):
        p = pred if mask is None else (pred & mask)
        return jnp.sum(jnp.where(p, 1.0, 0.0), axis=1, keepdims=True)

    def thr_body(i, thrs):
        bit = lax.shift_left(jnp.int32(1), AFF_BITS - 1 - i)
        out = []
        for (mask, cap), thr in zip(regions, thrs):
            cand = thr | bit
            out.append(jnp.where(count(bits >= cand, mask) >= cap, cand, thr))
        return tuple(out)

    zero = jnp.zeros((n_exp, 1), jnp.int32)
    thrs = lax.fori_loop(0, AFF_BITS, thr_body, tuple(zero for _ in regions))

    idx_bits = t.bit_length()
    sel = None
    for (mask, cap), thr in zip(regions, thrs):
        gt = bits > thr
        tie = bits == thr
        need = cap - count(gt, mask)

        def j_body(i, j, tie=tie, mask=mask, need=need):
            cand = j | lax.shift_left(jnp.int32(1), idx_bits - 1 - i)
            return jnp.where(count(tie & (lane < cand), mask) <= need, cand, j)

        jmax = lax.fori_loop(0, idx_bits, j_body, zero)
        s = gt | (tie & (lane < jmax))
        if mask is not None:
            s = s & mask
        sel = s if sel is None else (sel | s)

    lt = LANES if t % LANES == 0 else CHUNK
    ut = jnp.where(_iota((lt, lt), 0) < _iota((lt, lt), 1), 1.0, 0.0).astype(BF16)
    self = jnp.where(sel, 1.0, 0.0)
    run = jnp.zeros((n_exp, 1), F32)
    pres = []
    for i in range(t // lt):
        tile = self[:, i * lt:(i + 1) * lt]
        pres.append(_dot(tile.astype(BF16), ut) + run)
        run = run + jnp.sum(tile, axis=1, keepdims=True)
    pre = jnp.concatenate(pres, axis=1)
    if n_ctx:
        slot = jnp.where(is_ctx, pre + cap_l, pre - cap_c)
    else:
        slot = pre
    slot_ref[0] = jnp.where(sel, slot, -1.0)


def _route(x, modsel, g, w_router, *, n_ctx, cap_c, cap_l, tr):
    b, t, d = x.shape
    n_exp = w_router.shape[-1]
    wrt = w_router.T
    g2 = g.reshape(1, d)
    return pl.pallas_call(
        functools.partial(_route_kernel, n_ctx=n_ctx, cap_c=cap_c, cap_l=cap_l, tr=tr),
        grid=(b,),
        in_specs=[pl.BlockSpec((1, t, d), lambda i: (i, 0, 0)),
                  pl.BlockSpec((1, 2, 6, d), lambda i: (i, 0, 0, 0)),
                  pl.BlockSpec((1, d), lambda i: (0, 0)),
                  pl.BlockSpec((n_exp, d), lambda i: (0, 0))],
        out_specs=[pl.BlockSpec((1, t, d), lambda i: (i, 0, 0)),
                   pl.BlockSpec((1, n_exp, t), lambda i: (i, 0, 0)),
                   pl.BlockSpec((1, n_exp, t), lambda i: (i, 0, 0))],
        out_shape=[jax.ShapeDtypeStruct((b, t, d), BF16),
                   jax.ShapeDtypeStruct((b, n_exp, t), F32),
                   jax.ShapeDtypeStruct((b, n_exp, t), F32)],
        scratch_shapes=[pltpu.VMEM((n_exp, t), F32)],
        compiler_params=_cp("parallel"),
        name="moe_route",
    )(x, modsel, g2, wrt)


def _gather_kernel(slot_ref, h_ref, x_ref, *, n_ctx, cap_c, cap_l):
    slot = slot_ref[0, 0]
    t = slot.shape[1]
    sl = slot[:, n_ctx:]
    p = jnp.where(sl == _iota((cap_l, t - n_ctx), 0).astype(F32), 1.0, 0.0).astype(BF16)
    x_ref[0, 0, 0:cap_l, :] = _dot(p, h_ref[0, n_ctx:, :]).astype(BF16)
    if n_ctx:
        sc = slot[:, :n_ctx] - float(cap_l)
        p = jnp.where(sc == _iota((cap_c, n_ctx), 0).astype(F32), 1.0, 0.0).astype(BF16)
        x_ref[0, 0, cap_l:, :] = _dot(p, h_ref[0, :n_ctx, :]).astype(BF16)


def _gather(h, slot_row, *, n_ctx, cap_c, cap_l):
    b, t, d = h.shape
    n_exp = slot_row.shape[1]
    m = cap_l + (cap_c if n_ctx else 0)
    return pl.pallas_call(
        functools.partial(_gather_kernel, n_ctx=n_ctx, cap_c=cap_c, cap_l=cap_l),
        grid=(b, n_exp),
        in_specs=[pl.BlockSpec((1, 1, 1, t), lambda i, e: (i, e, 0, 0)),
                  pl.BlockSpec((1, t, d), lambda i, e: (i, 0, 0))],
        out_specs=pl.BlockSpec((1, 1, m, d), lambda i, e: (e, i, 0, 0)),
        out_shape=jax.ShapeDtypeStruct((n_exp, b, m, d), BF16),
        compiler_params=_cp("parallel", "parallel"),
        name="moe_gather",
    )(slot_row.reshape(b, n_exp, 1, t), h)


def _ffn_kernel(x_ref, wg_ref, wu_ref, wd_ref, y_ref, acc_ref, wgb_ref, wub_ref, wdb_ref, *, tr):
    ft = pl.program_id(1)

    @pl.when(ft == 0)
    def _():
        acc_ref[...] = jnp.zeros(acc_ref.shape, F32)

    wgb_ref[...] = wg_ref[...].astype(BF16)
    wub_ref[...] = wu_ref[...].astype(BF16)
    wdb_ref[...] = wd_ref[...].astype(BF16)

    def body(i, carry):
        rows = pl.ds(pl.multiple_of(i * tr, tr), tr)
        x = x_ref[0, rows, :]
        hid = (_silu(_dot(x, wgb_ref[...])) * _dot(x, wub_ref[...])).astype(BF16)
        acc_ref[rows, :] += _dot(hid, wdb_ref[...])
        return carry

    lax.fori_loop(0, x_ref.shape[1] // tr, body, 0)

    @pl.when(ft == pl.num_programs(1) - 1)
    def _():
        y_ref[0] = acc_ref[...].astype(y_ref.dtype)


def _ffn(xs, w_gate, w_up, w_down, *, layer):
    n_exp, bm, d = xs.shape
    f = w_gate.shape[-1]
    tf = FFN_TILE if f % FFN_TILE == 0 else f
    tr = bm // (-(-bm // FFN_MAX_ROWS))
    return pl.pallas_call(
        functools.partial(_ffn_kernel, tr=tr),
        grid=(n_exp, f // tf),
        in_specs=[pl.BlockSpec((1, bm, d), lambda e, j: (e, 0, 0)),
                  pl.BlockSpec((None, None, d, tf), lambda e, j: (layer, e, 0, j)),
                  pl.BlockSpec((None, None, d, tf), lambda e, j: (layer, e, 0, j)),
                  pl.BlockSpec((None, None, tf, d), lambda e, j: (layer, e, j, 0))],
        out_specs=pl.BlockSpec((1, bm, d), lambda e, j: (e, 0, 0)),
        out_shape=jax.ShapeDtypeStruct((n_exp, bm, d), BF16),
        scratch_shapes=[pltpu.VMEM((bm, d), F32),
                        pltpu.VMEM((d, tf), BF16),
                        pltpu.VMEM((d, tf), BF16),
                        pltpu.VMEM((tf, d), BF16)],
        compiler_params=_cp("parallel", "arbitrary"),
        name="moe_ffn",
    )(xs, w_gate, w_up, w_down)


def _combine_kernel(*refs, nc, cap_c, cap_l, final):
    x_ref, ys_ref, sc_ref, ac_ref, mod_ref = refs[:5]
    fg_ref = refs[5] if final else None
    outs = refs[5 + bool(final):-1]
    acc_ref = refs[-1]
    n_exp = ys_ref.shape[0]
    tm = x_ref.shape[1]

    def run(k0, kk, o_ref):
        sc = sc_ref[0]
        ac = ac_ref[0]
        lane = (_iota((tm, kk), 1) + k0).astype(F32)
        acc_ref[...] = jnp.zeros(acc_ref.shape, F32)
        for e in range(n_exp):
            pt = jnp.where(sc[:, e:e + 1] == lane, 1.0, 0.0).astype(BF16)
            acc_ref[...] += ac[:, e:e + 1] * _dot(pt, ys_ref[e, 0, k0:k0 + kk, :])
        x2 = x_ref[0] + mod_ref[0, 0][5:6] * acc_ref[...]
        if final:
            x2 = x2 * lax.rsqrt(jnp.mean(x2 * x2, axis=-1, keepdims=True) + EPS) * fg_ref[...]
        o_ref[0] = x2

    if nc:
        t = pl.program_id(1)

        @pl.when(t < nc)
        def _():
            run(cap_l, cap_c, outs[0])

        @pl.when(t >= nc)
        def _():
            run(0, cap_l, outs[1])
    else:
        run(0, cap_l, outs[0])


def _combine(x, ys, slot_col, aff_col, modsel, final_g, *, nc, cap_c, cap_l, tm):
    b, t, d = x.shape
    n_exp, _, m, _ = ys.shape
    seg = (lambda j: jnp.where(j >= nc, 1, 0)) if nc else (lambda j: 1)
    if nc:
        out_specs = [pl.BlockSpec((1, tm, d), lambda i, j: (i, jnp.minimum(j, nc - 1), 0)),
                     pl.BlockSpec((1, tm, d), lambda i, j: (i, jnp.maximum(j - nc, 0), 0))]
        out_shape = [jax.ShapeDtypeStruct((b, nc * tm, d), F32),
                     jax.ShapeDtypeStruct((b, t - nc * tm, d), F32)]
    else:
        out_specs = pl.BlockSpec((1, tm, d), lambda i, j: (i, j, 0))
        out_shape = jax.ShapeDtypeStruct((b, t, d), F32)
    in_specs = [pl.BlockSpec((1, tm, d), lambda i, j: (i, j, 0)),
                pl.BlockSpec((n_exp, 1, m, d), lambda i, j: (0, i, 0, 0)),
                pl.BlockSpec((1, tm, n_exp), lambda i, j: (i, j, 0)),
                pl.BlockSpec((1, tm, n_exp), lambda i, j: (i, j, 0)),
                pl.BlockSpec((1, 1, 6, d), lambda i, j: (i, seg(j), 0, 0))]
    args = [x, ys, slot_col, aff_col, modsel]
    if final_g is not None:
        in_specs.append(pl.BlockSpec((1, d), lambda i, j: (0, 0)))
        args.append(final_g.reshape(1, d))
    return pl.pallas_call(
        functools.partial(_combine_kernel, nc=nc, cap_c=cap_c, cap_l=cap_l,
                          final=final_g is not None),
        grid=(b, t // tm),
        in_specs=in_specs,
        out_specs=out_specs,
        out_shape=out_shape,
        scratch_shapes=[pltpu.VMEM((tm, d), F32)],
        compiler_params=_cp("parallel", "arbitrary"),
        name="moe_combine",
    )(*args)


def _moe(x, modsel, g, w_router, w_gate, w_up, w_down, final_g, *, layer, n_ctx, n_lat, tm):
    b, t, d = x.shape
    n_exp = w_router.shape[-1]
    cap_l = EC_CAPACITY_FACTOR * n_lat // n_exp
    cap_c = EC_CAPACITY_FACTOR * n_ctx // n_exp
    h, slot_row, aff_row = _route(x, modsel, g, w_router, n_ctx=n_ctx, cap_c=cap_c, cap_l=cap_l,
                                  tr=tm)
    xs = _gather(h, slot_row, n_ctx=n_ctx, cap_c=cap_c, cap_l=cap_l)
    m = xs.shape[2]
    ys = _ffn(xs.reshape(n_exp, b * m, d), w_gate, w_up, w_down, layer=layer)
    return _combine(x, ys.reshape(n_exp, b, m, d), jnp.swapaxes(slot_row, 1, 2),
                    jnp.swapaxes(aff_row, 1, 2), modsel, final_g,
                    nc=n_ctx // tm, cap_c=cap_c, cap_l=cap_l, tm=tm)


def kernel(x, c, ctx, c_ctx, w_mod, b_mod, norm_g, hg_w_in, hg_lb, hg_onorm, hg_w_out, gd_w_in,
           gd_conv, gd_a_log, gd_dt_bias, gd_onorm, gd_w_out, moe_router, moe_w_gate, moe_w_up,
           moe_w_down, final_g):
    bsz, n_lat, d = x.shape
    n_ctx = ctx.shape[1]
    depth = w_mod.shape[0]
    assert depth == 2, "layer 0 = HGRN2, layer 1 = gated DeltaNet"
    tm = math.gcd(math.gcd(n_ctx, n_lat), MAX_BLOCK_ROWS)
    assert tm % CHUNK == 0
    nc = n_ctx // tm
    o_dt = BF16

    rows = -(-(bsz + 1) // SUBLANES) * SUBLANES
    cond = jnp.zeros((rows, d), F32).at[:bsz].set(c).at[bsz].set(c_ctx)
    mod = _adaln(cond, w_mod, b_mod).reshape(depth, rows, 6, d)

    def modsel(i):
        ctx_mod = jnp.broadcast_to(mod[i, bsz][None], (bsz, 6, d))
        return jnp.stack([ctx_mod, mod[i, :bsz]], axis=1)

    ms = modsel(0)
    q, lf, v, gate = _hg_inproj(ctx, x, ms, norm_g[0, 0], hg_lb, hg_w_in[0], layer=0, nc=nc, tm=tm)
    o_f, o_b = _gla_scan(q, lf, v, nc=nc, tm=tm, out_dtype=o_dt)
    x_all = _outproj(o_f, o_b, gate, hg_onorm[0], hg_w_out[0], ctx, x, ms, nc=nc, tm=tm)
    x_ctx, x_lat = _moe(x_all, ms, norm_g[0, 1], moe_router[0], moe_w_gate, moe_w_up, moe_w_down,
                        None, layer=0, n_ctx=n_ctx, n_lat=n_lat, tm=tm)

    ms = modsel(1)
    rpc = n_lat // GRID_W
    q, k, v, gate, ab = _gd_inproj(x_ctx, x_lat, ms, norm_g[1, 0], gd_w_in[0], gd_conv[0],
                                   gd_a_log[0], gd_dt_bias[0], tm=tm)
    o_f, o_b = _gdn_scan(q, k, v, ab, nc=nc, tm=tm, out_dtype=o_dt)
    x_lat = _outproj(o_f, o_b, gate, gd_onorm[0], gd_w_out[0], x_ctx,
                     x_lat.reshape(bsz, rpc, GRID_W, d), ms, nc=nc, tm=tm, rpc=rpc)
    x_lat = x_lat.reshape(bsz, n_lat, d)
    return _moe(x_lat, ms, norm_g[1, 1], moe_router[1], moe_w_gate, moe_w_up, moe_w_down, final_g,
                layer=1, n_ctx=0, n_lat=n_lat, tm=tm)
```
